```python
import math
import jax, jax.numpy as jnp
from jax import lax
import numpy as np

D_MODEL = 1024
BATCH = 16
SEQ = 2048
DEPTH = 2
DEC_BATCH = 16
DEC_SEQ = 4096
PAST_LEN = 128

GRID_W = 64
GLA_HEADS = 4
GLA_DK = D_MODEL // 8
GLA_DV = D_MODEL // 4
GLA_RANK = 16
GLA_TAU = 16.0
GLA_CHUNK = 64
NA_HEADS = 16
NA_DH = D_MODEL // NA_HEADS
NA_KR = 8
NA_KC = 16
NA_QB = 16
NA_KB = 32
DIFF_HEADS = 8
DIFF_D = D_MODEL // (2 * DIFF_HEADS)
DIFF_QBLOCK = 128
ROPE_THETA = 10000.0
PEER_HEADS = 8
PEER_NKEYS = 128
PEER_N = PEER_NKEYS * PEER_NKEYS
PEER_DKEY = 128
PEER_TOPK = 16
PEER_TOKBLOCK = 128
N_BRANCH = 3
BRANCH_W = D_MODEL
DN_ALPHA = (2 * DEPTH) ** 0.25
DN_BETA = (8 * DEPTH) ** -0.25
LN_EPS = 1e-5
IN_SIZES = (GLA_HEADS * GLA_DK, GLA_HEADS * GLA_DK, GLA_HEADS * GLA_DV, GLA_HEADS * GLA_DV, 2 * GLA_RANK,
            NA_HEADS * NA_DH, NA_HEADS * NA_DH, NA_HEADS * NA_DH,
            DIFF_HEADS * 2 * DIFF_D, DIFF_HEADS * 2 * DIFF_D, DIFF_HEADS * 2 * DIFF_D,
            N_BRANCH * D_MODEL)
IN_COLS = int(sum(IN_SIZES))

kernel_name = 'hybrid_gla_natten_diff_peer_encoder'


def layer_norm(x, g, b):
    xf = x.astype(jnp.float32)
    mu = xf.mean(-1, keepdims=True)
    var = jnp.square(xf - mu).mean(-1, keepdims=True)
    return ((xf - mu) * lax.rsqrt(var + LN_EPS) * g + b).astype(x.dtype)


def head_rms_norm(x, g):
    xf = x.astype(jnp.float32)
    y = xf * lax.rsqrt(jnp.mean(jnp.square(xf), -1, keepdims=True) + LN_EPS) * g
    return y.astype(x.dtype)


def to_heads(a, n_heads):
    b, t, _ = a.shape
    return a.reshape(b, t, n_heads, -1).transpose(0, 2, 1, 3)


def from_heads(a):
    b, h, t, d = a.shape
    return a.transpose(0, 2, 1, 3).reshape(b, t, h * d)


def rope(x):
    t, d = x.shape[-2], x.shape[-1]
    half = d // 2
    inv = jnp.exp(-math.log(ROPE_THETA) * jnp.arange(half, dtype=jnp.float32) / half)
    ang = jnp.arange(t, dtype=jnp.float32)[:, None] * inv[None, :]
    cos, sin = jnp.cos(ang).astype(x.dtype), jnp.sin(ang).astype(x.dtype)
    x1, x2 = x[..., :half], x[..., half:]
    return jnp.concatenate([x1 * cos - x2 * sin, x2 * cos + x1 * sin], axis=-1)


def gla_chunk_scan(q, k, v, log_a, strict):
    bsz, nh, t, dk = q.shape
    dv = v.shape[-1]
    n = t // GLA_CHUNK

    def chunks(a):
        return jnp.moveaxis(a.astype(jnp.float32).reshape(bsz, nh, n, GLA_CHUNK, a.shape[-1]), 2, 0)

    ti = jnp.arange(GLA_CHUNK)
    mask = (ti[:, None] > ti[None, :]) if strict else (ti[:, None] >= ti[None, :])

    def step(s_state, inp):
        qc, kc, vc, gc = inp
        bcum = jnp.cumsum(gc, axis=-2)
        inter = jnp.einsum('bhtk,bhkv->bhtv', qc * jnp.exp(bcum), s_state)
        rel = jnp.exp(jnp.minimum(bcum[..., :, None, :] - bcum[..., None, :, :], 0.0))
        rel = jnp.where(mask[:, :, None], rel, 0.0)
        att = jnp.einsum('bhtk,bhsk,bhtsk->bhts', qc, kc, rel)
        intra = jnp.einsum('bhts,bhsv->bhtv', att, vc)
        s_state = s_state * jnp.exp(bcum[..., -1, :])[..., None] + jnp.einsum(
            'bhsk,bhsv->bhkv', kc * jnp.exp(bcum[..., -1:, :] - bcum), vc)
        return s_state, inter + intra

    s0 = jnp.zeros((bsz, nh, dk, dv), jnp.float32)
    _, out = lax.scan(step, s0, (chunks(q), chunks(k), chunks(v), chunks(log_a)))
    return jnp.moveaxis(out, 0, 2).reshape(bsz, nh, t, dv)


def gla_branch(q, k, v, r, glr, w_gk, b_gk, norm_g):
    qh = to_heads(q, GLA_HEADS) * (GLA_DK ** -0.5)
    kh = to_heads(k, GLA_HEADS)
    vh = to_heads(v, GLA_HEADS)
    la_f = to_heads(jax.nn.log_sigmoid(glr[..., :GLA_RANK] @ w_gk[0] + b_gk[0]) / GLA_TAU, GLA_HEADS)
    la_b = to_heads(jax.nn.log_sigmoid(glr[..., GLA_RANK:] @ w_gk[1] + b_gk[1]) / GLA_TAU, GLA_HEADS)
    fwd = gla_chunk_scan(qh, kh, vh, la_f, False)
    flip = lambda a: jnp.flip(a, axis=2)
    bwd = flip(gla_chunk_scan(flip(qh), flip(kh), flip(vh), flip(la_b), True))
    o = head_rms_norm(fwd + bwd, norm_g)
    return from_heads(o).astype(r.dtype) * jax.nn.silu(r)


def na_geometry():
    nj = GRID_W // NA_QB
    starts = np.clip(np.arange(nj) * NA_QB - NA_KC // 2, 0, GRID_W - NA_KB)
    col_idx = starts[:, None] + np.arange(NA_KB)[None, :]
    qcol = np.arange(nj)[:, None] * NA_QB + np.arange(NA_QB)[None, :]
    cs = np.clip(qcol - NA_KC // 2, 0, GRID_W - NA_KC)
    kc = col_idx[:, None, :]
    col_mask = (kc >= cs[..., None]) & (kc < cs[..., None] + NA_KC)
    dc_idx = np.clip(kc - qcol[..., None] + NA_KC - 1, 0, 2 * NA_KC - 2)
    return col_idx, col_mask, dc_idx


def na_branch(q, k, v, rpb):
    bsz, t, _ = q.shape
    rows = t // GRID_W
    kr = min(NA_KR, rows)

    def grid(a):
        return a.reshape(bsz, rows, GRID_W, NA_HEADS, NA_DH).transpose(0, 3, 1, 2, 4)

    qg = grid(q) * (NA_DH ** -0.5)
    kg = grid(k)
    vg = grid(v)
    col_idx, col_mask, dc_idx = na_geometry()
    nj = col_idx.shape[0]
    mask = jnp.asarray(col_mask)[:, :, None, :]

    def row(r):
        rs = jnp.clip(r - kr // 2, 0, rows - kr)
        qr = lax.dynamic_index_in_dim(qg, r, axis=2, keepdims=False).reshape(bsz, NA_HEADS, nj, NA_QB, NA_DH)
        kb = lax.dynamic_slice_in_dim(kg, rs, kr, axis=2)[:, :, :, col_idx]
        vb = lax.dynamic_slice_in_dim(vg, rs, kr, axis=2)[:, :, :, col_idx]
        dr_idx = rs + jnp.arange(kr) - r + NA_KR - 1
        bias = rpb[:, dr_idx[:, None, None, None], dc_idx[None]]
        bias = bias.transpose(0, 2, 3, 1, 4).astype(jnp.float32)
        s = jnp.einsum('bhjqd,bhijmd->bhjqim', qr, kb).astype(jnp.float32) + bias
        s = jnp.where(mask, s, -jnp.inf)
        p = jax.nn.softmax(s.reshape(s.shape[:-2] + (kr * NA_KB,)), axis=-1).reshape(s.shape)
        o = jnp.einsum('bhjqim,bhijmd->bhjqd', p.astype(vb.dtype), vb)
        return o.reshape(bsz, NA_HEADS, GRID_W, NA_DH)

    out = lax.map(row, jnp.arange(rows))
    return out.transpose(1, 0, 3, 2, 4).reshape(bsz, t, NA_HEADS * NA_DH)


def diff_branch(q, k, v, lam_vec, norm_g, lam_init):
    bsz, t, _ = q.shape

    def qk_heads(a):
        return rope(a.reshape(bsz, t, DIFF_HEADS, 2, DIFF_D).transpose(0, 2, 3, 1, 4))

    qh = qk_heads(q) * (DIFF_D ** -0.5)
    kh = qk_heads(k)
    vh = to_heads(v, DIFF_HEADS)
    lv = lam_vec.astype(jnp.float32)
    lam = jnp.exp(jnp.sum(lv[0] * lv[1])) - jnp.exp(jnp.sum(lv[2] * lv[3])) + lam_init
    nb = t // DIFF_QBLOCK
    qb = jnp.moveaxis(qh.reshape(bsz, DIFF_HEADS, 2, nb, DIFF_QBLOCK, DIFF_D), 3, 0)

    def block(qblk):
        s = jnp.einsum('bhpqd,bhpkd->bhpqk', qblk, kh).astype(jnp.float32)
        p = jax.nn.softmax(s, axis=-1)
        a = p[:, :, 0] - lam * p[:, :, 1]
        return jnp.einsum('bhqk,bhkv->bhqv', a.astype(vh.dtype), vh)

    o = lax.map(block, qb)
    o = jnp.moveaxis(o, 0, 2).reshape(bsz, DIFF_HEADS, t, 2 * DIFF_D)
    o = head_rms_norm(o, norm_g) * (1.0 - lam_init)
    return from_heads(o)


def mixer(h, l, p):
    bsz, t, _ = h.shape
    proj = h @ p['w_in'][l]
    offs = [int(o) for o in np.cumsum(IN_SIZES)[:-1]]
    aq, ak, av, ar, ag, bq, bk, bv, cq, ck, cv, gt = jnp.split(proj, offs, axis=-1)
    ya = gla_branch(aq, ak, av, ar, ag, p['gla_w_gk'][l], p['gla_b_gk'][l], p['gla_norm_g'][l])
    yb = na_branch(bq, bk, bv, p['na_rpb'][l])
    yc = diff_branch(cq, ck, cv, p['diff_lambda'][l], p['diff_norm_g'][l], 0.8 - 0.6 * math.exp(-0.3 * l))
    ys = jnp.stack([ya, yb.astype(ya.dtype), yc.astype(ya.dtype)], axis=2)
    br = jnp.einsum('btne,ned->btnd', ys, p['w_br'][l])
    gates = jax.nn.sigmoid(gt).reshape(bsz, t, N_BRANCH, D_MODEL)
    merged = jnp.sum(gates * br, axis=2)
    return merged @ p['w_o'][l]


def peer_ffn(h, wq, subkeys, u_tab, v_tab):
    bsz, t, d = h.shape
    nb = (bsz * t) // PEER_TOKBLOCK
    half = PEER_DKEY // 2
    hb = h.reshape(nb, PEER_TOKBLOCK, d)

    def block(xb):
        q = (xb @ wq).reshape(PEER_TOKBLOCK, PEER_HEADS, 2, half)
        s = jnp.einsum('nhpd,hpkd->nhpk', q, subkeys).astype(jnp.float32)
        s1, i1 = lax.top_k(s[:, :, 0], PEER_TOPK)
        s2, i2 = lax.top_k(s[:, :, 1], PEER_TOPK)
        cand = (s1[..., :, None] + s2[..., None, :]).reshape(PEER_TOKBLOCK, PEER_HEADS, PEER_TOPK * PEER_TOPK)
        cidx = (i1[..., :, None] * PEER_NKEYS + i2[..., None, :]).reshape(PEER_TOKBLOCK, PEER_HEADS, PEER_TOPK * PEER_TOPK)
        top_s, pos = lax.top_k(cand, PEER_TOPK)
        eidx = jnp.take_along_axis(cidx, pos, axis=-1)
        g = jax.nn.softmax(top_s, axis=-1)
        act = jax.nn.gelu(jnp.einsum('nd,nhkd->nhk', xb, u_tab[eidx]).astype(jnp.float32), approximate=False)
        w = (g * act).astype(xb.dtype)
        return jnp.einsum('nhk,nhkd->nd', w, v_tab[eidx])

    return lax.map(block, hb).reshape(bsz, t, d)


def trunk(x, c, p):
    bsz = x.shape[0]
    for l in range(DEPTH):
        mod = (jax.nn.silu(c) @ p['w_ada'][l] + p['b_ada'][l]).reshape(bsz, 6, 1, D_MODEL)
        h = x * (1.0 + mod[:, 1]) + mod[:, 0]
        x = layer_norm(DN_ALPHA * x + mod[:, 2] * mixer(h, l, p), p['ln1_g'][l], p['ln1_b'][l])
        h = x * (1.0 + mod[:, 4]) + mod[:, 3]
        f = peer_ffn(h, p['peer_wq'][l], p['peer_subkeys'][l], p['peer_u'][l], p['peer_v'][l])
        x = layer_norm(DN_ALPHA * x + mod[:, 5] * f, p['ln2_g'][l], p['ln2_b'][l])
    return x


def setup_inputs(seed: int = 0) -> dict:
    key = jax.random.key(seed)
    ks = jax.random.split(key, 24)
    nrm = lambda k, shape: jax.random.normal(k, shape, jnp.float32)
    D = D_MODEL
    return {
        'x_prompt': nrm(ks[0], (BATCH, SEQ, D)),
        'x_sample': nrm(ks[1], (DEC_BATCH, DEC_SEQ, D)),
        'c_prompt': nrm(ks[2], (BATCH, D)),
        'c_sample': nrm(ks[3], (DEC_BATCH, D)),
        'w_ada': nrm(ks[4], (DEPTH, D, 6 * D)) * D ** -0.5,
        'b_ada': nrm(ks[5], (DEPTH, 6 * D)) * 0.01,
        'w_in': nrm(ks[6], (DEPTH, D, IN_COLS)) * D ** -0.5,
        'gla_w_gk': nrm(ks[7], (DEPTH, 2, GLA_RANK, GLA_HEADS * GLA_DK)) * GLA_RANK ** -0.5,
        'gla_b_gk': nrm(ks[8], (DEPTH, 2, GLA_HEADS * GLA_DK)) * 0.1,
        'gla_norm_g': 1.0 + 0.02 * nrm(ks[9], (DEPTH, GLA_DV)),
        'na_rpb': nrm(ks[10], (DEPTH, NA_HEADS, 2 * NA_KR - 1, 2 * NA_KC - 1)) * 0.02,
        'diff_lambda': nrm(ks[11], (DEPTH, 4, DIFF_D)) * 0.1,
        'diff_norm_g': 1.0 + 0.02 * nrm(ks[12], (DEPTH, 2 * DIFF_D)),
        'w_br': nrm(ks[13], (DEPTH, N_BRANCH, BRANCH_W, D)) * (BRANCH_W ** -0.5) * DN_BETA,
        'w_o': nrm(ks[14], (DEPTH, D, D)) * (D ** -0.5) * DN_BETA,
        'ln1_g': 1.0 + 0.02 * nrm(ks[15], (DEPTH, D)),
        'ln1_b': 0.02 * nrm(ks[16], (DEPTH, D)),
        'peer_wq': nrm(ks[17], (DEPTH, D, PEER_HEADS * PEER_DKEY)) * D ** -0.5,
        'peer_subkeys': nrm(ks[18], (DEPTH, PEER_HEADS, 2, PEER_NKEYS, PEER_DKEY // 2)) * (PEER_DKEY // 2) ** -0.5,
        'peer_u': nrm(ks[19], (DEPTH, PEER_N, D)) * D ** -0.5,
        'peer_v': nrm(ks[20], (DEPTH, PEER_N, D)) * DN_BETA * PEER_HEADS ** -0.5,
        'ln2_g': 1.0 + 0.02 * nrm(ks[21], (DEPTH, D)),
        'ln2_b': 0.02 * nrm(ks[22], (DEPTH, D)),
    }


def reference(x_prompt, x_sample, c_prompt, c_sample, w_ada, b_ada, w_in, gla_w_gk, gla_b_gk, gla_norm_g,
              na_rpb, diff_lambda, diff_norm_g, w_br, w_o, ln1_g, ln1_b, peer_wq, peer_subkeys, peer_u,
              peer_v, ln2_g, ln2_b):
    p = dict(w_ada=w_ada, b_ada=b_ada, w_in=w_in, gla_w_gk=gla_w_gk, gla_b_gk=gla_b_gk,
             gla_norm_g=gla_norm_g, na_rpb=na_rpb, diff_lambda=diff_lambda, diff_norm_g=diff_norm_g,
             w_br=w_br, w_o=w_o, ln1_g=ln1_g, ln1_b=ln1_b, peer_wq=peer_wq, peer_subkeys=peer_subkeys,
             peer_u=peer_u, peer_v=peer_v, ln2_g=ln2_g, ln2_b=ln2_b)
    y_prompt = trunk(x_prompt, c_prompt, p)
    y_sample = trunk(x_sample, c_sample, p)
    return (y_prompt, y_sample)
```

```python
import functools
import math

import numpy as np
import jax
import jax.numpy as jnp
from jax import lax
from jax.experimental import pallas as pl
from jax.experimental.pallas import tpu as pltpu

F32 = jnp.float32
BF16 = jnp.bfloat16

D_MODEL = 1024
DEPTH = 2
GRID_W = 64
GLA_HEADS, GLA_DK, GLA_DV, GLA_RANK, GLA_TAU, GLA_CHUNK = 4, 128, 256, 16, 16.0, 64
GLA_SUB = 16
NA_HEADS, NA_DH, NA_KR, NA_KC = 16, 64, 8, 16
DIFF_HEADS, DIFF_D = 8, 64
ROPE_THETA = 10000.0
PEER_HEADS, PEER_NKEYS, PEER_DKEY, PEER_TOPK = 8, 128, 128, 16
PEER_N = PEER_NKEYS * PEER_NKEYS
DN_ALPHA = (2 * DEPTH) ** 0.25
LN_EPS = 1e-5
NEG = -1e30

GLA_COLS = 3200
ATT_COLS = 6144
VMEM_LIMIT = 56 * 1024 * 1024


def _cparams(sem):
    return pltpu.CompilerParams(dimension_semantics=sem, vmem_limit_bytes=VMEM_LIMIT)


def _dot(a, b):
    return jnp.dot(a, b, preferred_element_type=F32)


def _dot_nt(a, b):
    return lax.dot_general(a, b, (((1,), (1,)), ((), ())), preferred_element_type=F32)


def _dot_tn(a, b):
    return lax.dot_general(a, b, (((0,), (0,)), ((), ())), preferred_element_type=F32)


def _sigmoid(x):
    return 1.0 / (1.0 + jnp.exp(-x))


def _layer_norm(u, g, b):
    mu = jnp.mean(u, axis=-1, keepdims=True)
    d = u - mu
    var = jnp.mean(d * d, axis=-1, keepdims=True)
    return d * lax.rsqrt(var + LN_EPS) * g + b


def _ada_kernel(c_ref, w_ref, b_ref, o_ref):
    c = c_ref[...]
    s = (c * _sigmoid(c)).astype(BF16)
    o_ref[...] = _dot(s, w_ref[...]) + b_ref[...]


def _ada(c, w, b):
    n = c.shape[0]
    cols = w.shape[1]
    return pl.pallas_call(
        _ada_kernel,
        grid=(cols // D_MODEL,),
        in_specs=[pl.BlockSpec((n, D_MODEL), lambda j: (0, 0)),
                  pl.BlockSpec((D_MODEL, D_MODEL), lambda j: (0, j)),
                  pl.BlockSpec((1, D_MODEL), lambda j: (0, j))],
        out_specs=pl.BlockSpec((n, D_MODEL), lambda j: (0, j)),
        out_shape=jax.ShapeDtypeStruct((n, cols), F32),
        compiler_params=_cparams(("arbitrary",)),
        name="ada_mod",
    )(c, w, b)


def _proj_kernel(x_ref, mod_ref, w_ref, *rest, rope_tiles):
    if rope_tiles:
        cos_ref, sin_ref, o_ref, h_ref = rest
    else:
        o_ref, h_ref = rest
    j = pl.program_id(2)

    @pl.when(j == 0)
    def _():
        x = x_ref[0]
        h_ref[...] = (x * (1.0 + mod_ref[0, 1:2, :]) + mod_ref[0, 0:1, :]).astype(BF16)

    acc = _dot(h_ref[...], w_ref[...])
    if not rope_tiles:
        o_ref[0] = acc.astype(o_ref.dtype)
        return

    is_rope = functools.reduce(jnp.logical_or, [j == t for t in rope_tiles])

    @pl.when(is_rope)
    def _():
        cos = cos_ref[...]
        sin = sin_ref[...]
        lane = lax.broadcasted_iota(jnp.int32, cos.shape, 1)
        first = (lane % DIFF_D) < (DIFF_D // 2)
        for c in range(acc.shape[1] // 128):
            xc = acc[:, c * 128:(c + 1) * 128]
            rot = jnp.where(first, pltpu.roll(xc, 128 - DIFF_D // 2, 1), pltpu.roll(xc, DIFF_D // 2, 1))
            o_ref[0, :, c * 128:(c + 1) * 128] = (xc * cos + rot * sin).astype(o_ref.dtype)

    @pl.when(jnp.logical_not(is_rope))
    def _():
        o_ref[0] = acc.astype(o_ref.dtype)


def _proj(x, mod, w, out_dtype, tn, rope=None, name="proj"):
    bsz, t, _ = x.shape
    cols = w.shape[1]
    tm = min(1024, t)
    in_specs = [pl.BlockSpec((1, tm, D_MODEL), lambda b, i, j: (b, i, 0)),
                pl.BlockSpec((1, 2, D_MODEL), lambda b, i, j: (b, 0, 0)),
                pl.BlockSpec((D_MODEL, tn), lambda b, i, j: (0, j))]
    args = [x, mod, w]
    rope_tiles = ()
    if rope is not None:
        cos, sin, rope_tiles = rope
        in_specs += [pl.BlockSpec((tm, 128), lambda b, i, j: (i, 0)),
                     pl.BlockSpec((tm, 128), lambda b, i, j: (i, 0))]
        args += [cos, sin]
    return pl.pallas_call(
        functools.partial(_proj_kernel, rope_tiles=tuple(rope_tiles)),
        grid=(bsz, t // tm, cols // tn),
        in_specs=in_specs,
        out_specs=pl.BlockSpec((1, tm, tn), lambda b, i, j: (b, i, j)),
        out_shape=jax.ShapeDtypeStruct((bsz, t, cols), out_dtype),
        scratch_shapes=[pltpu.VMEM((tm, D_MODEL), BF16)],
        compiler_params=_cparams(("parallel", "parallel", "arbitrary")),
        name=name,
    )(*args)


def _gla_masks(reverse):
    c, s = GLA_CHUNK, GLA_SUB
    i = np.arange(c)[:, None]
    j = np.arange(c)[None, :]
    if not reverse:
        cum = (j <= i)
        ref = (j < (i // s) * s)
    else:
        cum = (j >= i)
        ref = (j >= (i // s + 1) * s)
    return np.concatenate([cum, ref], axis=0).astype(np.float32)


def _gla_kernel(q_ref, k_ref, v_ref, glr_ref, cm_ref, wgk_ref, bgk_ref, *rest, reverse, n_chunks):
    if reverse:
        ofwd_ref, r_ref, ng_ref, o_ref, st_ref, acc_ref = rest
    else:
        o_ref, st_ref = rest
        acc_ref = o_ref.at[0]
    c, s = GLA_CHUNK, GLA_SUB
    nsub = c // s

    @pl.when(pl.program_id(2) == 0)
    def _():
        st_ref[...] = jnp.zeros_like(st_ref)

    cm = cm_ref[...]
    z = _dot(glr_ref[0].astype(BF16), wgk_ref[...]) + bgk_ref[...]
    g_all = (jnp.minimum(z, 0.0) - jnp.log(1.0 + jnp.exp(-jnp.abs(z)))) * (1.0 / GLA_TAU)

    qi_l, d_l, ut_l = {}, {}, {}
    for ci in range(n_chunks):
        r0 = ci * c
        g = g_all[r0:r0 + c]
        cums = jnp.dot(cm, g, preferred_element_type=F32, precision=lax.Precision.HIGHEST)
        bc, bref = cums[:c], cums[c:]
        q = q_ref[0, r0:r0 + c, :] * (GLA_DK ** -0.5)
        k = k_ref[0, r0:r0 + c, :]
        v = v_ref[0, r0:r0 + c, :].astype(BF16)
        qe = q * jnp.exp(bc - bref)
        qi_l[ci] = (qe * jnp.exp(bref)).astype(BF16)
        qe = qe.astype(BF16)
        bl = bc[0:1] if reverse else bc[c - 1:c]
        d_l[ci] = jnp.exp(bl)
        kl = (k * jnp.exp(bl - bc)).astype(BF16)
        ut_l[ci] = _dot_tn(v, kl)
        for si in range(nsub):
            lo, hi = (si * s, c) if reverse else (0, (si + 1) * s)
            bi = bref[si * s:si * s + 1]
            ks = (k[lo:hi] * jnp.exp(bi - bc[lo:hi])).astype(BF16)
            att = _dot_nt(qe[si * s:(si + 1) * s], ks)
            rg = lax.broadcasted_iota(jnp.int32, (s, hi - lo), 0) + si * s
            cg = lax.broadcasted_iota(jnp.int32, (s, hi - lo), 1) + lo
            keep = (cg > rg) if reverse else (cg <= rg)
            att = jnp.where(keep, att, 0.0).astype(BF16)
            acc_ref[r0 + si * s:r0 + (si + 1) * s, :] = _dot(att, v[lo:hi])

    st = st_ref[...]
    order = range(n_chunks - 1, -1, -1) if reverse else range(n_chunks)
    for ci in order:
        r0 = ci * c
        inter = _dot_nt(qi_l[ci], st.astype(BF16))
        acc_ref[r0:r0 + c, :] = acc_ref[r0:r0 + c, :] + inter
        st = st * d_l[ci] + ut_l[ci]
    st_ref[...] = st

    if reverse:
        y = ofwd_ref[0] + acc_ref[...]
        y = y * lax.rsqrt(jnp.mean(y * y, axis=-1, keepdims=True) + LN_EPS) * ng_ref[...]
        r = r_ref[0]
        o_ref[0] = (y * (r * _sigmoid(r))).astype(o_ref.dtype)


def _gla_dir(pa, cm, wgk, bgk, reverse, ofwd=None, norm_g=None):
    bsz, t, _ = pa.shape
    tt = min(512, t)
    nt = t // tt

    def ti(i):
        return (nt - 1 - i) if reverse else i

    in_specs = [pl.BlockSpec((1, tt, GLA_DK), lambda b, h, i: (b, ti(i), h)),
                pl.BlockSpec((1, tt, GLA_DK), lambda b, h, i: (b, ti(i), GLA_HEADS + h)),
                pl.BlockSpec((1, tt, GLA_DV), lambda b, h, i: (b, ti(i), GLA_HEADS + h)),
                pl.BlockSpec((1, tt, 128), lambda b, h, i: (b, ti(i), 3072 // 128)),
                pl.BlockSpec((2 * GLA_CHUNK, GLA_CHUNK), lambda b, h, i: (0, 0)),
                pl.BlockSpec((128, GLA_DK), lambda b, h, i: (0, h)),
                pl.BlockSpec((1, GLA_DK), lambda b, h, i: (0, h))]
    args = [pa, pa, pa, pa, cm, wgk, bgk]
    scratch = [pltpu.VMEM((GLA_DV, GLA_DK), F32)]
    if reverse:
        in_specs += [pl.BlockSpec((1, tt, GLA_DV), lambda b, h, i: (b, ti(i), h)),
                     pl.BlockSpec((1, tt, GLA_DV), lambda b, h, i: (b, ti(i), 2 * GLA_HEADS + h)),
                     pl.BlockSpec((1, GLA_DV), lambda b, h, i: (0, 0))]
        args += [ofwd, pa, norm_g]
        scratch += [pltpu.VMEM((tt, GLA_DV), F32)]
        out_dtype = BF16
    else:
        out_dtype = F32
    return pl.pallas_call(
        functools.partial(_gla_kernel, reverse=reverse, n_chunks=tt // GLA_CHUNK),
        grid=(bsz, GLA_HEADS, nt),
        in_specs=in_specs,
        out_specs=pl.BlockSpec((1, tt, GLA_DV), lambda b, h, i: (b, ti(i), h)),
        out_shape=jax.ShapeDtypeStruct((bsz, t, GLA_HEADS * GLA_DV), out_dtype),
        scratch_shapes=scratch,
        compiler_params=_cparams(("parallel", "parallel", "arbitrary")),
        name="gla_bwd" if reverse else "gla_fwd",
    )(*args)


def _na_bias_table(rpb):
    qc = np.arange(GRID_W)[:, None]
    kc = np.arange(GRID_W)[None, :]
    cs = np.clip(qc - NA_KC // 2, 0, GRID_W - NA_KC)
    allowed = (kc >= cs) & (kc < cs + NA_KC)
    dc = np.clip(kc - qc + NA_KC - 1, 0, 2 * NA_KC - 2)
    dlt = np.arange(NA_KR)[:, None]
    wi = np.arange(NA_KR)[None, :]
    dr = np.clip(wi - dlt + NA_KR - 1, 0, 2 * NA_KR - 2)
    tab = rpb[:, dr[:, :, None, None], dc[None, None, :, :]]
    tab = jnp.where(jnp.asarray(allowed)[None, None, None], tab.astype(F32), NEG)
    tab = tab.transpose(0, 1, 3, 2, 4)
    return tab.reshape(NA_HEADS, NA_KR, GRID_W, NA_KR * GRID_W)


def _na_kernel(q_ref, k_ref, v_ref, bias_ref, o_ref, *, rows):
    lane = lax.broadcasted_iota(jnp.int32, (GRID_W, 128), 1)
    low = lane < NA_DH
    win = NA_KR * GRID_W

    def body(r, carry):
        rs = jnp.clip(r - NA_KR // 2, 0, rows - NA_KR)
        dlt = r - rs
        q0 = pl.multiple_of(r * GRID_W, GRID_W)
        k0 = pl.multiple_of(rs * GRID_W, GRID_W)
        q = q_ref[0, pl.ds(q0, GRID_W), :]
        kw = k_ref[0, pl.ds(k0, win), :]
        vw = v_ref[0, pl.ds(k0, win), :]
        outs = []
        for a in range(2):
            qm = jnp.where(low if a == 0 else jnp.logical_not(low), q, jnp.zeros_like(q))
            sc = _dot_nt(qm, kw) + bias_ref[a, dlt]
            m = jnp.max(sc, axis=-1, keepdims=True)
            e = jnp.exp(sc - m)
            zs = jnp.sum(e, axis=-1, keepdims=True)
            outs.append(_dot(e.astype(BF16), vw) / zs)
        o_ref[0, pl.ds(q0, GRID_W), :] = jnp.where(low, outs[0], outs[1]).astype(o_ref.dtype)
        return carry

    lax.fori_loop(0, rows, body, 0)


def _na(pb, bias):
    bsz, t, _ = pb.shape
    rows = t // GRID_W
    assert rows >= NA_KR
    npair = NA_HEADS // 2
    return pl.pallas_call(
        functools.partial(_na_kernel, rows=rows),
        grid=(bsz, npair),
        in_specs=[pl.BlockSpec((1, t, 128), lambda b, j: (b, 0, j)),
                  pl.BlockSpec((1, t, 128), lambda b, j: (b, 0, npair + j)),
                  pl.BlockSpec((1, t, 128), lambda b, j: (b, 0, 2 * npair + j)),
                  pl.BlockSpec((2, NA_KR, GRID_W, NA_KR * GRID_W), lambda b, j: (j, 0, 0, 0))],
        out_specs=pl.BlockSpec((1, t, 128), lambda b, j: (b, 0, j)),
        out_shape=jax.ShapeDtypeStruct((bsz, t, D_MODEL), BF16),
        compiler_params=_cparams(("parallel", "parallel")),
        name="na_attn",
    )(pb, pb, pb, bias)


def _diff_kernel(q_ref, k_ref, v_ref, lam_ref, ng_ref, o_ref, *, lam_init):
    lv = lam_ref[...]
    l1 = jnp.sum(lv[0:1] * lv[1:2], axis=-1, keepdims=True)
    l2 = jnp.sum(lv[2:3] * lv[3:4], axis=-1, keepdims=True)
    lam = jnp.exp(l1) - jnp.exp(l2) + lam_init
    q = q_ref[0]
    k = k_ref[0]
    v = v_ref[0]
    low = lax.broadcasted_iota(jnp.int32, q.shape, 1) < DIFF_D
    outs = []
    for p in range(2):
        qm = jnp.where(low if p == 0 else jnp.logical_not(low), q, jnp.zeros_like(q))
        sc = _dot_nt(qm, k)
        m = jnp.max(sc, axis=-1, keepdims=True)
        e = jnp.exp(sc - m)
        zs = jnp.sum(e, axis=-1, keepdims=True)
        outs.append(_dot(e.astype(BF16), v) / zs)
    o = outs[0] - lam * outs[1]
    y = o * lax.rsqrt(jnp.mean(o * o, axis=-1, keepdims=True) + LN_EPS) * ng_ref[...]
    o_ref[0] = (y * (1.0 - lam_init)).astype(o_ref.dtype)


def _diff(pb, lam_vec, norm_g, lam_init):
    bsz, t, _ = pb.shape
    tq = min(256, t)
    base = (NA_HEADS * NA_DH * 3) // 128
    return pl.pallas_call(
        functools.partial(_diff_kernel, lam_init=lam_init),
        grid=(bsz, DIFF_HEADS, t // tq),
        in_specs=[pl.BlockSpec((1, tq, 128), lambda b, h, i: (b, i, base + h)),
                  pl.BlockSpec((1, t, 128), lambda b, h, i: (b, 0, base + DIFF_HEADS + h)),
                  pl.BlockSpec((1, t, 128), lambda b, h, i: (b, 0, base + 2 * DIFF_HEADS + h)),
                  pl.BlockSpec((4, DIFF_D), lambda b, h, i: (0, 0)),
                  pl.BlockSpec((1, 2 * DIFF_D), lambda b, h, i: (0, 0))],
        out_specs=pl.BlockSpec((1, tq, 128), lambda b, h, i: (b, i, h)),
        out_shape=jax.ShapeDtypeStruct((bsz, t, D_MODEL), BF16),
        compiler_params=_cparams(("parallel", "parallel", "arbitrary")),
        name="diff_attn",
    )(pb, pb, pb, lam_vec, norm_g)


def _merge_kernel(x_ref, mod_ref, ya_ref, yb_ref, yc_ref, wgt_ref, wbr_ref, wo_ref, g_ref, b_ref, o_ref):
    x = x_ref[0]
    h = (x * (1.0 + mod_ref[0, 1:2, :]) + mod_ref[0, 0:1, :]).astype(BF16)
    merged = None
    for n, y_ref in enumerate((ya_ref, yb_ref, yc_ref)):
        gt = _dot(h, wgt_ref[:, n * D_MODEL:(n + 1) * D_MODEL])
        br = _dot(y_ref[0], wbr_ref[n])
        term = _sigmoid(gt) * br
        merged = term if merged is None else merged + term
    out = _dot(merged.astype(BF16), wo_ref[...])
    u = DN_ALPHA * x + mod_ref[0, 2:3, :] * out
    o_ref[0] = _layer_norm(u, g_ref[...], b_ref[...])


def _const_spec(shape, nidx):
    zeros = (0,) * len(shape)
    if nidx == 2:
        return pl.BlockSpec(shape, lambda b, i: zeros, pipeline_mode=pl.Buffered(1))
    return pl.BlockSpec(shape, lambda b, i, e: zeros, pipeline_mode=pl.Buffered(1))


def _merge(x, mod, ya, yb, yc, wgt, wbr, wo, g, b):
    bsz, t, _ = x.shape
    tm = min(512, t)
    tok = pl.BlockSpec((1, tm, D_MODEL), lambda bb, i: (bb, i, 0))
    return pl.pallas_call(
        _merge_kernel,
        grid=(bsz, t // tm),
        in_specs=[tok, pl.BlockSpec((1, 3, D_MODEL), lambda bb, i: (bb, 0, 0)), tok, tok, tok,
                  _const_spec((D_MODEL, 3 * D_MODEL), 2), _const_spec((3, D_MODEL, D_MODEL), 2),
                  _const_spec((D_MODEL, D_MODEL), 2), _const_spec((1, D_MODEL), 2), _const_spec((1, D_MODEL), 2)],
        out_specs=tok,
        out_shape=jax.ShapeDtypeStruct((bsz, t, D_MODEL), F32),
        compiler_params=_cparams(("parallel", "parallel")),
        name="merge_ln",
    )(x, mod, ya, yb, yc, wgt, wbr, wo, g, b)


def _top_rows(x, n):
    rows = []
    for _ in range(n):
        m = jnp.max(x, axis=0, keepdims=True)
        rows.append(m)
        x = jnp.where(x >= m, NEG, x)
    return rows


def _peer_route_kernel(x_ref, mod_ref, wq_ref, sk_ref, h_ref, p_ref):
    x = x_ref[0]
    h = (x * (1.0 + mod_ref[0, 1:2, :]) + mod_ref[0, 0:1, :]).astype(BF16)
    h_ref[0] = h
    q = _dot(h, wq_ref[...]).astype(BF16)
    kk = PEER_TOPK
    for hd in range(PEER_HEADS):
        qh = q[:, hd * PEER_DKEY:(hd + 1) * PEER_DKEY]
        s1 = _dot_nt(sk_ref[2 * hd], qh)
        s2 = _dot_nt(sk_ref[2 * hd + 1], qh)
        a = _top_rows(s1, kk + 1)
        b = _top_rows(s2, kk + 1)
        b_lo = jnp.concatenate(b[:8], axis=0)
        cand = [a[i] + b_lo for i in range(8)]
        cand.append(a[0] + jnp.concatenate(b[8:16], axis=0))
        cand.append(jnp.concatenate(a[8:16], axis=0) + b[0])
        cand.append(jnp.concatenate([a[0] + b[16], a[16] + b[0]] + [jnp.full_like(a[0], NEG)] * 6, axis=0))
        cand = jnp.concatenate(cand, axis=0)
        top = _top_rows(cand, kk + 1)
        thr = 0.5 * (top[kk - 1] + top[kk])
        mx = a[0] + b[0]
        zs = jnp.sum(jnp.where(cand >= thr, jnp.exp(cand - mx), 0.0), axis=0, keepdims=True)
        p_ref[0, hd, 0] = thr - s1
        p_ref[0, hd, 1] = jnp.exp(s1 - a[0]) / zs
        p_ref[0, hd, 2] = s2
        p_ref[0, hd, 3] = jnp.exp(s2 - b[0])


def _peer_route(x1, mod, wq, sk):
    bsz, t, _ = x1.shape
    tp = min(256, t)
    return pl.pallas_call(
        _peer_route_kernel,
        grid=(bsz, t // tp),
        in_specs=[pl.BlockSpec((1, tp, D_MODEL), lambda b, i: (b, i, 0)),
                  pl.BlockSpec((1, 2, D_MODEL), lambda b, i: (b, 0, 0)),
                  _const_spec((D_MODEL, PEER_HEADS * PEER_DKEY), 2),
                  _const_spec((2 * PEER_HEADS, PEER_NKEYS, PEER_DKEY), 2)],
        out_specs=[pl.BlockSpec((1, tp, D_MODEL), lambda b, i: (b, i, 0)),
                   pl.BlockSpec((1, PEER_HEADS, 4, PEER_NKEYS, tp), lambda b, i: (b, 0, 0, 0, i))],
        out_shape=[jax.ShapeDtypeStruct((bsz, t, D_MODEL), BF16),
                   jax.ShapeDtypeStruct((bsz, PEER_HEADS, 4, PEER_NKEYS, t), F32)],
        compiler_params=_cparams(("parallel", "parallel")),
        name="peer_route",
    )(x1, mod, wq, sk)


def _peer_dense_kernel(h_ref, u_ref, vt_ref, p_ref, x_ref, mod_ref, g_ref, b_ref, o_ref, acc_ref, w_ref, *, eb):
    e = pl.program_id(2)

    @pl.when(e == 0)
    def _():
        acc_ref[...] = jnp.zeros_like(acc_ref)

    act = _dot_nt(u_ref[...], h_ref[0])
    nk = PEER_NKEYS
    for jj in range(eb // nk):
        j = e * (eb // nk) + jj
        w = None
        for hd in range(PEER_HEADS):
            t1 = p_ref[0, hd, 0, pl.ds(j, 1), :]
            e1 = p_ref[0, hd, 1, pl.ds(j, 1), :]
            term = jnp.where(p_ref[0, hd, 2] >= t1, p_ref[0, hd, 3], 0.0) * e1
            w = term if w is None else w + term
        a = act[jj * nk:(jj + 1) * nk]
        gelu = 0.5 * a * (1.0 + lax.erf(a * (2.0 ** -0.5)))
        w_ref[jj * nk:(jj + 1) * nk, :] = (w * gelu).astype(BF16)
    acc_ref[...] += _dot(vt_ref[...], w_ref[...])

    @pl.when(e == pl.num_programs(2) - 1)
    def _():
        f = acc_ref[...].T
        x = x_ref[0]
        u = DN_ALPHA * x + mod_ref[0, 0:1, :] * f
        o_ref[0] = _layer_norm(u, g_ref[...], b_ref[...])


def _peer_dense(h2, u_tab, vt_tab, p, x1, mod, g, b):
    bsz, t, _ = x1.shape
    tm = min(512, t)
    eb = 512
    tok = pl.BlockSpec((1, tm, D_MODEL), lambda bb, i, e: (bb, i, 0))
    return pl.pallas_call(
        functools.partial(_peer_dense_kernel, eb=eb),
        grid=(bsz, t // tm, PEER_N // eb),
        in_specs=[tok,
                  pl.BlockSpec((eb, D_MODEL), lambda bb, i, e: (e, 0)),
                  pl.BlockSpec((D_MODEL, eb), lambda bb, i, e: (0, e)),
                  pl.BlockSpec((1, PEER_HEADS, 4, PEER_NKEYS, tm), lambda bb, i, e: (bb, 0, 0, 0, i)),
                  tok,
                  pl.BlockSpec((1, 1, D_MODEL), lambda bb, i, e: (bb, 0, 0)),
                  _const_spec((1, D_MODEL), 3), _const_spec((1, D_MODEL), 3)],
        out_specs=tok,
        out_shape=jax.ShapeDtypeStruct((bsz, t, D_MODEL), F32),
        scratch_shapes=[pltpu.VMEM((D_MODEL, tm), F32), pltpu.VMEM((eb, tm), BF16)],
        compiler_params=_cparams(("parallel", "parallel", "arbitrary")),
        name="peer_dense",
    )(h2, u_tab, vt_tab, p, x1, mod, g, b)


def _rope_tables(t):
    half = DIFF_D // 2
    inv = jnp.exp(-math.log(ROPE_THETA) * jnp.arange(half, dtype=F32) / half)
    ang = jnp.arange(t, dtype=F32)[:, None] * inv[None, :]
    cos, sin = jnp.cos(ang), jnp.sin(ang)
    cos128 = jnp.tile(cos, (1, 4))
    sin128 = jnp.tile(jnp.concatenate([-sin, sin], axis=1), (1, 2))
    return cos128, sin128


def _layer_params(l, w_in, gla_w_gk, gla_b_gk, gla_norm_g, na_rpb, diff_lambda, diff_norm_g, w_br, w_o,
                  ln1_g, ln1_b, peer_wq, peer_subkeys, peer_u, peer_v, ln2_g, ln2_b):
    w = w_in[l]
    w_gla = jnp.pad(w[:, :3104], ((0, 0), (0, GLA_COLS - 3104))).astype(BF16)
    w_att = w[:, 3104:3104 + ATT_COLS]
    qscale = np.ones((ATT_COLS,), np.float32)
    qscale[0:1024] = NA_DH ** -0.5
    qscale[3072:4096] = DIFF_D ** -0.5
    w_att = (w_att * qscale).astype(BF16)
    w_gt = w[:, 3104 + ATT_COLS:].astype(BF16)
    wgk = jnp.zeros((2, 128, GLA_HEADS * GLA_DK), F32)
    wgk = wgk.at[0, :GLA_RANK].set(gla_w_gk[l, 0]).at[1, GLA_RANK:2 * GLA_RANK].set(gla_w_gk[l, 1]).astype(BF16)
    sk = peer_subkeys[l]
    half = PEER_DKEY // 2
    skp = jnp.zeros((PEER_HEADS, 2, PEER_NKEYS, PEER_DKEY), F32)
    skp = skp.at[:, 0, :, :half].set(sk[:, 0]).at[:, 1, :, half:].set(sk[:, 1])
    return dict(
        w_gla=w_gla, w_att=w_att, w_gt=w_gt, wgk=wgk, bgk=gla_b_gk[l],
        gla_g=gla_norm_g[l][None], na_bias=_na_bias_table(na_rpb[l]),
        lam=diff_lambda[l], diff_g=diff_norm_g[l][None],
        w_br=w_br[l].astype(BF16), w_o=w_o[l].astype(BF16),
        ln1_g=ln1_g[l][None], ln1_b=ln1_b[l][None],
        wq=peer_wq[l].astype(BF16), sk=skp.reshape(2 * PEER_HEADS, PEER_NKEYS, PEER_DKEY).astype(BF16),
        u=peer_u[l].astype(BF16), vt=peer_v[l].T.astype(BF16),
        ln2_g=ln2_g[l][None], ln2_b=ln2_b[l][None],
        lam_init=0.8 - 0.6 * math.exp(-0.3 * l),
    )


def _layer(x, mod, p, rope, cms):
    pa = _proj(x, mod[:, 0:2], p["w_gla"], F32, tn=640, name="proj_gla")
    pb = _proj(x, mod[:, 0:2], p["w_att"], BF16, tn=1024, rope=rope + ((3, 4),), name="proj_att")
    ofwd = _gla_dir(pa, cms[0], p["wgk"][0], p["bgk"][0:1], reverse=False)
    ya = _gla_dir(pa, cms[1], p["wgk"][1], p["bgk"][1:2], reverse=True, ofwd=ofwd, norm_g=p["gla_g"])
    yb = _na(pb, p["na_bias"])
    yc = _diff(pb, p["lam"], p["diff_g"], p["lam_init"])
    x1 = _merge(x, mod[:, 0:3], ya, yb, yc, p["w_gt"], p["w_br"], p["w_o"], p["ln1_g"], p["ln1_b"])
    h2, pr = _peer_route(x1, mod[:, 3:5], p["wq"], p["sk"])
    return _peer_dense(h2, p["u"], p["vt"], pr, x1, mod[:, 5:6], p["ln2_g"], p["ln2_b"])


def kernel(x_prompt, x_sample, c_prompt, c_sample, w_ada, b_ada, w_in, gla_w_gk, gla_b_gk, gla_norm_g, na_rpb, diff_lambda, diff_norm_g, w_br, w_o, ln1_g, ln1_b, peer_wq, peer_subkeys, peer_u, peer_v, ln2_g, ln2_b):
    nb = x_prompt.shape[0]
    c_all = jnp.concatenate([c_prompt, c_sample], axis=0)
    cms = (jnp.asarray(_gla_masks(False)), jnp.asarray(_gla_masks(True)))
    ropes = {x.shape[1]: _rope_tables(x.shape[1]) for x in (x_prompt, x_sample)}
    xs = [x_prompt, x_sample]
    for l in range(DEPTH):
        p = _layer_params(l, w_in, gla_w_gk, gla_b_gk, gla_norm_g, na_rpb, diff_lambda, diff_norm_g, w_br, w_o,
                          ln1_g, ln1_b, peer_wq, peer_subkeys, peer_u, peer_v, ln2_g, ln2_b)
        mod_all = _ada(c_all, w_ada[l].astype(BF16), b_ada[l][None]).reshape(c_all.shape[0], 6, D_MODEL)
        mods = [mod_all[:nb], mod_all[nb:]]
        xs = [_layer(x, m, p, ropes[x.shape[1]], cms) for x, m in zip(xs, mods)]
    return (xs[0], xs[1])
```

```python
import functools
import math

import numpy as np
import jax
import jax.numpy as jnp
from jax import lax
from jax.experimental import pallas as pl
from jax.experimental.pallas import tpu as pltpu

F32 = jnp.float32
BF16 = jnp.bfloat16

D_MODEL = 1024
DEPTH = 2
GRID_W = 64
GLA_HEADS, GLA_DK, GLA_DV, GLA_RANK, GLA_TAU, GLA_CHUNK = 4, 128, 256, 16, 16.0, 64
GLA_SUB = 16
NA_HEADS, NA_DH, NA_KR, NA_KC = 16, 64, 8, 16
NA_ROW_UNROLL = 4
DIFF_HEADS, DIFF_D = 8, 64
ROPE_THETA = 10000.0
PEER_HEADS, PEER_NKEYS, PEER_DKEY, PEER_TOPK = 8, 128, 128, 16
PEER_N = PEER_NKEYS * PEER_NKEYS
DN_ALPHA = (2 * DEPTH) ** 0.25
LN_EPS = 1e-5
NEG = -1e30

GLA_COLS = 3200
ATT_COLS = 6144
VMEM_LIMIT = 56 * 1024 * 1024


def _cparams(sem):
    return pltpu.CompilerParams(dimension_semantics=sem, vmem_limit_bytes=VMEM_LIMIT)


def _dot(a, b):
    return jnp.dot(a, b, preferred_element_type=F32)


def _dot_nt(a, b):
    return lax.dot_general(a, b, (((1,), (1,)), ((), ())), preferred_element_type=F32)


def _dot_tn(a, b):
    return lax.dot_general(a, b, (((0,), (0,)), ((), ())), preferred_element_type=F32)


def _sigmoid(x):
    return 1.0 / (1.0 + jnp.exp(-x))


def _layer_norm(u, g, b):
    mu = jnp.mean(u, axis=-1, keepdims=True)
    d = u - mu
    var = jnp.mean(d * d, axis=-1, keepdims=True)
    return d * lax.rsqrt(var + LN_EPS) * g + b


def _ada_kernel(c_ref, w_ref, b_ref, o_ref):
    c = c_ref[...]
    s = (c * _sigmoid(c)).astype(BF16)
    o_ref[...] = _dot(s, w_ref[...]) + b_ref[...]


def _ada(c, w, b):
    n = c.shape[0]
    cols = w.shape[1]
    return pl.pallas_call(
        _ada_kernel,
        grid=(cols // D_MODEL,),
        in_specs=[pl.BlockSpec((n, D_MODEL), lambda j: (0, 0)),
                  pl.BlockSpec((D_MODEL, D_MODEL), lambda j: (0, j)),
                  pl.BlockSpec((1, D_MODEL), lambda j: (0, j))],
        out_specs=pl.BlockSpec((n, D_MODEL), lambda j: (0, j)),
        out_shape=jax.ShapeDtypeStruct((n, cols), F32),
        compiler_params=_cparams(("arbitrary",)),
        name="ada_mod",
    )(c, w, b)


def _proj_kernel(x_ref, mod_ref, w_ref, *rest, rope_tiles):
    if rope_tiles:
        cos_ref, sin_ref, o_ref, h_ref = rest
    else:
        o_ref, h_ref = rest
    j = pl.program_id(2)

    @pl.when(j == 0)
    def _():
        x = x_ref[0]
        h_ref[...] = (x * (1.0 + mod_ref[0, 1:2, :]) + mod_ref[0, 0:1, :]).astype(BF16)

    acc = _dot(h_ref[...], w_ref[...])
    if not rope_tiles:
        o_ref[0] = acc.astype(o_ref.dtype)
        return

    is_rope = functools.reduce(jnp.logical_or, [j == t for t in rope_tiles])

    @pl.when(is_rope)
    def _():
        cos = cos_ref[...]
        sin = sin_ref[...]
        lane = lax.broadcasted_iota(jnp.int32, cos.shape, 1)
        first = (lane % DIFF_D) < (DIFF_D // 2)
        for c in range(acc.shape[1] // 128):
            xc = acc[:, c * 128:(c + 1) * 128]
            rot = jnp.where(first, pltpu.roll(xc, 128 - DIFF_D // 2, 1), pltpu.roll(xc, DIFF_D // 2, 1))
            o_ref[0, :, c * 128:(c + 1) * 128] = (xc * cos + rot * sin).astype(o_ref.dtype)

    @pl.when(jnp.logical_not(is_rope))
    def _():
        o_ref[0] = acc.astype(o_ref.dtype)


def _proj(x, mod, w, out_dtype, tn, rope=None, name="proj"):
    bsz, t, _ = x.shape
    cols = w.shape[1]
    tm = min(1024, t)
    in_specs = [pl.BlockSpec((1, tm, D_MODEL), lambda b, i, j: (b, i, 0)),
                pl.BlockSpec((1, 2, D_MODEL), lambda b, i, j: (b, 0, 0)),
                pl.BlockSpec((D_MODEL, tn), lambda b, i, j: (0, j))]
    args = [x, mod, w]
    rope_tiles = ()
    if rope is not None:
        cos, sin, rope_tiles = rope
        in_specs += [pl.BlockSpec((tm, 128), lambda b, i, j: (i, 0)),
                     pl.BlockSpec((tm, 128), lambda b, i, j: (i, 0))]
        args += [cos, sin]
    return pl.pallas_call(
        functools.partial(_proj_kernel, rope_tiles=tuple(rope_tiles)),
        grid=(bsz, t // tm, cols // tn),
        in_specs=in_specs,
        out_specs=pl.BlockSpec((1, tm, tn), lambda b, i, j: (b, i, j)),
        out_shape=jax.ShapeDtypeStruct((bsz, t, cols), out_dtype),
        scratch_shapes=[pltpu.VMEM((tm, D_MODEL), BF16)],
        compiler_params=_cparams(("parallel", "parallel", "arbitrary")),
        name=name,
    )(*args)


def _gla_masks(reverse):
    c, s = GLA_CHUNK, GLA_SUB
    i = np.arange(c)[:, None]
    j = np.arange(c)[None, :]
    if not reverse:
        cum = (j <= i)
        ref = (j < (i // s) * s)
    else:
        cum = (j >= i)
        ref = (j >= (i // s + 1) * s)
    return np.concatenate([cum, ref], axis=0).astype(np.float32)


def _gla_kernel(q_ref, k_ref, v_ref, glr_ref, cm_ref, wgk_ref, bgk_ref, *rest, reverse, n_chunks):
    if reverse:
        ofwd_ref, r_ref, ng_ref, o_ref, st_ref, acc_ref = rest
    else:
        o_ref, st_ref = rest
        acc_ref = o_ref.at[0]
    c, s = GLA_CHUNK, GLA_SUB
    nsub = c // s
    dk = GLA_DK

    @pl.when(pl.program_id(2) == 0)
    def _():
        st_ref[...] = jnp.zeros_like(st_ref)

    def lanes(x):
        return jnp.concatenate([x[ci * c:(ci + 1) * c] for ci in range(n_chunks)], axis=1)

    z = _dot(glr_ref[0].astype(BF16), wgk_ref[...]) + bgk_ref[...]
    g = lanes((jnp.minimum(z, 0.0) - jnp.log(1.0 + jnp.exp(-jnp.abs(z)))) * (1.0 / GLA_TAU))
    g_hi = g.astype(BF16)
    g_lo = (g - g_hi.astype(F32)).astype(BF16)
    cm = cm_ref[...].astype(BF16)
    cums = _dot(cm, g_hi) + _dot(cm, g_lo)
    bc, bref = cums[:c], cums[c:]
    q = lanes(q_ref[0]) * (GLA_DK ** -0.5)
    k = lanes(k_ref[0])
    qe = q * jnp.exp(bc - bref)
    qi = (qe * jnp.exp(bref)).astype(BF16)
    qe = qe.astype(BF16)
    bl = bc[0:1] if reverse else bc[c - 1:c]
    dec = jnp.exp(bl)
    kl = (k * jnp.exp(bl - bc)).astype(BF16)
    row = lax.broadcasted_iota(jnp.int32, (c, 1), 0)
    ksub = []
    for si in range(nsub):
        ok = (row >= si * s) if reverse else (row < (si + 1) * s)
        ksub.append(jnp.where(ok, k * jnp.exp(bref[si * s:si * s + 1] - bc), 0.0).astype(BF16))
    rr = lax.broadcasted_iota(jnp.int32, (c, nsub * c), 0)
    cc = lax.broadcasted_iota(jnp.int32, (c, nsub * c), 1)
    causal = (cc % c > rr) if reverse else (cc % c <= rr)
    keep = jnp.logical_and(cc // c == rr // s, causal)
    vs = [v_ref[0, ci * c:(ci + 1) * c, :].astype(BF16) for ci in range(n_chunks)]
    atts = []
    for ci in range(n_chunks):
        ls = slice(ci * dk, (ci + 1) * dk)
        kcat = jnp.concatenate([ks[:, ls] for ks in ksub], axis=0)
        atts.append(_dot_nt(qe[:, ls], kcat))
    atts = [jnp.where(keep, a, 0.0).astype(BF16) for a in atts]
    for ci in range(n_chunks):
        acc_ref[ci * c:(ci + 1) * c, :] = _dot(atts[ci], jnp.concatenate([vs[ci]] * nsub, axis=0))
    uts = [_dot_tn(vs[ci], kl[:, ci * dk:(ci + 1) * dk]) for ci in range(n_chunks)]

    st = st_ref[...]
    order = range(n_chunks - 1, -1, -1) if reverse else range(n_chunks)
    for ci in order:
        ls = slice(ci * dk, (ci + 1) * dk)
        inter = _dot_nt(qi[:, ls], st.astype(BF16))
        acc_ref[ci * c:(ci + 1) * c, :] = acc_ref[ci * c:(ci + 1) * c, :] + inter
        st = st * dec[:, ls] + uts[ci]
    st_ref[...] = st

    if reverse:
        y = ofwd_ref[0] + acc_ref[...]
        y = y * lax.rsqrt(jnp.mean(y * y, axis=-1, keepdims=True) + LN_EPS) * ng_ref[...]
        r = r_ref[0]
        o_ref[0] = (y * (r * _sigmoid(r))).astype(o_ref.dtype)


def _gla_dir(pa, cm, wgk, bgk, reverse, ofwd=None, norm_g=None):
    bsz, t, _ = pa.shape
    tt = min(512, t)
    nt = t // tt

    def ti(i):
        return (nt - 1 - i) if reverse else i

    in_specs = [pl.BlockSpec((1, tt, GLA_DK), lambda b, h, i: (b, ti(i), h)),
                pl.BlockSpec((1, tt, GLA_DK), lambda b, h, i: (b, ti(i), GLA_HEADS + h)),
                pl.BlockSpec((1, tt, GLA_DV), lambda b, h, i: (b, ti(i), GLA_HEADS + h)),
                pl.BlockSpec((1, tt, 128), lambda b, h, i: (b, ti(i), 3072 // 128)),
                pl.BlockSpec((2 * GLA_CHUNK, GLA_CHUNK), lambda b, h, i: (0, 0)),
                pl.BlockSpec((128, GLA_DK), lambda b, h, i: (0, h)),
                pl.BlockSpec((1, GLA_DK), lambda b, h, i: (0, h))]
    args = [pa, pa, pa, pa, cm, wgk, bgk]
    scratch = [pltpu.VMEM((GLA_DV, GLA_DK), F32)]
    if reverse:
        in_specs += [pl.BlockSpec((1, tt, GLA_DV), lambda b, h, i: (b, ti(i), h)),
                     pl.BlockSpec((1, tt, GLA_DV), lambda b, h, i: (b, ti(i), 2 * GLA_HEADS + h)),
                     pl.BlockSpec((1, GLA_DV), lambda b, h, i: (0, 0))]
        args += [ofwd, pa, norm_g]
        scratch += [pltpu.VMEM((tt, GLA_DV), F32)]
        out_dtype = BF16
    else:
        out_dtype = F32
    return pl.pallas_call(
        functools.partial(_gla_kernel, reverse=reverse, n_chunks=tt // GLA_CHUNK),
        grid=(bsz, GLA_HEADS, nt),
        in_specs=in_specs,
        out_specs=pl.BlockSpec((1, tt, GLA_DV), lambda b, h, i: (b, ti(i), h)),
        out_shape=jax.ShapeDtypeStruct((bsz, t, GLA_HEADS * GLA_DV), out_dtype),
        scratch_shapes=scratch,
        compiler_params=_cparams(("parallel", "parallel", "arbitrary")),
        name="gla_bwd" if reverse else "gla_fwd",
    )(*args)


def _na_bias_table(rpb):
    qc = np.arange(GRID_W)[:, None]
    kc = np.arange(GRID_W)[None, :]
    cs = np.clip(qc - NA_KC // 2, 0, GRID_W - NA_KC)
    allowed = (kc >= cs) & (kc < cs + NA_KC)
    dc = np.clip(kc - qc + NA_KC - 1, 0, 2 * NA_KC - 2)
    onehot = (dc[None] == np.arange(2 * NA_KC - 1)[:, None, None]).astype(np.float32)
    tab = jnp.einsum("hrd,dqk->hrqk", rpb.astype(F32), jnp.asarray(onehot), precision=lax.Precision.HIGHEST)
    tab = jnp.where(jnp.asarray(allowed)[None, None], tab, NEG)
    tab = jnp.stack([tab[:, NA_KR - 1 - d:2 * NA_KR - 1 - d] for d in range(NA_KR)], axis=1)
    tab = tab.transpose(0, 1, 3, 2, 4).reshape(NA_HEADS // 2, 2, NA_KR, GRID_W, NA_KR * GRID_W)
    return tab.transpose(0, 2, 1, 3, 4).reshape(NA_HEADS // 2, NA_KR, 2 * GRID_W, NA_KR * GRID_W)


def _na_kernel(q_ref, k_ref, v_ref, bias_ref, o_ref, *, rows):
    lane = lax.broadcasted_iota(jnp.int32, (GRID_W, 128), 1)
    low = lane < NA_DH
    win = NA_KR * GRID_W
    nr = NA_ROW_UNROLL

    def body(it, carry):
        q0s, scs, vws = [], [], []
        for u in range(nr):
            r = it * nr + u
            rs = jnp.clip(r - NA_KR // 2, 0, rows - NA_KR)
            q0 = pl.multiple_of(r * GRID_W, GRID_W)
            k0 = pl.multiple_of(rs * GRID_W, GRID_W)
            q = q_ref[0, pl.ds(q0, GRID_W), :]
            zq = jnp.zeros_like(q)
            q2 = jnp.concatenate([jnp.where(low, q, zq), jnp.where(low, zq, q)], axis=0)
            scs.append(_dot_nt(q2, k_ref[0, pl.ds(k0, win), :]) + bias_ref[0, r - rs])
            vws.append(v_ref[0, pl.ds(k0, win), :])
            q0s.append(q0)
        ms = [jnp.max(sc, axis=-1, keepdims=True) for sc in scs]
        es = [jnp.exp(sc - m) for sc, m in zip(scs, ms)]
        zs = [jnp.sum(e, axis=-1, keepdims=True) for e in es]
        os_ = [_dot(e.astype(BF16), vw) / z for e, vw, z in zip(es, vws, zs)]
        for q0, o in zip(q0s, os_):
            o_ref[0, pl.ds(q0, GRID_W), :] = jnp.where(low, o[:GRID_W], o[GRID_W:]).astype(o_ref.dtype)
        return carry

    lax.fori_loop(0, rows // nr, body, 0)


def _na(pb, bias):
    bsz, t, _ = pb.shape
    rows = t // GRID_W
    assert rows >= NA_KR and rows % NA_ROW_UNROLL == 0
    npair = NA_HEADS // 2
    return pl.pallas_call(
        functools.partial(_na_kernel, rows=rows),
        grid=(bsz, npair),
        in_specs=[pl.BlockSpec((1, t, 128), lambda b, j: (b, 0, j)),
                  pl.BlockSpec((1, t, 128), lambda b, j: (b, 0, npair + j)),
                  pl.BlockSpec((1, t, 128), lambda b, j: (b, 0, 2 * npair + j)),
                  pl.BlockSpec((1, NA_KR, 2 * GRID_W, NA_KR * GRID_W), lambda b, j: (j, 0, 0, 0))],
        out_specs=pl.BlockSpec((1, t, 128), lambda b, j: (b, 0, j)),
        out_shape=jax.ShapeDtypeStruct((bsz, t, D_MODEL), BF16),
        compiler_params=_cparams(("parallel", "parallel")),
        name="na_attn",
    )(pb, pb, pb, bias)


def _diff_kernel(q_ref, k_ref, v_ref, lam_ref, ng_ref, o_ref, *, lam_init):
    lv = lam_ref[...]
    l1 = jnp.sum(lv[0:1] * lv[1:2], axis=-1, keepdims=True)
    l2 = jnp.sum(lv[2:3] * lv[3:4], axis=-1, keepdims=True)
    lam = jnp.exp(l1) - jnp.exp(l2) + lam_init
    q = q_ref[0]
    k = k_ref[0]
    v = v_ref[0]
    low = lax.broadcasted_iota(jnp.int32, q.shape, 1) < DIFF_D
    outs = []
    for p in range(2):
        qm = jnp.where(low if p == 0 else jnp.logical_not(low), q, jnp.zeros_like(q))
        sc = _dot_nt(qm, k)
        m = jnp.max(sc, axis=-1, keepdims=True)
        e = jnp.exp(sc - m)
        zs = jnp.sum(e, axis=-1, keepdims=True)
        outs.append(_dot(e.astype(BF16), v) / zs)
    o = outs[0] - lam * outs[1]
    y = o * lax.rsqrt(jnp.mean(o * o, axis=-1, keepdims=True) + LN_EPS) * ng_ref[...]
    o_ref[0] = (y * (1.0 - lam_init)).astype(o_ref.dtype)


def _diff(pb, lam_vec, norm_g, lam_init):
    bsz, t, _ = pb.shape
    tq = min(256, t)
    base = (NA_HEADS * NA_DH * 3) // 128
    return pl.pallas_call(
        functools.partial(_diff_kernel, lam_init=lam_init),
        grid=(bsz, DIFF_HEADS, t // tq),
        in_specs=[pl.BlockSpec((1, tq, 128), lambda b, h, i: (b, i, base + h)),
                  pl.BlockSpec((1, t, 128), lambda b, h, i: (b, 0, base + DIFF_HEADS + h)),
                  pl.BlockSpec((1, t, 128), lambda b, h, i: (b, 0, base + 2 * DIFF_HEADS + h)),
                  pl.BlockSpec((4, DIFF_D), lambda b, h, i: (0, 0)),
                  pl.BlockSpec((1, 2 * DIFF_D), lambda b, h, i: (0, 0))],
        out_specs=pl.BlockSpec((1, tq, 128), lambda b, h, i: (b, i, h)),
        out_shape=jax.ShapeDtypeStruct((bsz, t, D_MODEL), BF16),
        compiler_params=_cparams(("parallel", "parallel", "arbitrary")),
        name="diff_attn",
    )(pb, pb, pb, lam_vec, norm_g)


def _merge_kernel(x_ref, mod_ref, ya_ref, yb_ref, yc_ref, wgt_ref, wbr_ref, wo_ref, g_ref, b_ref, o_ref):
    x = x_ref[0]
    h = (x * (1.0 + mod_ref[0, 1:2, :]) + mod_ref[0, 0:1, :]).astype(BF16)
    merged = None
    for n, y_ref in enumerate((ya_ref, yb_ref, yc_ref)):
        gt = _dot(h, wgt_ref[:, n * D_MODEL:(n + 1) * D_MODEL])
        br = _dot(y_ref[0], wbr_ref[n])
        term = _sigmoid(gt) * br
        merged = term if merged is None else merged + term
    out = _dot(merged.astype(BF16), wo_ref[...])
    u = DN_ALPHA * x + mod_ref[0, 2:3, :] * out
    o_ref[0] = _layer_norm(u, g_ref[...], b_ref[...])


def _const_spec(shape, nidx):
    zeros = (0,) * len(shape)
    if nidx == 2:
        return pl.BlockSpec(shape, lambda b, i: zeros, pipeline_mode=pl.Buffered(1))
    return pl.BlockSpec(shape, lambda b, i, e: zeros, pipeline_mode=pl.Buffered(1))


def _merge(x, mod, ya, yb, yc, wgt, wbr, wo, g, b):
    bsz, t, _ = x.shape
    tm = min(512, t)
    tok = pl.BlockSpec((1, tm, D_MODEL), lambda bb, i: (bb, i, 0))
    return pl.pallas_call(
        _merge_kernel,
        grid=(bsz, t // tm),
        in_specs=[tok, pl.BlockSpec((1, 3, D_MODEL), lambda bb, i: (bb, 0, 0)), tok, tok, tok,
                  _const_spec((D_MODEL, 3 * D_MODEL), 2), _const_spec((3, D_MODEL, D_MODEL), 2),
                  _const_spec((D_MODEL, D_MODEL), 2), _const_spec((1, D_MODEL), 2), _const_spec((1, D_MODEL), 2)],
        out_specs=tok,
        out_shape=jax.ShapeDtypeStruct((bsz, t, D_MODEL), F32),
        compiler_params=_cparams(("parallel", "parallel")),
        name="merge_ln",
    )(x, mod, ya, yb, yc, wgt, wbr, wo, g, b)


def _top_rows(x, n, with_rank=False):
    rows = []
    rank = jnp.full(x.shape, float(PEER_NKEYS - 1), F32) if with_rank else None
    for r in range(n):
        m = jnp.max(x, axis=0, keepdims=True)
        rows.append(m)
        hit = x >= m
        if with_rank:
            rank = jnp.where(hit, float(r), rank)
        x = jnp.where(hit, NEG, x)
    return (rows, rank) if with_rank else rows


def _peer_route_kernel(x_ref, mod_ref, wq_ref, sk_ref, h_ref, pj_ref, pb_ref):
    x = x_ref[0]
    h = (x * (1.0 + mod_ref[0, 1:2, :]) + mod_ref[0, 0:1, :]).astype(BF16)
    h_ref[0] = h
    q = _dot(h, wq_ref[...]).astype(BF16)
    kk = PEER_TOPK
    for hd in range(PEER_HEADS):
        qh = q[:, hd * PEER_DKEY:(hd + 1) * PEER_DKEY]
        s1 = _dot_nt(sk_ref[2 * hd], qh)
        s2 = _dot_nt(sk_ref[2 * hd + 1], qh)
        a = _top_rows(s1, kk + 1)
        b, rank2 = _top_rows(s2, kk + 1, with_rank=True)
        b_lo = jnp.concatenate(b[:8], axis=0)
        cand = [a[i] + b_lo for i in range(8)]
        cand.append(a[0] + jnp.concatenate(b[8:16], axis=0))
        cand.append(jnp.concatenate(a[8:16], axis=0) + b[0])
        cand.append(jnp.concatenate([a[0] + b[16], a[16] + b[0]] + [jnp.full_like(a[0], NEG)] * 6, axis=0))
        cand = jnp.concatenate(cand, axis=0)
        top = _top_rows(cand, kk + 1)
        thr = 0.5 * (top[kk - 1] + top[kk])
        mx = a[0] + b[0]
        zs = jnp.sum(jnp.where(cand >= thr, jnp.exp(cand - mx), 0.0), axis=0, keepdims=True)
        t1 = thr - s1
        cnt = jnp.zeros_like(s1)
        for c in range(kk):
            cnt = cnt + jnp.where(b[c] >= t1, 1.0, 0.0)
        e1 = jnp.exp(s1 - a[0]) / zs
        for c in range(cnt.shape[1] // 128):
            cs = slice(c * 128, (c + 1) * 128)
            pj_ref[0, hd, 0, :, c] = cnt[:, cs].reshape(PEER_NKEYS // 8, 8, 128)
            pj_ref[0, hd, 1, :, c] = e1[:, cs].reshape(PEER_NKEYS // 8, 8, 128)
        pb_ref[0, hd, 0] = rank2.astype(BF16)
        pb_ref[0, hd, 1] = jnp.exp(s2 - b[0]).astype(BF16)


def _peer_route(x1, mod, wq, sk):
    bsz, t, _ = x1.shape
    tp = min(256, t)
    pspec = pl.BlockSpec((1, PEER_HEADS, 2, PEER_NKEYS, tp), lambda b, i: (b, 0, 0, 0, i))
    jspec = pl.BlockSpec((1, PEER_HEADS, 2, PEER_NKEYS // 8, tp // 128, 8, 128), lambda b, i: (b, 0, 0, 0, i, 0, 0))
    return pl.pallas_call(
        _peer_route_kernel,
        grid=(bsz, t // tp),
        in_specs=[pl.BlockSpec((1, tp, D_MODEL), lambda b, i: (b, i, 0)),
                  pl.BlockSpec((1, 2, D_MODEL), lambda b, i: (b, 0, 0)),
                  _const_spec((D_MODEL, PEER_HEADS * PEER_DKEY), 2),
                  _const_spec((2 * PEER_HEADS, PEER_NKEYS, PEER_DKEY), 2)],
        out_specs=[pl.BlockSpec((1, tp, D_MODEL), lambda b, i: (b, i, 0)), jspec, pspec],
        out_shape=[jax.ShapeDtypeStruct((bsz, t, D_MODEL), BF16),
                   jax.ShapeDtypeStruct((bsz, PEER_HEADS, 2, PEER_NKEYS // 8, t // 128, 8, 128), F32),
                   jax.ShapeDtypeStruct((bsz, PEER_HEADS, 2, PEER_NKEYS, t), BF16)],
        compiler_params=_cparams(("parallel", "parallel")),
        name="peer_route",
    )(x1, mod, wq, sk)


PEER_LANE_CHUNK = 256


def _bcast_rows(pj_ref, hd, which, r0, cc, lc):
    parts = [pj_ref[0, hd, which, 0, cc * (lc // 128) + c, pl.ds(r0, 16, stride=0), :] for c in range(lc // 128)]
    return jnp.concatenate(parts, axis=1).astype(BF16)


def _peer_dense_kernel(h_ref, u_ref, vt_ref, pj_ref, pb_ref, x_ref, mod_ref, g_ref, b_ref, o_ref,
                       acc_ref, act0_ref, act1_ref, w0_ref, w1_ref, *, eb, nblk):
    e = pl.program_id(2)
    nk = PEER_NKEYS
    tm = acc_ref.shape[1]
    lc = PEER_LANE_CHUNK
    assert tm == 2 * lc and eb == 4 * nk
    zero = jnp.zeros((), BF16)

    @pl.when(e == 0)
    def _():
        acc_ref[...] = jnp.zeros_like(acc_ref)
        act1_ref[...] = jnp.zeros_like(act1_ref)
        w0_ref[...] = jnp.zeros_like(w0_ref)
        w1_ref[...] = jnp.zeros_like(w1_ref)

    def step(par, act_in, act_out, w_in, w_out):
        nj = eb // nk
        rc = nk // 2
        for cc in range(tm // lc):
            ls = slice(cc * lc, (cc + 1) * lc)
            for ri in range(2):
                rr = slice(ri * rc, (ri + 1) * rc)
                for jp in range(nj // 2):
                    w = [None, None]
                    for hd in range(PEER_HEADS):
                        rank = pb_ref[0, hd, 0, rr, ls]
                        e2 = pb_ref[0, hd, 1, rr, ls]
                        for k in range(2):
                            r0 = (1 - par) * nj + jp * 2 + k
                            cnt = _bcast_rows(pj_ref, hd, 0, r0, cc, lc)
                            e1 = _bcast_rows(pj_ref, hd, 1, r0, cc, lc)
                            cnt = jnp.concatenate([cnt] * (rc // 16), axis=0)
                            e1 = jnp.concatenate([e1] * (rc // 16), axis=0)
                            term = jnp.where(rank < cnt, e2, zero) * e1
                            w[k] = term if w[k] is None else w[k] + term
                    for k in range(2):
                        jj = jp * 2 + k
                        rs = slice(jj * nk + ri * rc, jj * nk + (ri + 1) * rc)
                        a = act_in[rs, ls]
                        gelu = 0.5 * a * (1.0 + lax.erf(a * (2.0 ** -0.5)))
                        w_out[rs, ls] = w[k] * gelu.astype(BF16)
                if ri == 0:
                    act_out[:, ls] = _dot_nt(u_ref[...], h_ref[0, ls, :])
                else:
                    acc_ref[:, ls] += _dot(vt_ref[...], w_in[:, ls])

    @pl.when(e % 2 == 0)
    def _():
        step(0, act1_ref, act0_ref, w0_ref, w1_ref)

    @pl.when(e % 2 == 1)
    def _():
        step(1, act0_ref, act1_ref, w1_ref, w0_ref)

    @pl.when(e == pl.num_programs(2) - 1)
    def _():
        f = acc_ref[...].T
        x = x_ref[0]
        u = DN_ALPHA * x + mod_ref[0, 0:1, :] * f
        o_ref[0] = _layer_norm(u, g_ref[...], b_ref[...])


def _peer_dense(h2, u_tab, vt_tab, pj, pb, x1, mod, g, b):
    bsz, t, _ = x1.shape
    tm = min(512, t)
    eb = 512
    tok = pl.BlockSpec((1, tm, D_MODEL), lambda bb, i, e: (bb, i, 0))
    pspec = pl.BlockSpec((1, PEER_HEADS, 2, PEER_NKEYS, tm), lambda bb, i, e: (bb, 0, 0, 0, i))
    nblk = PEER_N // eb
    assert 2 * (eb // PEER_NKEYS) == 8
    jspec = pl.BlockSpec((1, PEER_HEADS, 2, 1, tm // 128, 8, 128),
                         lambda bb, i, e: (bb, 0, 0, jnp.clip(e - 1, 0, nblk - 1) // 2, i, 0, 0))
    return pl.pallas_call(
        functools.partial(_peer_dense_kernel, eb=eb, nblk=nblk),
        grid=(bsz, t // tm, nblk + 2),
        in_specs=[tok,
                  pl.BlockSpec((eb, D_MODEL), lambda bb, i, e: (jnp.minimum(e, nblk - 1), 0)),
                  pl.BlockSpec((D_MODEL, eb), lambda bb, i, e: (0, jnp.clip(e - 2, 0, nblk - 1))),
                  jspec, pspec,
                  tok,
                  pl.BlockSpec((1, 1, D_MODEL), lambda bb, i, e: (bb, 0, 0)),
                  _const_spec((1, D_MODEL), 3), _const_spec((1, D_MODEL), 3)],
        out_specs=tok,
        out_shape=jax.ShapeDtypeStruct((bsz, t, D_MODEL), F32),
        scratch_shapes=[pltpu.VMEM((D_MODEL, tm), F32),
                        pltpu.VMEM((eb, tm), F32), pltpu.VMEM((eb, tm), F32),
                        pltpu.VMEM((eb, tm), BF16), pltpu.VMEM((eb, tm), BF16)],
        compiler_params=_cparams(("parallel", "parallel", "arbitrary")),
        name="peer_dense",
    )(h2, u_tab, vt_tab, pj, pb, x1, mod, g, b)


def _rope_tables(t):
    half = DIFF_D // 2
    inv = jnp.exp(-math.log(ROPE_THETA) * jnp.arange(half, dtype=F32) / half)
    ang = jnp.arange(t, dtype=F32)[:, None] * inv[None, :]
    cos, sin = jnp.cos(ang), jnp.sin(ang)
    cos128 = jnp.tile(cos, (1, 4))
    sin128 = jnp.tile(jnp.concatenate([-sin, sin], axis=1), (1, 2))
    return cos128, sin128


def _layer_params(l, w_in, gla_w_gk, gla_b_gk, gla_norm_g, na_rpb, diff_lambda, diff_norm_g, w_br, w_o,
                  ln1_g, ln1_b, peer_wq, peer_subkeys, peer_u, peer_v, ln2_g, ln2_b):
    w = w_in[l]
    w_gla = jnp.pad(w[:, :3104], ((0, 0), (0, GLA_COLS - 3104))).astype(BF16)
    w_att = w[:, 3104:3104 + ATT_COLS]
    qscale = np.ones((ATT_COLS,), np.float32)
    qscale[0:1024] = NA_DH ** -0.5
    qscale[3072:4096] = DIFF_D ** -0.5
    w_att = (w_att * qscale).astype(BF16)
    w_gt = w[:, 3104 + ATT_COLS:].astype(BF16)
    wgk = jnp.zeros((2, 128, GLA_HEADS * GLA_DK), F32)
    wgk = wgk.at[0, :GLA_RANK].set(gla_w_gk[l, 0]).at[1, GLA_RANK:2 * GLA_RANK].set(gla_w_gk[l, 1]).astype(BF16)
    sk = peer_subkeys[l]
    half = PEER_DKEY // 2
    skp = jnp.zeros((PEER_HEADS, 2, PEER_NKEYS, PEER_DKEY), F32)
    skp = skp.at[:, 0, :, :half].set(sk[:, 0]).at[:, 1, :, half:].set(sk[:, 1])
    return dict(
        w_gla=w_gla, w_att=w_att, w_gt=w_gt, wgk=wgk, bgk=gla_b_gk[l],
        gla_g=gla_norm_g[l][None], na_bias=_na_bias_table(na_rpb[l]),
        lam=diff_lambda[l], diff_g=diff_norm_g[l][None],
        w_br=w_br[l].astype(BF16), w_o=w_o[l].astype(BF16),
        ln1_g=ln1_g[l][None], ln1_b=ln1_b[l][None],
        wq=peer_wq[l].astype(BF16), sk=skp.reshape(2 * PEER_HEADS, PEER_NKEYS, PEER_DKEY).astype(BF16),
        u=peer_u[l].astype(BF16), vt=peer_v[l].T.astype(BF16),
        ln2_g=ln2_g[l][None], ln2_b=ln2_b[l][None],
        lam_init=0.8 - 0.6 * math.exp(-0.3 * l),
    )


def _layer(x, mod, p, rope, cms):
    pa = _proj(x, mod[:, 0:2], p["w_gla"], F32, tn=640, name="proj_gla")
    pb = _proj(x, mod[:, 0:2], p["w_att"], BF16, tn=1024, rope=rope + ((3, 4),), name="proj_att")
    ofwd = _gla_dir(pa, cms[0], p["wgk"][0], p["bgk"][0:1], reverse=False)
    ya = _gla_dir(pa, cms[1], p["wgk"][1], p["bgk"][1:2], reverse=True, ofwd=ofwd, norm_g=p["gla_g"])
    yb = _na(pb, p["na_bias"])
    yc = _diff(pb, p["lam"], p["diff_g"], p["lam_init"])
    x1 = _merge(x, mod[:, 0:3], ya, yb, yc, p["w_gt"], p["w_br"], p["w_o"], p["ln1_g"], p["ln1_b"])
    h2, pj, pb2 = _peer_route(x1, mod[:, 3:5], p["wq"], p["sk"])
    return _peer_dense(h2, p["u"], p["vt"], pj, pb2, x1, mod[:, 5:6], p["ln2_g"], p["ln2_b"])


def kernel(x_prompt, x_sample, c_prompt, c_sample, w_ada, b_ada, w_in, gla_w_gk, gla_b_gk, gla_norm_g, na_rpb, diff_lambda, diff_norm_g, w_br, w_o, ln1_g, ln1_b, peer_wq, peer_subkeys, peer_u, peer_v, ln2_g, ln2_b):
    nb = x_prompt.shape[0]
    c_all = jnp.concatenate([c_prompt, c_sample], axis=0)
    cms = (jnp.asarray(_gla_masks(False)), jnp.asarray(_gla_masks(True)))
    ropes = {x.shape[1]: _rope_tables(x.shape[1]) for x in (x_prompt, x_sample)}
    xs = [x_prompt, x_sample]
    for l in range(DEPTH):
        p = _layer_params(l, w_in, gla_w_gk, gla_b_gk, gla_norm_g, na_rpb, diff_lambda, diff_norm_g, w_br, w_o,
                          ln1_g, ln1_b, peer_wq, peer_subkeys, peer_u, peer_v, ln2_g, ln2_b)
        mod_all = _ada(c_all, w_ada[l].astype(BF16), b_ada[l][None]).reshape(c_all.shape[0], 6, D_MODEL)
        mods = [mod_all[:nb], mod_all[nb:]]
        xs = [_layer(x, m, p, ropes[x.shape[1]], cms) for x, m in zip(xs, mods)]
    return (xs[0], xs[1])
```

```python
import functools
import math

import numpy as np
import jax
import jax.numpy as jnp
from jax import lax
from jax.experimental import pallas as pl
from jax.experimental.pallas import tpu as pltpu

F32 = jnp.float32
BF16 = jnp.bfloat16

D_MODEL = 1024
DEPTH = 2
GRID_W = 64
GLA_HEADS, GLA_DK, GLA_DV, GLA_RANK, GLA_TAU, GLA_CHUNK = 4, 128, 256, 16, 16.0, 64
GLA_SUB = 16
NA_HEADS, NA_DH, NA_KR, NA_KC = 16, 64, 8, 16
NA_ROW_UNROLL = 4
DIFF_HEADS, DIFF_D = 8, 64
DIFF_KEY_CHUNK = 1024
ROPE_THETA = 10000.0
PEER_HEADS, PEER_NKEYS, PEER_DKEY, PEER_TOPK = 8, 128, 128, 16
PEER_N = PEER_NKEYS * PEER_NKEYS
DN_ALPHA = (2 * DEPTH) ** 0.25
LN_EPS = 1e-5
NEG = -1e30

GLA_COLS = 3200
ATT_COLS = 6144
VMEM_LIMIT = 56 * 1024 * 1024


def _cparams(sem):
    return pltpu.CompilerParams(dimension_semantics=sem, vmem_limit_bytes=VMEM_LIMIT)


def _dot(a, b):
    return jnp.dot(a, b, preferred_element_type=F32)


def _dot_nt(a, b):
    return lax.dot_general(a, b, (((1,), (1,)), ((), ())), preferred_element_type=F32)


def _dot_tn(a, b):
    return lax.dot_general(a, b, (((0,), (0,)), ((), ())), preferred_element_type=F32)


def _sigmoid(x):
    return 1.0 / (1.0 + jnp.exp(-x))


def _layer_norm(u, g, b):
    mu = jnp.mean(u, axis=-1, keepdims=True)
    d = u - mu
    var = jnp.mean(d * d, axis=-1, keepdims=True)
    return d * lax.rsqrt(var + LN_EPS) * g + b


def _ada_kernel(c_ref, w_ref, b_ref, o_ref):
    c = c_ref[...]
    s = (c * _sigmoid(c)).astype(BF16)
    o_ref[...] = _dot(s, w_ref[...]) + b_ref[...]


def _ada(c, w, b):
    n = c.shape[0]
    cols = w.shape[1]
    return pl.pallas_call(
        _ada_kernel,
        grid=(cols // D_MODEL,),
        in_specs=[pl.BlockSpec((n, D_MODEL), lambda j: (0, 0)),
                  pl.BlockSpec((D_MODEL, D_MODEL), lambda j: (0, j)),
                  pl.BlockSpec((1, D_MODEL), lambda j: (0, j))],
        out_specs=pl.BlockSpec((n, D_MODEL), lambda j: (0, j)),
        out_shape=jax.ShapeDtypeStruct((n, cols), F32),
        compiler_params=_cparams(("arbitrary",)),
        name="ada_mod",
    )(c, w, b)


def _proj_kernel(x_ref, mod_ref, w_ref, *rest, rope_tiles):
    if rope_tiles:
        cos_ref, sin_ref, o_ref, h_ref = rest
    else:
        o_ref, h_ref = rest
    j = pl.program_id(2)

    @pl.when(j == 0)
    def _():
        x = x_ref[0]
        h_ref[...] = (x * (1.0 + mod_ref[0, 1:2, :]) + mod_ref[0, 0:1, :]).astype(BF16)

    acc = _dot(h_ref[...], w_ref[...])
    if not rope_tiles:
        o_ref[0] = acc.astype(o_ref.dtype)
        return

    is_rope = functools.reduce(jnp.logical_or, [j == t for t in rope_tiles])

    @pl.when(is_rope)
    def _():
        cos = cos_ref[...]
        sin = sin_ref[...]
        lane = lax.broadcasted_iota(jnp.int32, cos.shape, 1)
        first = (lane % DIFF_D) < (DIFF_D // 2)
        for c in range(acc.shape[1] // 128):
            xc = acc[:, c * 128:(c + 1) * 128]
            rot = jnp.where(first, pltpu.roll(xc, 128 - DIFF_D // 2, 1), pltpu.roll(xc, DIFF_D // 2, 1))
            o_ref[0, :, c * 128:(c + 1) * 128] = (xc * cos + rot * sin).astype(o_ref.dtype)

    @pl.when(jnp.logical_not(is_rope))
    def _():
        o_ref[0] = acc.astype(o_ref.dtype)


def _proj(x, mod, w, out_dtype, tn, rope=None, name="proj"):
    bsz, t, _ = x.shape
    cols = w.shape[1]
    tm = min(1024, t)
    in_specs = [pl.BlockSpec((1, tm, D_MODEL), lambda b, i, j: (b, i, 0)),
                pl.BlockSpec((1, 2, D_MODEL), lambda b, i, j: (b, 0, 0)),
                pl.BlockSpec((D_MODEL, tn), lambda b, i, j: (0, j))]
    args = [x, mod, w]
    rope_tiles = ()
    if rope is not None:
        cos, sin, rope_tiles = rope
        in_specs += [pl.BlockSpec((tm, 128), lambda b, i, j: (i, 0)),
                     pl.BlockSpec((tm, 128), lambda b, i, j: (i, 0))]
        args += [cos, sin]
    return pl.pallas_call(
        functools.partial(_proj_kernel, rope_tiles=tuple(rope_tiles)),
        grid=(bsz, t // tm, cols // tn),
        in_specs=in_specs,
        out_specs=pl.BlockSpec((1, tm, tn), lambda b, i, j: (b, i, j)),
        out_shape=jax.ShapeDtypeStruct((bsz, t, cols), out_dtype),
        scratch_shapes=[pltpu.VMEM((tm, D_MODEL), BF16)],
        compiler_params=_cparams(("parallel", "parallel", "arbitrary")),
        name=name,
    )(*args)


def _gla_masks(reverse):
    c, s = GLA_CHUNK, GLA_SUB
    i = np.arange(c)[:, None]
    j = np.arange(c)[None, :]
    if not reverse:
        cum = (j <= i)
        ref = (j < (i // s) * s)
    else:
        cum = (j >= i)
        ref = (j >= (i // s + 1) * s)
    return np.concatenate([cum, ref], axis=0).astype(np.float32)


def _gla_kernel(q_ref, k_ref, v_ref, glr_ref, cm_ref, wgk_ref, bgk_ref, *rest, reverse, n_chunks):
    if reverse:
        ofwd_ref, r_ref, ng_ref, o_ref, st_ref, acc_ref = rest
    else:
        o_ref, st_ref = rest
        acc_ref = o_ref.at[0]
    c, s = GLA_CHUNK, GLA_SUB
    nsub = c // s
    dk = GLA_DK

    @pl.when(pl.program_id(2) == 0)
    def _():
        st_ref[...] = jnp.zeros_like(st_ref)

    def lanes(x):
        return jnp.concatenate([x[ci * c:(ci + 1) * c] for ci in range(n_chunks)], axis=1)

    z = _dot(glr_ref[0].astype(BF16), wgk_ref[...]) + bgk_ref[...]
    g = lanes((jnp.minimum(z, 0.0) - jnp.log(1.0 + jnp.exp(-jnp.abs(z)))) * (1.0 / GLA_TAU))
    g_hi = g.astype(BF16)
    g_lo = (g - g_hi.astype(F32)).astype(BF16)
    cm = cm_ref[...].astype(BF16)
    cums = _dot(cm, g_hi) + _dot(cm, g_lo)
    bc, bref = cums[:c], cums[c:]
    q = lanes(q_ref[0]) * (GLA_DK ** -0.5)
    k = lanes(k_ref[0])
    qe = q * jnp.exp(bc - bref)
    qi = (qe * jnp.exp(bref)).astype(BF16)
    qe = qe.astype(BF16)
    bl = bc[0:1] if reverse else bc[c - 1:c]
    dec = jnp.exp(bl)
    kl = (k * jnp.exp(bl - bc)).astype(BF16)
    row = lax.broadcasted_iota(jnp.int32, (c, 1), 0)
    ksub = []
    for si in range(nsub):
        ok = (row >= si * s) if reverse else (row < (si + 1) * s)
        ksub.append(jnp.where(ok, k * jnp.exp(bref[si * s:si * s + 1] - bc), 0.0).astype(BF16))
    rr = lax.broadcasted_iota(jnp.int32, (c, nsub * c), 0)
    cc = lax.broadcasted_iota(jnp.int32, (c, nsub * c), 1)
    causal = (cc % c > rr) if reverse else (cc % c <= rr)
    keep = jnp.logical_and(cc // c == rr // s, causal)
    vs = [v_ref[0, ci * c:(ci + 1) * c, :].astype(BF16) for ci in range(n_chunks)]
    atts = []
    for ci in range(n_chunks):
        ls = slice(ci * dk, (ci + 1) * dk)
        kcat = jnp.concatenate([ks[:, ls] for ks in ksub], axis=0)
        atts.append(_dot_nt(qe[:, ls], kcat))
    atts = [jnp.where(keep, a, 0.0).astype(BF16) for a in atts]
    for ci in range(n_chunks):
        acc_ref[ci * c:(ci + 1) * c, :] = _dot(atts[ci], jnp.concatenate([vs[ci]] * nsub, axis=0))
    uts = [_dot_tn(vs[ci], kl[:, ci * dk:(ci + 1) * dk]) for ci in range(n_chunks)]

    st = st_ref[...]
    order = range(n_chunks - 1, -1, -1) if reverse else range(n_chunks)
    for ci in order:
        ls = slice(ci * dk, (ci + 1) * dk)
        inter = _dot_nt(qi[:, ls], st.astype(BF16))
        acc_ref[ci * c:(ci + 1) * c, :] = acc_ref[ci * c:(ci + 1) * c, :] + inter
        st = st * dec[:, ls] + uts[ci]
    st_ref[...] = st

    if reverse:
        y = ofwd_ref[0] + acc_ref[...]
        y = y * lax.rsqrt(jnp.mean(y * y, axis=-1, keepdims=True) + LN_EPS) * ng_ref[...]
        r = r_ref[0]
        o_ref[0] = (y * (r * _sigmoid(r))).astype(o_ref.dtype)


def _gla_dir(pa, cm, wgk, bgk, reverse, ofwd=None, norm_g=None):
    bsz, t, _ = pa.shape
    tt = min(512, t)
    nt = t // tt

    def ti(i):
        return (nt - 1 - i) if reverse else i

    in_specs = [pl.BlockSpec((1, tt, GLA_DK), lambda b, h, i: (b, ti(i), h)),
                pl.BlockSpec((1, tt, GLA_DK), lambda b, h, i: (b, ti(i), GLA_HEADS + h)),
                pl.BlockSpec((1, tt, GLA_DV), lambda b, h, i: (b, ti(i), GLA_HEADS + h)),
                pl.BlockSpec((1, tt, 128), lambda b, h, i: (b, ti(i), 3072 // 128)),
                pl.BlockSpec((2 * GLA_CHUNK, GLA_CHUNK), lambda b, h, i: (0, 0)),
                pl.BlockSpec((128, GLA_DK), lambda b, h, i: (0, h)),
                pl.BlockSpec((1, GLA_DK), lambda b, h, i: (0, h))]
    args = [pa, pa, pa, pa, cm, wgk, bgk]
    scratch = [pltpu.VMEM((GLA_DV, GLA_DK), F32)]
    if reverse:
        in_specs += [pl.BlockSpec((1, tt, GLA_DV), lambda b, h, i: (b, ti(i), h)),
                     pl.BlockSpec((1, tt, GLA_DV), lambda b, h, i: (b, ti(i), 2 * GLA_HEADS + h)),
                     pl.BlockSpec((1, GLA_DV), lambda b, h, i: (0, 0))]
        args += [ofwd, pa, norm_g]
        scratch += [pltpu.VMEM((tt, GLA_DV), F32)]
        out_dtype = BF16
    else:
        out_dtype = F32
    return pl.pallas_call(
        functools.partial(_gla_kernel, reverse=reverse, n_chunks=tt // GLA_CHUNK),
        grid=(bsz, GLA_HEADS, nt),
        in_specs=in_specs,
        out_specs=pl.BlockSpec((1, tt, GLA_DV), lambda b, h, i: (b, ti(i), h)),
        out_shape=jax.ShapeDtypeStruct((bsz, t, GLA_HEADS * GLA_DV), out_dtype),
        scratch_shapes=scratch,
        compiler_params=_cparams(("parallel", "parallel", "arbitrary")),
        name="gla_bwd" if reverse else "gla_fwd",
    )(*args)


def _na_bias_table(rpb):
    qc = np.arange(GRID_W)[:, None]
    kc = np.arange(GRID_W)[None, :]
    cs = np.clip(qc - NA_KC // 2, 0, GRID_W - NA_KC)
    allowed = (kc >= cs) & (kc < cs + NA_KC)
    dc = np.clip(kc - qc + NA_KC - 1, 0, 2 * NA_KC - 2)
    onehot = (dc[None] == np.arange(2 * NA_KC - 1)[:, None, None]).astype(np.float32)
    tab = jnp.einsum("hrd,dqk->hrqk", rpb.astype(F32), jnp.asarray(onehot), precision=lax.Precision.HIGHEST)
    tab = jnp.where(jnp.asarray(allowed)[None, None], tab, NEG)
    tab = jnp.stack([tab[:, NA_KR - 1 - d:2 * NA_KR - 1 - d] for d in range(NA_KR)], axis=1)
    tab = tab.transpose(0, 1, 3, 2, 4).reshape(NA_HEADS // 2, 2, NA_KR, GRID_W, NA_KR * GRID_W)
    return tab.transpose(0, 2, 1, 3, 4).reshape(NA_HEADS // 2, NA_KR, 2 * GRID_W, NA_KR * GRID_W)


def _na_kernel(q_ref, k_ref, v_ref, bias_ref, o_ref, *, rows):
    lane = lax.broadcasted_iota(jnp.int32, (GRID_W, 128), 1)
    low = lane < NA_DH
    win = NA_KR * GRID_W
    nr = NA_ROW_UNROLL

    def body(it, carry):
        q0s, scs, vws = [], [], []
        for u in range(nr):
            r = it * nr + u
            rs = jnp.clip(r - NA_KR // 2, 0, rows - NA_KR)
            q0 = pl.multiple_of(r * GRID_W, GRID_W)
            k0 = pl.multiple_of(rs * GRID_W, GRID_W)
            q = q_ref[0, pl.ds(q0, GRID_W), :]
            zq = jnp.zeros_like(q)
            q2 = jnp.concatenate([jnp.where(low, q, zq), jnp.where(low, zq, q)], axis=0)
            scs.append(_dot_nt(q2, k_ref[0, pl.ds(k0, win), :]) + bias_ref[0, r - rs])
            vws.append(v_ref[0, pl.ds(k0, win), :])
            q0s.append(q0)
        ms = [jnp.max(sc, axis=-1, keepdims=True) for sc in scs]
        es = [jnp.exp(sc - m) for sc, m in zip(scs, ms)]
        zs = [jnp.sum(e, axis=-1, keepdims=True) for e in es]
        os_ = [_dot(e.astype(BF16), vw) / z for e, vw, z in zip(es, vws, zs)]
        for q0, o in zip(q0s, os_):
            o_ref[0, pl.ds(q0, GRID_W), :] = jnp.where(low, o[:GRID_W], o[GRID_W:]).astype(o_ref.dtype)
        return carry

    lax.fori_loop(0, rows // nr, body, 0)


def _na(pb, bias):
    bsz, t, _ = pb.shape
    rows = t // GRID_W
    assert rows >= NA_KR and rows % NA_ROW_UNROLL == 0
    npair = NA_HEADS // 2
    return pl.pallas_call(
        functools.partial(_na_kernel, rows=rows),
        grid=(bsz, npair),
        in_specs=[pl.BlockSpec((1, t, 128), lambda b, j: (b, 0, j)),
                  pl.BlockSpec((1, t, 128), lambda b, j: (b, 0, npair + j)),
                  pl.BlockSpec((1, t, 128), lambda b, j: (b, 0, 2 * npair + j)),
                  pl.BlockSpec((1, NA_KR, 2 * GRID_W, NA_KR * GRID_W), lambda b, j: (j, 0, 0, 0))],
        out_specs=pl.BlockSpec((1, t, 128), lambda b, j: (b, 0, j)),
        out_shape=jax.ShapeDtypeStruct((bsz, t, D_MODEL), BF16),
        compiler_params=_cparams(("parallel", "parallel")),
        name="na_attn",
    )(pb, pb, pb, bias)


def _diff_kernel(q_ref, k_ref, v_ref, lam_ref, ng_ref, o_ref, vt_ref, *, lam_init):
    dv = 2 * DIFF_D

    @pl.when(pl.program_id(2) == 0)
    def _():
        vt_ref[0:dv, :] = v_ref[0].astype(F32).T.astype(BF16)
        vt_ref[dv:, :] = jnp.ones((8, vt_ref.shape[1]), BF16)

    lv = lam_ref[...]
    l1 = jnp.sum(lv[0:1] * lv[1:2], axis=-1, keepdims=True)
    l2 = jnp.sum(lv[2:3] * lv[3:4], axis=-1, keepdims=True)
    lam = jnp.exp(l1) - jnp.exp(l2) + lam_init
    q = q_ref[0]
    k = k_ref[0]
    low = lax.broadcasted_iota(jnp.int32, q.shape, 1) < DIFF_D
    zq = jnp.zeros_like(q)
    t = k.shape[0]
    kc = min(DIFF_KEY_CHUNK, t)
    nc = t // kc
    qms = (jnp.where(low, q, zq), jnp.where(low, zq, q))
    st = [dict(m=None, acc=None) for _ in qms]
    sc = [[None] * nc for _ in qms]
    for p_, qm in enumerate(qms):
        sc[p_][0] = _dot_nt(k[:kc], qm)
    for c in range(nc):
        for p_, qm in enumerate(qms):
            if c + 1 < nc:
                sc[p_][c + 1] = _dot_nt(k[(c + 1) * kc:(c + 2) * kc], qm)
            s_ = sc[p_][c]
            d = st[p_]
            cm = jnp.max(s_, axis=0, keepdims=True)
            vc = vt_ref[:, c * kc:(c + 1) * kc]
            if c == 0:
                d["m"] = cm
                d["acc"] = _dot(vc, jnp.exp2(s_ - cm).astype(BF16))
            else:
                m_new = jnp.maximum(d["m"], cm)
                d["acc"] = jnp.exp2(d["m"] - m_new) * d["acc"] + _dot(vc, jnp.exp2(s_ - m_new).astype(BF16))
                d["m"] = m_new
    outs = [d["acc"][:dv] / d["acc"][dv:dv + 1] for d in st]
    o = outs[0] - lam * outs[1]
    y = o * lax.rsqrt(jnp.mean(o * o, axis=0, keepdims=True) + LN_EPS)
    o_ref[0] = (y.T * (ng_ref[...] * (1.0 - lam_init))).astype(o_ref.dtype)


def _diff(pb, lam_vec, norm_g, lam_init):
    bsz, t, _ = pb.shape
    tq = min(512, t)
    base = (NA_HEADS * NA_DH * 3) // 128
    return pl.pallas_call(
        functools.partial(_diff_kernel, lam_init=lam_init),
        grid=(bsz, DIFF_HEADS, t // tq),
        in_specs=[pl.BlockSpec((1, tq, 128), lambda b, h, i: (b, i, base + h)),
                  pl.BlockSpec((1, t, 128), lambda b, h, i: (b, 0, base + DIFF_HEADS + h)),
                  pl.BlockSpec((1, t, 128), lambda b, h, i: (b, 0, base + 2 * DIFF_HEADS + h)),
                  pl.BlockSpec((4, DIFF_D), lambda b, h, i: (0, 0)),
                  pl.BlockSpec((1, 2 * DIFF_D), lambda b, h, i: (0, 0))],
        out_specs=pl.BlockSpec((1, tq, 128), lambda b, h, i: (b, i, h)),
        out_shape=jax.ShapeDtypeStruct((bsz, t, D_MODEL), BF16),
        scratch_shapes=[pltpu.VMEM((2 * DIFF_D + 8, t), BF16)],
        compiler_params=_cparams(("parallel", "parallel", "arbitrary")),
        name="diff_attn",
    )(pb, pb, pb, lam_vec, norm_g)


def _merge_kernel(x_ref, mod_ref, ya_ref, yb_ref, yc_ref, wgt_ref, wbr_ref, wo_ref, g_ref, b_ref, o_ref):
    x = x_ref[0]
    h = (x * (1.0 + mod_ref[0, 1:2, :]) + mod_ref[0, 0:1, :]).astype(BF16)
    merged = None
    for n, y_ref in enumerate((ya_ref, yb_ref, yc_ref)):
        gt = _dot(h, wgt_ref[:, n * D_MODEL:(n + 1) * D_MODEL])
        br = _dot(y_ref[0], wbr_ref[n])
        term = _sigmoid(gt) * br
        merged = term if merged is None else merged + term
    out = _dot(merged.astype(BF16), wo_ref[...])
    u = DN_ALPHA * x + mod_ref[0, 2:3, :] * out
    o_ref[0] = _layer_norm(u, g_ref[...], b_ref[...])


def _const_spec(shape, nidx):
    zeros = (0,) * len(shape)
    if nidx == 2:
        return pl.BlockSpec(shape, lambda b, i: zeros, pipeline_mode=pl.Buffered(1))
    return pl.BlockSpec(shape, lambda b, i, e: zeros, pipeline_mode=pl.Buffered(1))


def _merge(x, mod, ya, yb, yc, wgt, wbr, wo, g, b):
    bsz, t, _ = x.shape
    tm = min(512, t)
    tok = pl.BlockSpec((1, tm, D_MODEL), lambda bb, i: (bb, i, 0))
    return pl.pallas_call(
        _merge_kernel,
        grid=(bsz, t // tm),
        in_specs=[tok, pl.BlockSpec((1, 3, D_MODEL), lambda bb, i: (bb, 0, 0)), tok, tok, tok,
                  _const_spec((D_MODEL, 3 * D_MODEL), 2), _const_spec((3, D_MODEL, D_MODEL), 2),
                  _const_spec((D_MODEL, D_MODEL), 2), _const_spec((1, D_MODEL), 2), _const_spec((1, D_MODEL), 2)],
        out_specs=tok,
        out_shape=jax.ShapeDtypeStruct((bsz, t, D_MODEL), F32),
        compiler_params=_cparams(("parallel", "parallel")),
        name="merge_ln",
    )(x, mod, ya, yb, yc, wgt, wbr, wo, g, b)


def _top_rows(x, n, with_rank=False):
    rows = []
    rank = jnp.full(x.shape, float(PEER_NKEYS - 1), F32) if with_rank else None
    for r in range(n):
        m = jnp.max(x, axis=0, keepdims=True)
        rows.append(m)
        hit = x >= m
        if with_rank:
            rank = jnp.where(hit, float(r), rank)
        x = jnp.where(hit, NEG, x)
    return (rows, rank) if with_rank else rows


def _peer_route_kernel(x_ref, mod_ref, wq_ref, sk_ref, h_ref, pj_ref, pb_ref):
    x = x_ref[0]
    h = (x * (1.0 + mod_ref[0, 1:2, :]) + mod_ref[0, 0:1, :]).astype(BF16)
    h_ref[0] = h
    q = _dot(h, wq_ref[...]).astype(BF16)
    kk = PEER_TOPK
    for hd in range(PEER_HEADS):
        qh = q[:, hd * PEER_DKEY:(hd + 1) * PEER_DKEY]
        s1 = _dot_nt(sk_ref[2 * hd], qh)
        s2 = _dot_nt(sk_ref[2 * hd + 1], qh)
        a = _top_rows(s1, kk + 1)
        b, rank2 = _top_rows(s2, kk + 1, with_rank=True)
        b_lo = jnp.concatenate(b[:8], axis=0)
        cand = [a[i] + b_lo for i in range(8)]
        cand.append(a[0] + jnp.concatenate(b[8:16], axis=0))
        cand.append(jnp.concatenate(a[8:16], axis=0) + b[0])
        cand.append(jnp.concatenate([a[0] + b[16], a[16] + b[0]] + [jnp.full_like(a[0], NEG)] * 6, axis=0))
        cand = jnp.concatenate(cand, axis=0)
        top = _top_rows(cand, kk + 1)
        thr = 0.5 * (top[kk - 1] + top[kk])
        mx = a[0] + b[0]
        zs = jnp.sum(jnp.where(cand >= thr, jnp.exp(cand - mx), 0.0), axis=0, keepdims=True)
        t1 = thr - s1
        cnt = jnp.zeros_like(s1)
        for c in range(kk):
            cnt = cnt + jnp.where(b[c] >= t1, 1.0, 0.0)
        e1 = jnp.exp(s1 - a[0]) / zs
        for c in range(cnt.shape[1] // 128):
            cs = slice(c * 128, (c + 1) * 128)
            pj_ref[0, hd, 0, :, c] = cnt[:, cs].reshape(PEER_NKEYS // 8, 8, 128)
            pj_ref[0, hd, 1, :, c] = e1[:, cs].reshape(PEER_NKEYS // 8, 8, 128)
        pb_ref[0, hd, 0] = rank2.astype(BF16)
        pb_ref[0, hd, 1] = jnp.exp(s2 - b[0]).astype(BF16)


def _peer_route(x1, mod, wq, sk):
    bsz, t, _ = x1.shape
    tp = min(256, t)
    pspec = pl.BlockSpec((1, PEER_HEADS, 2, PEER_NKEYS, tp), lambda b, i: (b, 0, 0, 0, i))
    jspec = pl.BlockSpec((1, PEER_HEADS, 2, PEER_NKEYS // 8, tp // 128, 8, 128), lambda b, i: (b, 0, 0, 0, i, 0, 0))
    return pl.pallas_call(
        _peer_route_kernel,
        grid=(bsz, t // tp),
        in_specs=[pl.BlockSpec((1, tp, D_MODEL), lambda b, i: (b, i, 0)),
                  pl.BlockSpec((1, 2, D_MODEL), lambda b, i: (b, 0, 0)),
                  _const_spec((D_MODEL, PEER_HEADS * PEER_DKEY), 2),
                  _const_spec((2 * PEER_HEADS, PEER_NKEYS, PEER_DKEY), 2)],
        out_specs=[pl.BlockSpec((1, tp, D_MODEL), lambda b, i: (b, i, 0)), jspec, pspec],
        out_shape=[jax.ShapeDtypeStruct((bsz, t, D_MODEL), BF16),
                   jax.ShapeDtypeStruct((bsz, PEER_HEADS, 2, PEER_NKEYS // 8, t // 128, 8, 128), F32),
                   jax.ShapeDtypeStruct((bsz, PEER_HEADS, 2, PEER_NKEYS, t), BF16)],
        compiler_params=_cparams(("parallel", "parallel")),
        name="peer_route",
    )(x1, mod, wq, sk)


PEER_LANE_CHUNK = 256


def _bcast_rows(pj_ref, hd, which, r0, cc, lc):
    parts = [pj_ref[0, hd, which, 0, cc * (lc // 128) + c, pl.ds(r0, 16, stride=0), :] for c in range(lc // 128)]
    return jnp.concatenate(parts, axis=1).astype(BF16)


def _peer_dense_kernel(h_ref, u_ref, vt_ref, pj_ref, pb_ref, x_ref, mod_ref, g_ref, b_ref, o_ref,
                       acc_ref, act0_ref, act1_ref, w0_ref, w1_ref, *, eb, nblk):
    e = pl.program_id(2)
    nk = PEER_NKEYS
    tm = acc_ref.shape[1]
    lc = PEER_LANE_CHUNK
    assert tm == 2 * lc and eb == 4 * nk
    zero = jnp.zeros((), BF16)

    @pl.when(e == 0)
    def _():
        acc_ref[...] = jnp.zeros_like(acc_ref)
        act1_ref[...] = jnp.zeros_like(act1_ref)
        w0_ref[...] = jnp.zeros_like(w0_ref)
        w1_ref[...] = jnp.zeros_like(w1_ref)

    def step(par, act_in, act_out, w_in, w_out):
        nj = eb // nk
        rc = nk // 2
        for cc in range(tm // lc):
            ls = slice(cc * lc, (cc + 1) * lc)
            for ri in range(2):
                rr = slice(ri * rc, (ri + 1) * rc)
                for jp in range(nj // 2):
                    if ri == 0:
                        us = slice(jp * (eb // 2), (jp + 1) * (eb // 2))
                        act_out[us, ls] = _dot_nt(u_ref[us, :], h_ref[0, ls, :])
                    else:
                        ds_ = slice(jp * (D_MODEL // 2), (jp + 1) * (D_MODEL // 2))
                        acc_ref[ds_, ls] += _dot(vt_ref[ds_, :], w_in[:, ls])
                    w = [None, None]
                    for hd in range(PEER_HEADS):
                        rank = pb_ref[0, hd, 0, rr, ls]
                        e2 = pb_ref[0, hd, 1, rr, ls]
                        for k in range(2):
                            r0 = (1 - par) * nj + jp * 2 + k
                            cnt = _bcast_rows(pj_ref, hd, 0, r0, cc, lc)
                            e1 = _bcast_rows(pj_ref, hd, 1, r0, cc, lc)
                            cnt = jnp.concatenate([cnt] * (rc // 16), axis=0)
                            e1 = jnp.concatenate([e1] * (rc // 16), axis=0)
                            term = jnp.where(rank < cnt, e2, zero) * e1
                            w[k] = term if w[k] is None else w[k] + term
                    for k in range(2):
                        jj = jp * 2 + k
                        rs = slice(jj * nk + ri * rc, jj * nk + (ri + 1) * rc)
                        a = act_in[rs, ls]
                        gelu = 0.5 * a * (1.0 + lax.erf(a * (2.0 ** -0.5)))
                        w_out[rs, ls] = w[k] * gelu.astype(BF16)

    @pl.when(e % 2 == 0)
    def _():
        step(0, act1_ref, act0_ref, w0_ref, w1_ref)

    @pl.when(e % 2 == 1)
    def _():
        step(1, act0_ref, act1_ref, w1_ref, w0_ref)

    @pl.when(e == pl.num_programs(2) - 1)
    def _():
        f = acc_ref[...].T
        x = x_ref[0]
        u = DN_ALPHA * x + mod_ref[0, 0:1, :] * f
        o_ref[0] = _layer_norm(u, g_ref[...], b_ref[...])


def _peer_dense(h2, u_tab, vt_tab, pj, pb, x1, mod, g, b):
    bsz, t, _ = x1.shape
    tm = min(512, t)
    eb = 512
    tok = pl.BlockSpec((1, tm, D_MODEL), lambda bb, i, e: (bb, i, 0))
    pspec = pl.BlockSpec((1, PEER_HEADS, 2, PEER_NKEYS, tm), lambda bb, i, e: (bb, 0, 0, 0, i))
    nblk = PEER_N // eb
    assert 2 * (eb // PEER_NKEYS) == 8
    jspec = pl.BlockSpec((1, PEER_HEADS, 2, 1, tm // 128, 8, 128),
                         lambda bb, i, e: (bb, 0, 0, jnp.clip(e - 1, 0, nblk - 1) // 2, i, 0, 0))
    return pl.pallas_call(
        functools.partial(_peer_dense_kernel, eb=eb, nblk=nblk),
        grid=(bsz, t // tm, nblk + 2),
        in_specs=[tok,
                  pl.BlockSpec((eb, D_MODEL), lambda bb, i, e: (jnp.minimum(e, nblk - 1), 0)),
                  pl.BlockSpec((D_MODEL, eb), lambda bb, i, e: (0, jnp.clip(e - 2, 0, nblk - 1))),
                  jspec, pspec,
                  tok,
                  pl.BlockSpec((1, 1, D_MODEL), lambda bb, i, e: (bb, 0, 0)),
                  _const_spec((1, D_MODEL), 3), _const_spec((1, D_MODEL), 3)],
        out_specs=tok,
        out_shape=jax.ShapeDtypeStruct((bsz, t, D_MODEL), F32),
        scratch_shapes=[pltpu.VMEM((D_MODEL, tm), F32),
                        pltpu.VMEM((eb, tm), F32), pltpu.VMEM((eb, tm), F32),
                        pltpu.VMEM((eb, tm), BF16), pltpu.VMEM((eb, tm), BF16)],
        compiler_params=_cparams(("parallel", "parallel", "arbitrary")),
        name="peer_dense",
    )(h2, u_tab, vt_tab, pj, pb, x1, mod, g, b)


def _rope_tables(t):
    half = DIFF_D // 2
    inv = jnp.exp(-math.log(ROPE_THETA) * jnp.arange(half, dtype=F32) / half)
    ang = jnp.arange(t, dtype=F32)[:, None] * inv[None, :]
    cos, sin = jnp.cos(ang), jnp.sin(ang)
    cos128 = jnp.tile(cos, (1, 4))
    sin128 = jnp.tile(jnp.concatenate([-sin, sin], axis=1), (1, 2))
    return cos128, sin128


def _layer_params(l, w_in, gla_w_gk, gla_b_gk, gla_norm_g, na_rpb, diff_lambda, diff_norm_g, w_br, w_o,
                  ln1_g, ln1_b, peer_wq, peer_subkeys, peer_u, peer_v, ln2_g, ln2_b):
    w = w_in[l]
    w_gla = jnp.pad(w[:, :3104], ((0, 0), (0, GLA_COLS - 3104))).astype(BF16)
    w_att = w[:, 3104:3104 + ATT_COLS]
    qscale = np.ones((ATT_COLS,), np.float32)
    qscale[0:1024] = NA_DH ** -0.5
    qscale[3072:4096] = DIFF_D ** -0.5 * math.log2(math.e)
    w_att = (w_att * qscale).astype(BF16)
    w_gt = w[:, 3104 + ATT_COLS:].astype(BF16)
    wgk = jnp.zeros((2, 128, GLA_HEADS * GLA_DK), F32)
    wgk = wgk.at[0, :GLA_RANK].set(gla_w_gk[l, 0]).at[1, GLA_RANK:2 * GLA_RANK].set(gla_w_gk[l, 1]).astype(BF16)
    sk = peer_subkeys[l]
    half = PEER_DKEY // 2
    skp = jnp.zeros((PEER_HEADS, 2, PEER_NKEYS, PEER_DKEY), F32)
    skp = skp.at[:, 0, :, :half].set(sk[:, 0]).at[:, 1, :, half:].set(sk[:, 1])
    return dict(
        w_gla=w_gla, w_att=w_att, w_gt=w_gt, wgk=wgk, bgk=gla_b_gk[l],
        gla_g=gla_norm_g[l][None], na_bias=_na_bias_table(na_rpb[l]),
        lam=diff_lambda[l], diff_g=diff_norm_g[l][None],
        w_br=w_br[l].astype(BF16), w_o=w_o[l].astype(BF16),
        ln1_g=ln1_g[l][None], ln1_b=ln1_b[l][None],
        wq=peer_wq[l].astype(BF16), sk=skp.reshape(2 * PEER_HEADS, PEER_NKEYS, PEER_DKEY).astype(BF16),
        u=peer_u[l].astype(BF16), vt=peer_v[l].T.astype(BF16),
        ln2_g=ln2_g[l][None], ln2_b=ln2_b[l][None],
        lam_init=0.8 - 0.6 * math.exp(-0.3 * l),
    )


def _layer(x, mod, p, rope, cms):
    pa = _proj(x, mod[:, 0:2], p["w_gla"], F32, tn=640, name="proj_gla")
    pb = _proj(x, mod[:, 0:2], p["w_att"], BF16, tn=1024, rope=rope + ((3, 4),), name="proj_att")
    ofwd = _gla_dir(pa, cms[0], p["wgk"][0], p["bgk"][0:1], reverse=False)
    ya = _gla_dir(pa, cms[1], p["wgk"][1], p["bgk"][1:2], reverse=True, ofwd=ofwd, norm_g=p["gla_g"])
    yb = _na(pb, p["na_bias"])
    yc = _diff(pb, p["lam"], p["diff_g"], p["lam_init"])
    x1 = _merge(x, mod[:, 0:3], ya, yb, yc, p["w_gt"], p["w_br"], p["w_o"], p["ln1_g"], p["ln1_b"])
    h2, pj, pb2 = _peer_route(x1, mod[:, 3:5], p["wq"], p["sk"])
    return _peer_dense(h2, p["u"], p["vt"], pj, pb2, x1, mod[:, 5:6], p["ln2_g"], p["ln2_b"])


def kernel(x_prompt, x_sample, c_prompt, c_sample, w_ada, b_ada, w_in, gla_w_gk, gla_b_gk, gla_norm_g, na_rpb, diff_lambda, diff_norm_g, w_br, w_o, ln1_g, ln1_b, peer_wq, peer_subkeys, peer_u, peer_v, ln2_g, ln2_b):
    nb = x_prompt.shape[0]
    c_all = jnp.concatenate([c_prompt, c_sample], axis=0)
    cms = (jnp.asarray(_gla_masks(False)), jnp.asarray(_gla_masks(True)))
    ropes = {x.shape[1]: _rope_tables(x.shape[1]) for x in (x_prompt, x_sample)}
    xs = [x_prompt, x_sample]
    for l in range(DEPTH):
        p = _layer_params(l, w_in, gla_w_gk, gla_b_gk, gla_norm_g, na_rpb, diff_lambda, diff_norm_g, w_br, w_o,
                          ln1_g, ln1_b, peer_wq, peer_subkeys, peer_u, peer_v, ln2_g, ln2_b)
        mod_all = _ada(c_all, w_ada[l].astype(BF16), b_ada[l][None]).reshape(c_all.shape[0], 6, D_MODEL)
        mods = [mod_all[:nb], mod_all[nb:]]
        xs = [_layer(x, m, p, ropes[x.shape[1]], cms) for x, m in zip(xs, mods)]
    return (xs[0], xs[1])
```

```python
import functools
import math

import numpy as np
import jax
import jax.numpy as jnp
from jax import lax
from jax.experimental import pallas as pl
from jax.experimental.pallas import tpu as pltpu

F32 = jnp.float32
BF16 = jnp.bfloat16

D_MODEL = 1024
DEPTH = 2
GRID_W = 64
GLA_HEADS, GLA_DK, GLA_DV, GLA_RANK, GLA_TAU, GLA_CHUNK = 4, 128, 256, 16, 16.0, 64
GLA_SUB = 16
NA_HEADS, NA_DH, NA_KR, NA_KC = 16, 64, 8, 16
NA_ROW_UNROLL = 4
DIFF_HEADS, DIFF_D = 8, 64
DIFF_KEY_CHUNK = 1024
ROPE_THETA = 10000.0
PEER_HEADS, PEER_NKEYS, PEER_DKEY, PEER_TOPK = 8, 128, 128, 16
PEER_N = PEER_NKEYS * PEER_NKEYS
DN_ALPHA = (2 * DEPTH) ** 0.25
LN_EPS = 1e-5
NEG = -1e30

GLA_COLS = 3200
ATT_COLS = 6144
VMEM_LIMIT = 56 * 1024 * 1024


def _cparams(sem):
    return pltpu.CompilerParams(dimension_semantics=sem, vmem_limit_bytes=VMEM_LIMIT)


def _dot(a, b):
    return jnp.dot(a, b, preferred_element_type=F32)


def _dot_nt(a, b):
    return lax.dot_general(a, b, (((1,), (1,)), ((), ())), preferred_element_type=F32)


def _dot_tn(a, b):
    return lax.dot_general(a, b, (((0,), (0,)), ((), ())), preferred_element_type=F32)


def _sigmoid(x):
    return 1.0 / (1.0 + jnp.exp(-x))


def _layer_norm(u, g, b):
    mu = jnp.mean(u, axis=-1, keepdims=True)
    d = u - mu
    var = jnp.mean(d * d, axis=-1, keepdims=True)
    return d * lax.rsqrt(var + LN_EPS) * g + b


def _ada_kernel(c_ref, w_ref, b_ref, o_ref):
    c = c_ref[...]
    s = (c * _sigmoid(c)).astype(BF16)
    o_ref[...] = _dot(s, w_ref[...]) + b_ref[...]


def _ada(c, w, b):
    n = c.shape[0]
    cols = w.shape[1]
    return pl.pallas_call(
        _ada_kernel,
        grid=(cols // D_MODEL,),
        in_specs=[pl.BlockSpec((n, D_MODEL), lambda j: (0, 0)),
                  pl.BlockSpec((D_MODEL, D_MODEL), lambda j: (0, j)),
                  pl.BlockSpec((1, D_MODEL), lambda j: (0, j))],
        out_specs=pl.BlockSpec((n, D_MODEL), lambda j: (0, j)),
        out_shape=jax.ShapeDtypeStruct((n, cols), F32),
        compiler_params=_cparams(("arbitrary",)),
        name="ada_mod",
    )(c, w, b)


def _proj_kernel(x_ref, mod_ref, w_ref, *rest, rope_tiles):
    if rope_tiles:
        cos_ref, sin_ref, o_ref, h_ref = rest
    else:
        o_ref, h_ref = rest
    j = pl.program_id(2)

    @pl.when(j == 0)
    def _():
        x = x_ref[0]
        h_ref[...] = (x * (1.0 + mod_ref[0, 1:2, :]) + mod_ref[0, 0:1, :]).astype(BF16)

    acc = _dot(h_ref[...], w_ref[...])
    if not rope_tiles:
        o_ref[0] = acc.astype(o_ref.dtype)
        return

    is_rope = functools.reduce(jnp.logical_or, [j == t for t in rope_tiles])

    @pl.when(is_rope)
    def _():
        cos = cos_ref[...]
        sin = sin_ref[...]
        lane = lax.broadcasted_iota(jnp.int32, cos.shape, 1)
        first = (lane % DIFF_D) < (DIFF_D // 2)
        for c in range(acc.shape[1] // 128):
            xc = acc[:, c * 128:(c + 1) * 128]
            rot = jnp.where(first, pltpu.roll(xc, 128 - DIFF_D // 2, 1), pltpu.roll(xc, DIFF_D // 2, 1))
            o_ref[0, :, c * 128:(c + 1) * 128] = (xc * cos + rot * sin).astype(o_ref.dtype)

    @pl.when(jnp.logical_not(is_rope))
    def _():
        o_ref[0] = acc.astype(o_ref.dtype)


def _proj(x, mod, w, out_dtype, tn, rope=None, name="proj"):
    bsz, t, _ = x.shape
    cols = w.shape[1]
    tm = min(1024, t)
    in_specs = [pl.BlockSpec((1, tm, D_MODEL), lambda b, i, j: (b, i, 0)),
                pl.BlockSpec((1, 2, D_MODEL), lambda b, i, j: (b, 0, 0)),
                pl.BlockSpec((D_MODEL, tn), lambda b, i, j: (0, j))]
    args = [x, mod, w]
    rope_tiles = ()
    if rope is not None:
        cos, sin, rope_tiles = rope
        in_specs += [pl.BlockSpec((tm, 128), lambda b, i, j: (i, 0)),
                     pl.BlockSpec((tm, 128), lambda b, i, j: (i, 0))]
        args += [cos, sin]
    return pl.pallas_call(
        functools.partial(_proj_kernel, rope_tiles=tuple(rope_tiles)),
        grid=(bsz, t // tm, cols // tn),
        in_specs=in_specs,
        out_specs=pl.BlockSpec((1, tm, tn), lambda b, i, j: (b, i, j)),
        out_shape=jax.ShapeDtypeStruct((bsz, t, cols), out_dtype),
        scratch_shapes=[pltpu.VMEM((tm, D_MODEL), BF16)],
        compiler_params=_cparams(("parallel", "parallel", "arbitrary")),
        name=name,
    )(*args)


def _gla_masks(reverse):
    c, s = GLA_CHUNK, GLA_SUB
    i = np.arange(c)[:, None]
    j = np.arange(c)[None, :]
    if not reverse:
        cum = (j <= i)
        ref = (j < (i // s) * s)
    else:
        cum = (j >= i)
        ref = (j >= (i // s + 1) * s)
    return np.concatenate([cum, ref], axis=0).astype(np.float32)


def _gla_kernel(q_ref, k_ref, v_ref, glr_ref, cm_ref, wgk_ref, bgk_ref, *rest, reverse, n_chunks):
    if reverse:
        ofwd_ref, r_ref, ng_ref, o_ref, st_ref, acc_ref = rest
    else:
        o_ref, st_ref = rest
        acc_ref = o_ref.at[0]
    c, s = GLA_CHUNK, GLA_SUB
    nsub = c // s
    dk = GLA_DK

    @pl.when(pl.program_id(2) == 0)
    def _():
        st_ref[...] = jnp.zeros_like(st_ref)

    def lanes(x):
        return jnp.concatenate([x[ci * c:(ci + 1) * c] for ci in range(n_chunks)], axis=1)

    z = _dot(glr_ref[0].astype(BF16), wgk_ref[...]) + bgk_ref[...]
    g = lanes((jnp.minimum(z, 0.0) - jnp.log(1.0 + jnp.exp(-jnp.abs(z)))) * (1.0 / GLA_TAU))
    g_hi = g.astype(BF16)
    g_lo = (g - g_hi.astype(F32)).astype(BF16)
    cm = cm_ref[...].astype(BF16)
    cums = _dot(cm, g_hi) + _dot(cm, g_lo)
    bc, bref = cums[:c], cums[c:]
    q = lanes(q_ref[0]) * (GLA_DK ** -0.5)
    k = lanes(k_ref[0])
    qe = q * jnp.exp(bc - bref)
    qi = (qe * jnp.exp(bref)).astype(BF16)
    qe = qe.astype(BF16)
    bl = bc[0:1] if reverse else bc[c - 1:c]
    dec = jnp.exp(bl)
    kl = (k * jnp.exp(bl - bc)).astype(BF16)
    row = lax.broadcasted_iota(jnp.int32, (c, 1), 0)
    ksub = []
    for si in range(nsub):
        ok = (row >= si * s) if reverse else (row < (si + 1) * s)
        ksub.append(jnp.where(ok, k * jnp.exp(bref[si * s:si * s + 1] - bc), 0.0).astype(BF16))
    rr = lax.broadcasted_iota(jnp.int32, (c, nsub * c), 0)
    cc = lax.broadcasted_iota(jnp.int32, (c, nsub * c), 1)
    causal = (cc % c > rr) if reverse else (cc % c <= rr)
    keep = jnp.logical_and(cc // c == rr // s, causal)
    vs = [v_ref[0, ci * c:(ci + 1) * c, :].astype(BF16) for ci in range(n_chunks)]
    atts = []
    for ci in range(n_chunks):
        ls = slice(ci * dk, (ci + 1) * dk)
        kcat = jnp.concatenate([ks[:, ls] for ks in ksub], axis=0)
        atts.append(_dot_nt(qe[:, ls], kcat))
    atts = [jnp.where(keep, a, 0.0).astype(BF16) for a in atts]
    for ci in range(n_chunks):
        acc_ref[ci * c:(ci + 1) * c, :] = _dot(atts[ci], jnp.concatenate([vs[ci]] * nsub, axis=0))
    uts = [_dot_tn(vs[ci], kl[:, ci * dk:(ci + 1) * dk]) for ci in range(n_chunks)]

    st = st_ref[...]
    order = range(n_chunks - 1, -1, -1) if reverse else range(n_chunks)
    for ci in order:
        ls = slice(ci * dk, (ci + 1) * dk)
        inter = _dot_nt(qi[:, ls], st.astype(BF16))
        acc_ref[ci * c:(ci + 1) * c, :] = acc_ref[ci * c:(ci + 1) * c, :] + inter
        st = st * dec[:, ls] + uts[ci]
    st_ref[...] = st

    if reverse:
        y = ofwd_ref[0] + acc_ref[...]
        y = y * lax.rsqrt(jnp.mean(y * y, axis=-1, keepdims=True) + LN_EPS) * ng_ref[...]
        r = r_ref[0]
        o_ref[0] = (y * (r * _sigmoid(r))).astype(o_ref.dtype)


def _gla_dir(pa, cm, wgk, bgk, reverse, ofwd=None, norm_g=None):
    bsz, t, _ = pa.shape
    tt = min(512, t)
    nt = t // tt

    def ti(i):
        return (nt - 1 - i) if reverse else i

    in_specs = [pl.BlockSpec((1, tt, GLA_DK), lambda b, h, i: (b, ti(i), h)),
                pl.BlockSpec((1, tt, GLA_DK), lambda b, h, i: (b, ti(i), GLA_HEADS + h)),
                pl.BlockSpec((1, tt, GLA_DV), lambda b, h, i: (b, ti(i), GLA_HEADS + h)),
                pl.BlockSpec((1, tt, 128), lambda b, h, i: (b, ti(i), 3072 // 128)),
                pl.BlockSpec((2 * GLA_CHUNK, GLA_CHUNK), lambda b, h, i: (0, 0)),
                pl.BlockSpec((128, GLA_DK), lambda b, h, i: (0, h)),
                pl.BlockSpec((1, GLA_DK), lambda b, h, i: (0, h))]
    args = [pa, pa, pa, pa, cm, wgk, bgk]
    scratch = [pltpu.VMEM((GLA_DV, GLA_DK), F32)]
    if reverse:
        in_specs += [pl.BlockSpec((1, tt, GLA_DV), lambda b, h, i: (b, ti(i), h)),
                     pl.BlockSpec((1, tt, GLA_DV), lambda b, h, i: (b, ti(i), 2 * GLA_HEADS + h)),
                     pl.BlockSpec((1, GLA_DV), lambda b, h, i: (0, 0))]
        args += [ofwd, pa, norm_g]
        scratch += [pltpu.VMEM((tt, GLA_DV), F32)]
        out_dtype = BF16
    else:
        out_dtype = F32
    return pl.pallas_call(
        functools.partial(_gla_kernel, reverse=reverse, n_chunks=tt // GLA_CHUNK),
        grid=(bsz, GLA_HEADS, nt),
        in_specs=in_specs,
        out_specs=pl.BlockSpec((1, tt, GLA_DV), lambda b, h, i: (b, ti(i), h)),
        out_shape=jax.ShapeDtypeStruct((bsz, t, GLA_HEADS * GLA_DV), out_dtype),
        scratch_shapes=scratch,
        compiler_params=_cparams(("parallel", "parallel", "arbitrary")),
        name="gla_bwd" if reverse else "gla_fwd",
    )(*args)


def _na_bias_table(rpb):
    qc = np.arange(GRID_W)[:, None]
    kc = np.arange(GRID_W)[None, :]
    cs = np.clip(qc - NA_KC // 2, 0, GRID_W - NA_KC)
    allowed = (kc >= cs) & (kc < cs + NA_KC)
    dc = np.clip(kc - qc + NA_KC - 1, 0, 2 * NA_KC - 2)
    onehot = (dc[None] == np.arange(2 * NA_KC - 1)[:, None, None]).astype(np.float32)
    tab = jnp.einsum("hrd,dqk->hrqk", rpb.astype(F32), jnp.asarray(onehot), precision=lax.Precision.HIGHEST)
    tab = jnp.where(jnp.asarray(allowed)[None, None], tab, NEG)
    tab = jnp.stack([tab[:, NA_KR - 1 - d:2 * NA_KR - 1 - d] for d in range(NA_KR)], axis=1)
    tab = tab.transpose(0, 1, 3, 2, 4).reshape(NA_HEADS // 2, 2, NA_KR, GRID_W, NA_KR * GRID_W)
    return tab.transpose(0, 2, 1, 3, 4).reshape(NA_HEADS // 2, NA_KR, 2 * GRID_W, NA_KR * GRID_W)


def _na_kernel(q_ref, k_ref, v_ref, bias_ref, o_ref, *, rows):
    lane = lax.broadcasted_iota(jnp.int32, (GRID_W, 128), 1)
    low = lane < NA_DH
    win = NA_KR * GRID_W
    nr = NA_ROW_UNROLL

    def body(it, carry):
        q0s, scs, vws = [], [], []
        for u in range(nr):
            r = it * nr + u
            rs = jnp.clip(r - NA_KR // 2, 0, rows - NA_KR)
            q0 = pl.multiple_of(r * GRID_W, GRID_W)
            k0 = pl.multiple_of(rs * GRID_W, GRID_W)
            q = q_ref[0, pl.ds(q0, GRID_W), :]
            zq = jnp.zeros_like(q)
            q2 = jnp.concatenate([jnp.where(low, q, zq), jnp.where(low, zq, q)], axis=0)
            scs.append(_dot_nt(q2, k_ref[0, pl.ds(k0, win), :]) + bias_ref[0, r - rs])
            vws.append(v_ref[0, pl.ds(k0, win), :])
            q0s.append(q0)
        ms = [jnp.max(sc, axis=-1, keepdims=True) for sc in scs]
        es = [jnp.exp(sc - m) for sc, m in zip(scs, ms)]
        zs = [jnp.sum(e, axis=-1, keepdims=True) for e in es]
        os_ = [_dot(e.astype(BF16), vw) / z for e, vw, z in zip(es, vws, zs)]
        for q0, o in zip(q0s, os_):
            o_ref[0, pl.ds(q0, GRID_W), :] = jnp.where(low, o[:GRID_W], o[GRID_W:]).astype(o_ref.dtype)
        return carry

    lax.fori_loop(0, rows // nr, body, 0)


def _na(pb, bias):
    bsz, t, _ = pb.shape
    rows = t // GRID_W
    assert rows >= NA_KR and rows % NA_ROW_UNROLL == 0
    npair = NA_HEADS // 2
    return pl.pallas_call(
        functools.partial(_na_kernel, rows=rows),
        grid=(bsz, npair),
        in_specs=[pl.BlockSpec((1, t, 128), lambda b, j: (b, 0, j)),
                  pl.BlockSpec((1, t, 128), lambda b, j: (b, 0, npair + j)),
                  pl.BlockSpec((1, t, 128), lambda b, j: (b, 0, 2 * npair + j)),
                  pl.BlockSpec((1, NA_KR, 2 * GRID_W, NA_KR * GRID_W), lambda b, j: (j, 0, 0, 0))],
        out_specs=pl.BlockSpec((1, t, 128), lambda b, j: (b, 0, j)),
        out_shape=jax.ShapeDtypeStruct((bsz, t, D_MODEL), BF16),
        compiler_params=_cparams(("parallel", "parallel")),
        name="na_attn",
    )(pb, pb, pb, bias)


def _diff_kernel(q_ref, k_ref, v_ref, lam_ref, ng_ref, o_ref, vt_ref, *, lam_init):
    dv = 2 * DIFF_D

    @pl.when(pl.program_id(2) == 0)
    def _():
        vt_ref[0:dv, :] = v_ref[0].astype(F32).T.astype(BF16)
        vt_ref[dv:, :] = jnp.ones((8, vt_ref.shape[1]), BF16)

    lv = lam_ref[...]
    l1 = jnp.sum(lv[0:1] * lv[1:2], axis=-1, keepdims=True)
    l2 = jnp.sum(lv[2:3] * lv[3:4], axis=-1, keepdims=True)
    lam = jnp.exp(l1) - jnp.exp(l2) + lam_init
    q = q_ref[0]
    k = k_ref[0]
    low = lax.broadcasted_iota(jnp.int32, q.shape, 1) < DIFF_D
    zq = jnp.zeros_like(q)
    t = k.shape[0]
    kc = min(DIFF_KEY_CHUNK, t)
    nc = t // kc
    qms = (jnp.where(low, q, zq), jnp.where(low, zq, q))
    st = [dict(m=None, acc=None) for _ in qms]
    sc = [[None] * nc for _ in qms]
    for p_, qm in enumerate(qms):
        sc[p_][0] = _dot_nt(k[:kc], qm)
    for c in range(nc):
        for p_, qm in enumerate(qms):
            if c + 1 < nc:
                sc[p_][c + 1] = _dot_nt(k[(c + 1) * kc:(c + 2) * kc], qm)
            s_ = sc[p_][c]
            d = st[p_]
            cm = jnp.max(s_, axis=0, keepdims=True)
            vc = vt_ref[:, c * kc:(c + 1) * kc]
            if c == 0:
                d["m"] = cm
                d["acc"] = _dot(vc, jnp.exp2(s_ - cm).astype(BF16))
            else:
                m_new = jnp.maximum(d["m"], cm)
                d["acc"] = jnp.exp2(d["m"] - m_new) * d["acc"] + _dot(vc, jnp.exp2(s_ - m_new).astype(BF16))
                d["m"] = m_new
    outs = [d["acc"][:dv] / d["acc"][dv:dv + 1] for d in st]
    o = outs[0] - lam * outs[1]
    y = o * lax.rsqrt(jnp.mean(o * o, axis=0, keepdims=True) + LN_EPS)
    o_ref[0] = (y.T * (ng_ref[...] * (1.0 - lam_init))).astype(o_ref.dtype)


def _diff(pb, lam_vec, norm_g, lam_init):
    bsz, t, _ = pb.shape
    tq = min(512, t)
    base = (NA_HEADS * NA_DH * 3) // 128
    return pl.pallas_call(
        functools.partial(_diff_kernel, lam_init=lam_init),
        grid=(bsz, DIFF_HEADS, t // tq),
        in_specs=[pl.BlockSpec((1, tq, 128), lambda b, h, i: (b, i, base + h)),
                  pl.BlockSpec((1, t, 128), lambda b, h, i: (b, 0, base + DIFF_HEADS + h)),
                  pl.BlockSpec((1, t, 128), lambda b, h, i: (b, 0, base + 2 * DIFF_HEADS + h)),
                  pl.BlockSpec((4, DIFF_D), lambda b, h, i: (0, 0)),
                  pl.BlockSpec((1, 2 * DIFF_D), lambda b, h, i: (0, 0))],
        out_specs=pl.BlockSpec((1, tq, 128), lambda b, h, i: (b, i, h)),
        out_shape=jax.ShapeDtypeStruct((bsz, t, D_MODEL), BF16),
        scratch_shapes=[pltpu.VMEM((2 * DIFF_D + 8, t), BF16)],
        compiler_params=_cparams(("parallel", "parallel", "arbitrary")),
        name="diff_attn",
    )(pb, pb, pb, lam_vec, norm_g)


def _merge_kernel(x_ref, mod_ref, ya_ref, yb_ref, yc_ref, wgt_ref, wbr_ref, wo_ref, g_ref, b_ref, o_ref):
    x = x_ref[0]
    h = (x * (1.0 + mod_ref[0, 1:2, :]) + mod_ref[0, 0:1, :]).astype(BF16)
    merged = None
    for n, y_ref in enumerate((ya_ref, yb_ref, yc_ref)):
        gt = _dot(h, wgt_ref[:, n * D_MODEL:(n + 1) * D_MODEL])
        br = _dot(y_ref[0], wbr_ref[n])
        term = _sigmoid(gt) * br
        merged = term if merged is None else merged + term
    out = _dot(merged.astype(BF16), wo_ref[...])
    u = DN_ALPHA * x + mod_ref[0, 2:3, :] * out
    o_ref[0] = _layer_norm(u, g_ref[...], b_ref[...])


def _const_spec(shape, nidx):
    zeros = (0,) * len(shape)
    if nidx == 2:
        return pl.BlockSpec(shape, lambda b, i: zeros, pipeline_mode=pl.Buffered(1))
    return pl.BlockSpec(shape, lambda b, i, e: zeros, pipeline_mode=pl.Buffered(1))


def _merge(x, mod, ya, yb, yc, wgt, wbr, wo, g, b):
    bsz, t, _ = x.shape
    tm = min(512, t)
    tok = pl.BlockSpec((1, tm, D_MODEL), lambda bb, i: (bb, i, 0))
    return pl.pallas_call(
        _merge_kernel,
        grid=(bsz, t // tm),
        in_specs=[tok, pl.BlockSpec((1, 3, D_MODEL), lambda bb, i: (bb, 0, 0)), tok, tok, tok,
                  _const_spec((D_MODEL, 3 * D_MODEL), 2), _const_spec((3, D_MODEL, D_MODEL), 2),
                  _const_spec((D_MODEL, D_MODEL), 2), _const_spec((1, D_MODEL), 2), _const_spec((1, D_MODEL), 2)],
        out_specs=tok,
        out_shape=jax.ShapeDtypeStruct((bsz, t, D_MODEL), F32),
        compiler_params=_cparams(("parallel", "parallel")),
        name="merge_ln",
    )(x, mod, ya, yb, yc, wgt, wbr, wo, g, b)


def _sort_network(n):
    pairs = []
    p = 1
    while p < n:
        k = p
        while k >= 1:
            for jj in range(k % p, n - k, 2 * k):
                for ii in range(min(k, n - jj - k)):
                    if (ii + jj) // (2 * p) == (ii + jj + k) // (2 * p):
                        pairs.append((ii + jj, ii + jj + k))
            k //= 2
        p *= 2
    return pairs


_SORT16 = _sort_network(16)


def _top_rows(x, n):
    rows_in, lanes = x.shape
    groups = [x[8 * g:8 * g + 8] for g in range(rows_in // 8)]
    groups += [jnp.full((8, lanes), NEG, F32)] * (16 - len(groups))
    for i, j in _SORT16:
        hi = jnp.maximum(groups[i], groups[j])
        groups[j] = jnp.minimum(groups[i], groups[j])
        groups[i] = hi
    rows = []
    for r in range(n):
        m = jnp.max(groups[0], axis=0, keepdims=True)
        rows.append(m)
        hit = groups[0] >= m
        for v in range(min(16, n - r - 1)):
            nxt = groups[v + 1] if v + 1 < 16 else NEG
            groups[v] = jnp.where(hit, nxt, groups[v])
    return rows


def _peer_route_kernel(x_ref, mod_ref, wq_ref, sk_ref, h_ref, pj_ref, pb_ref):
    x = x_ref[0]
    h = (x * (1.0 + mod_ref[0, 1:2, :]) + mod_ref[0, 0:1, :]).astype(BF16)
    h_ref[0] = h
    q = _dot(h, wq_ref[...]).astype(BF16)
    kk = PEER_TOPK
    for hd in range(PEER_HEADS):
        qh = q[:, hd * PEER_DKEY:(hd + 1) * PEER_DKEY]
        s1 = _dot_nt(sk_ref[2 * hd], qh)
        s2 = _dot_nt(sk_ref[2 * hd + 1], qh)
        a = _top_rows(s1, kk + 1)
        b = _top_rows(s2, kk + 1)
        b_lo = jnp.concatenate(b[:8], axis=0)
        cand = [a[i] + b_lo for i in range(8)]
        cand.append(a[0] + jnp.concatenate(b[8:16], axis=0))
        cand.append(jnp.concatenate(a[8:16], axis=0) + b[0])
        cand.append(jnp.concatenate([a[0] + b[16], a[16] + b[0]] + [jnp.full_like(a[0], NEG)] * 6, axis=0))
        cand = jnp.concatenate(cand, axis=0)
        top = _top_rows(cand, kk + 1)
        thr = 0.5 * (top[kk - 1] + top[kk])
        mx = a[0] + b[0]
        zs = jnp.sum(jnp.where(cand >= thr, jnp.exp(cand - mx), 0.0), axis=0, keepdims=True)
        t1 = thr - s1
        cnt = jnp.zeros_like(s1)
        rank2 = jnp.zeros_like(s2)
        for c in range(kk):
            cnt = jnp.where(b[c] >= t1, float(c + 1), cnt)
        for c in range(kk + 1):
            rank2 = jnp.where(s2 < b[c], float(c + 1), rank2)
        e1 = jnp.exp(s1 - a[0]) / zs
        for c in range(cnt.shape[1] // 128):
            cs = slice(c * 128, (c + 1) * 128)
            pj_ref[0, hd, 0, :, c] = cnt[:, cs].reshape(PEER_NKEYS // 8, 8, 128)
            pj_ref[0, hd, 1, :, c] = e1[:, cs].reshape(PEER_NKEYS // 8, 8, 128)
        pb_ref[0, hd, 0] = rank2.astype(BF16)
        pb_ref[0, hd, 1] = jnp.exp(s2 - b[0]).astype(BF16)


def _peer_route(x1, mod, wq, sk):
    bsz, t, _ = x1.shape
    tp = min(256, t)
    pspec = pl.BlockSpec((1, PEER_HEADS, 2, PEER_NKEYS, tp), lambda b, i: (b, 0, 0, 0, i))
    jspec = pl.BlockSpec((1, PEER_HEADS, 2, PEER_NKEYS // 8, tp // 128, 8, 128), lambda b, i: (b, 0, 0, 0, i, 0, 0))
    return pl.pallas_call(
        _peer_route_kernel,
        grid=(bsz, t // tp),
        in_specs=[pl.BlockSpec((1, tp, D_MODEL), lambda b, i: (b, i, 0)),
                  pl.BlockSpec((1, 2, D_MODEL), lambda b, i: (b, 0, 0)),
                  _const_spec((D_MODEL, PEER_HEADS * PEER_DKEY), 2),
                  _const_spec((2 * PEER_HEADS, PEER_NKEYS, PEER_DKEY), 2)],
        out_specs=[pl.BlockSpec((1, tp, D_MODEL), lambda b, i: (b, i, 0)), jspec, pspec],
        out_shape=[jax.ShapeDtypeStruct((bsz, t, D_MODEL), BF16),
                   jax.ShapeDtypeStruct((bsz, PEER_HEADS, 2, PEER_NKEYS // 8, t // 128, 8, 128), F32),
                   jax.ShapeDtypeStruct((bsz, PEER_HEADS, 2, PEER_NKEYS, t), BF16)],
        compiler_params=_cparams(("parallel", "parallel")),
        name="peer_route",
    )(x1, mod, wq, sk)


PEER_LANE_CHUNK = 256
PEER_EB = 512


def _bcast_rows(pj_ref, hd, which, r0, cc, lc):
    parts = [pj_ref[0, hd, which, 0, cc * (lc // 128) + c, pl.ds(r0, 16, stride=0), :] for c in range(lc // 128)]
    return jnp.concatenate(parts, axis=1).astype(BF16)


def _peer_dense_kernel(h_ref, u_ref, vt_ref, pj_ref, pb_ref, x_ref, mod_ref, g_ref, b_ref, o_ref,
                       acc_ref, act0_ref, act1_ref, w0_ref, w1_ref, *, eb, nblk):
    e = pl.program_id(2)
    nk = PEER_NKEYS
    tm = acc_ref.shape[1]
    lc = PEER_LANE_CHUNK
    assert tm == 2 * lc and eb == 4 * nk
    zero = jnp.zeros((), BF16)

    @pl.when(e == 0)
    def _():
        acc_ref[...] = jnp.zeros_like(acc_ref)
        act1_ref[...] = jnp.zeros_like(act1_ref)
        w0_ref[...] = jnp.zeros_like(w0_ref)
        w1_ref[...] = jnp.zeros_like(w1_ref)

    def step(par, act_in, act_out, w_in, w_out):
        nj = eb // nk
        rc = nk // 2
        for cc in range(tm // lc):
            ls = slice(cc * lc, (cc + 1) * lc)
            for ri in range(2):
                rr = slice(ri * rc, (ri + 1) * rc)
                for jp in range(nj // 2):
                    if ri == 0:
                        us = slice(jp * (eb // 2), (jp + 1) * (eb // 2))
                        act_out[us, ls] = _dot_nt(u_ref[us, :], h_ref[0, ls, :])
                    else:
                        ds_ = slice(jp * (D_MODEL // 2), (jp + 1) * (D_MODEL // 2))
                        acc_ref[ds_, ls] += _dot(vt_ref[0, ds_, :], w_in[:, ls])
                    w = [None, None]
                    for hd in range(PEER_HEADS):
                        rank = pb_ref[0, hd, 0, rr, ls]
                        e2 = pb_ref[0, hd, 1, rr, ls]
                        for k in range(2):
                            r0 = (1 - par) * nj + jp * 2 + k
                            cnt = _bcast_rows(pj_ref, hd, 0, r0, cc, lc)
                            e1 = _bcast_rows(pj_ref, hd, 1, r0, cc, lc)
                            cnt = jnp.concatenate([cnt] * (rc // 16), axis=0)
                            e1 = jnp.concatenate([e1] * (rc // 16), axis=0)
                            term = jnp.where(rank < cnt, e2, zero) * e1
                            w[k] = term if w[k] is None else w[k] + term
                    for k in range(2):
                        jj = jp * 2 + k
                        rs = slice(jj * nk + ri * rc, jj * nk + (ri + 1) * rc)
                        a = act_in[rs, ls]
                        gelu = 0.5 * a * (1.0 + lax.erf(a * (2.0 ** -0.5)))
                        w_out[rs, ls] = w[k] * gelu.astype(BF16)

    @pl.when(e % 2 == 0)
    def _():
        step(0, act1_ref, act0_ref, w0_ref, w1_ref)

    @pl.when(e % 2 == 1)
    def _():
        step(1, act0_ref, act1_ref, w1_ref, w0_ref)

    @pl.when(e == pl.num_programs(2) - 1)
    def _():
        f = acc_ref[...].T
        x = x_ref[0]
        u = DN_ALPHA * x + mod_ref[0, 0:1, :] * f
        o_ref[0] = _layer_norm(u, g_ref[...], b_ref[...])


def _peer_dense(h2, u_tab, vt_tab, pj, pb, x1, mod, g, b):
    bsz, t, _ = x1.shape
    tm = min(512, t)
    eb = PEER_EB
    tok = pl.BlockSpec((1, tm, D_MODEL), lambda bb, i, e: (bb, i, 0))
    pspec = pl.BlockSpec((1, PEER_HEADS, 2, PEER_NKEYS, tm), lambda bb, i, e: (bb, 0, 0, 0, i))
    nblk = PEER_N // eb
    assert 2 * (eb // PEER_NKEYS) == 8
    jspec = pl.BlockSpec((1, PEER_HEADS, 2, 1, tm // 128, 8, 128),
                         lambda bb, i, e: (bb, 0, 0, jnp.clip(e - 1, 0, nblk - 1) // 2, i, 0, 0))
    return pl.pallas_call(
        functools.partial(_peer_dense_kernel, eb=eb, nblk=nblk),
        grid=(bsz, t // tm, nblk + 2),
        in_specs=[tok,
                  pl.BlockSpec((eb, D_MODEL), lambda bb, i, e: (jnp.minimum(e, nblk - 1), 0)),
                  pl.BlockSpec((1, D_MODEL, eb), lambda bb, i, e: (jnp.clip(e - 2, 0, nblk - 1), 0, 0)),
                  jspec, pspec,
                  tok,
                  pl.BlockSpec((1, 1, D_MODEL), lambda bb, i, e: (bb, 0, 0)),
                  _const_spec((1, D_MODEL), 3), _const_spec((1, D_MODEL), 3)],
        out_specs=tok,
        out_shape=jax.ShapeDtypeStruct((bsz, t, D_MODEL), F32),
        scratch_shapes=[pltpu.VMEM((D_MODEL, tm), F32),
                        pltpu.VMEM((eb, tm), F32), pltpu.VMEM((eb, tm), F32),
                        pltpu.VMEM((eb, tm), BF16), pltpu.VMEM((eb, tm), BF16)],
        compiler_params=_cparams(("parallel", "parallel", "arbitrary")),
        name="peer_dense",
    )(h2, u_tab, vt_tab, pj, pb, x1, mod, g, b)


def _rope_tables(t):
    half = DIFF_D // 2
    inv = jnp.exp(-math.log(ROPE_THETA) * jnp.arange(half, dtype=F32) / half)
    ang = jnp.arange(t, dtype=F32)[:, None] * inv[None, :]
    cos, sin = jnp.cos(ang), jnp.sin(ang)
    cos128 = jnp.tile(cos, (1, 4))
    sin128 = jnp.tile(jnp.concatenate([-sin, sin], axis=1), (1, 2))
    return cos128, sin128


def _layer_params(l, w_in, gla_w_gk, gla_b_gk, gla_norm_g, na_rpb, diff_lambda, diff_norm_g, w_br, w_o,
                  ln1_g, ln1_b, peer_wq, peer_subkeys, peer_u, peer_v, ln2_g, ln2_b):
    w = w_in[l]
    w_gla = jnp.pad(w[:, :3104], ((0, 0), (0, GLA_COLS - 3104))).astype(BF16)
    w_att = w[:, 3104:3104 + ATT_COLS]
    qscale = np.ones((ATT_COLS,), np.float32)
    qscale[0:1024] = NA_DH ** -0.5
    qscale[3072:4096] = DIFF_D ** -0.5 * math.log2(math.e)
    w_att = (w_att * qscale).astype(BF16)
    w_gt = w[:, 3104 + ATT_COLS:].astype(BF16)
    wgk = jnp.zeros((2, 128, GLA_HEADS * GLA_DK), F32)
    wgk = wgk.at[0, :GLA_RANK].set(gla_w_gk[l, 0]).at[1, GLA_RANK:2 * GLA_RANK].set(gla_w_gk[l, 1]).astype(BF16)
    sk = peer_subkeys[l]
    half = PEER_DKEY // 2
    skp = jnp.zeros((PEER_HEADS, 2, PEER_NKEYS, PEER_DKEY), F32)
    skp = skp.at[:, 0, :, :half].set(sk[:, 0]).at[:, 1, :, half:].set(sk[:, 1])
    return dict(
        w_gla=w_gla, w_att=w_att, w_gt=w_gt, wgk=wgk, bgk=gla_b_gk[l],
        gla_g=gla_norm_g[l][None], na_bias=_na_bias_table(na_rpb[l]),
        lam=diff_lambda[l], diff_g=diff_norm_g[l][None],
        w_br=w_br[l].astype(BF16), w_o=w_o[l].astype(BF16),
        ln1_g=ln1_g[l][None], ln1_b=ln1_b[l][None],
        wq=peer_wq[l].astype(BF16), sk=skp.reshape(2 * PEER_HEADS, PEER_NKEYS, PEER_DKEY).astype(BF16),
        u=peer_u[l].astype(BF16),
        vt=peer_v[l].astype(BF16).reshape(PEER_N // PEER_EB, PEER_EB, D_MODEL).transpose(0, 2, 1),
        ln2_g=ln2_g[l][None], ln2_b=ln2_b[l][None],
        lam_init=0.8 - 0.6 * math.exp(-0.3 * l),
    )


def _layer(x, mod, p, rope, cms):
    pa = _proj(x, mod[:, 0:2], p["w_gla"], F32, tn=640, name="proj_gla")
    pb = _proj(x, mod[:, 0:2], p["w_att"], BF16, tn=1024, rope=rope + ((3, 4),), name="proj_att")
    ofwd = _gla_dir(pa, cms[0], p["wgk"][0], p["bgk"][0:1], reverse=False)
    ya = _gla_dir(pa, cms[1], p["wgk"][1], p["bgk"][1:2], reverse=True, ofwd=ofwd, norm_g=p["gla_g"])
    yb = _na(pb, p["na_bias"])
    yc = _diff(pb, p["lam"], p["diff_g"], p["lam_init"])
    x1 = _merge(x, mod[:, 0:3], ya, yb, yc, p["w_gt"], p["w_br"], p["w_o"], p["ln1_g"], p["ln1_b"])
    h2, pj, pb2 = _peer_route(x1, mod[:, 3:5], p["wq"], p["sk"])
    return _peer_dense(h2, p["u"], p["vt"], pj, pb2, x1, mod[:, 5:6], p["ln2_g"], p["ln2_b"])


def kernel(x_prompt, x_sample, c_prompt, c_sample, w_ada, b_ada, w_in, gla_w_gk, gla_b_gk, gla_norm_g, na_rpb, diff_lambda, diff_norm_g, w_br, w_o, ln1_g, ln1_b, peer_wq, peer_subkeys, peer_u, peer_v, ln2_g, ln2_b):
    nb = x_prompt.shape[0]
    c_all = jnp.concatenate([c_prompt, c_sample], axis=0)
    cms = (jnp.asarray(_gla_masks(False)), jnp.asarray(_gla_masks(True)))
    ropes = {x.shape[1]: _rope_tables(x.shape[1]) for x in (x_prompt, x_sample)}
    xs = [x_prompt, x_sample]
    for l in range(DEPTH):
        p = _layer_params(l, w_in, gla_w_gk, gla_b_gk, gla_norm_g, na_rpb, diff_lambda, diff_norm_g, w_br, w_o,
                          ln1_g, ln1_b, peer_wq, peer_subkeys, peer_u, peer_v, ln2_g, ln2_b)
        mod_all = _ada(c_all, w_ada[l].astype(BF16), b_ada[l][None]).reshape(c_all.shape[0], 6, D_MODEL)
        mods = [mod_all[:nb], mod_all[nb:]]
        xs = [_layer(x, m, p, ropes[x.shape[1]], cms) for x, m in zip(xs, mods)]
    return (xs[0], xs[1])
```

```python
import functools
import math

import numpy as np
import jax
import jax.numpy as jnp
from jax import lax
from jax.experimental import pallas as pl
from jax.experimental.pallas import tpu as pltpu

F32 = jnp.float32
BF16 = jnp.bfloat16

D_MODEL = 1024
DEPTH = 2
GRID_W = 64
GLA_HEADS, GLA_DK, GLA_DV, GLA_RANK, GLA_TAU, GLA_CHUNK = 4, 128, 256, 16, 16.0, 64
GLA_SUB = 16
NA_HEADS, NA_DH, NA_KR, NA_KC = 16, 64, 8, 16
NA_ROW_UNROLL = 4
DIFF_HEADS, DIFF_D = 8, 64
DIFF_KEY_CHUNK = 512
ROPE_THETA = 10000.0
PEER_HEADS, PEER_NKEYS, PEER_DKEY, PEER_TOPK = 8, 128, 128, 16
PEER_N = PEER_NKEYS * PEER_NKEYS
DN_ALPHA = (2 * DEPTH) ** 0.25
LN_EPS = 1e-5
NEG = -1e30

GLA_COLS = 3200
ATT_COLS = 6144
VMEM_LIMIT = 56 * 1024 * 1024


def _cparams(sem):
    return pltpu.CompilerParams(dimension_semantics=sem, vmem_limit_bytes=VMEM_LIMIT)


def _dot(a, b):
    return jnp.dot(a, b, preferred_element_type=F32)


def _dot_nt(a, b):
    return lax.dot_general(a, b, (((1,), (1,)), ((), ())), preferred_element_type=F32)


def _dot_tn(a, b):
    return lax.dot_general(a, b, (((0,), (0,)), ((), ())), preferred_element_type=F32)


def _sigmoid(x):
    return 1.0 / (1.0 + jnp.exp(-x))


def _layer_norm(u, g, b):
    mu = jnp.mean(u, axis=-1, keepdims=True)
    d = u - mu
    var = jnp.mean(d * d, axis=-1, keepdims=True)
    return d * lax.rsqrt(var + LN_EPS) * g + b


def _ada_kernel(c_ref, w_ref, b_ref, o_ref):
    c = c_ref[...]
    s = (c * _sigmoid(c)).astype(BF16)
    o_ref[...] = _dot(s, w_ref[...]) + b_ref[...]


def _ada(c, w, b):
    n = c.shape[0]
    cols = w.shape[1]
    return pl.pallas_call(
        _ada_kernel,
        grid=(cols // D_MODEL,),
        in_specs=[pl.BlockSpec((n, D_MODEL), lambda j: (0, 0)),
                  pl.BlockSpec((D_MODEL, D_MODEL), lambda j: (0, j)),
                  pl.BlockSpec((1, D_MODEL), lambda j: (0, j))],
        out_specs=pl.BlockSpec((n, D_MODEL), lambda j: (0, j)),
        out_shape=jax.ShapeDtypeStruct((n, cols), F32),
        compiler_params=_cparams(("arbitrary",)),
        name="ada_mod",
    )(c, w, b)


def _proj_kernel(x_ref, mod_ref, w_ref, *rest, rope_tiles):
    if rope_tiles:
        cos_ref, sin_ref, o_ref, h_ref = rest
    else:
        o_ref, h_ref = rest
    j = pl.program_id(2)

    @pl.when(j == 0)
    def _():
        x = x_ref[0]
        h_ref[...] = (x * (1.0 + mod_ref[0, 1:2, :]) + mod_ref[0, 0:1, :]).astype(BF16)

    acc = _dot(h_ref[...], w_ref[...])
    if not rope_tiles:
        o_ref[0] = acc.astype(o_ref.dtype)
        return

    is_rope = functools.reduce(jnp.logical_or, [j == t for t in rope_tiles])

    @pl.when(is_rope)
    def _():
        cos = cos_ref[...]
        sin = sin_ref[...]
        lane = lax.broadcasted_iota(jnp.int32, cos.shape, 1)
        first = (lane % DIFF_D) < (DIFF_D // 2)
        for c in range(acc.shape[1] // 128):
            xc = acc[:, c * 128:(c + 1) * 128]
            rot = jnp.where(first, pltpu.roll(xc, 128 - DIFF_D // 2, 1), pltpu.roll(xc, DIFF_D // 2, 1))
            o_ref[0, :, c * 128:(c + 1) * 128] = (xc * cos + rot * sin).astype(o_ref.dtype)

    @pl.when(jnp.logical_not(is_rope))
    def _():
        o_ref[0] = acc.astype(o_ref.dtype)


def _proj(x, mod, w, out_dtype, tn, rope=None, name="proj"):
    bsz, t, _ = x.shape
    cols = w.shape[1]
    tm = min(1024, t)
    in_specs = [pl.BlockSpec((1, tm, D_MODEL), lambda b, i, j: (b, i, 0)),
                pl.BlockSpec((1, 2, D_MODEL), lambda b, i, j: (b, 0, 0)),
                pl.BlockSpec((D_MODEL, tn), lambda b, i, j: (0, j))]
    args = [x, mod, w]
    rope_tiles = ()
    if rope is not None:
        cos, sin, rope_tiles = rope
        in_specs += [pl.BlockSpec((tm, 128), lambda b, i, j: (i, 0)),
                     pl.BlockSpec((tm, 128), lambda b, i, j: (i, 0))]
        args += [cos, sin]
    return pl.pallas_call(
        functools.partial(_proj_kernel, rope_tiles=tuple(rope_tiles)),
        grid=(bsz, t // tm, cols // tn),
        in_specs=in_specs,
        out_specs=pl.BlockSpec((1, tm, tn), lambda b, i, j: (b, i, j)),
        out_shape=jax.ShapeDtypeStruct((bsz, t, cols), out_dtype),
        scratch_shapes=[pltpu.VMEM((tm, D_MODEL), BF16)],
        compiler_params=_cparams(("parallel", "parallel", "arbitrary")),
        name=name,
    )(*args)


def _gla_masks(reverse):
    c, s = GLA_CHUNK, GLA_SUB
    i = np.arange(c)[:, None]
    j = np.arange(c)[None, :]
    if not reverse:
        cum = (j <= i)
        ref = (j < (i // s) * s)
    else:
        cum = (j >= i)
        ref = (j >= (i // s + 1) * s)
    return np.concatenate([cum, ref], axis=0).astype(np.float32)


def _gla_kernel(q_ref, k_ref, v_ref, glr_ref, cm_ref, wgk_ref, bgk_ref, *rest, reverse, n_chunks):
    if reverse:
        ofwd_ref, r_ref, ng_ref, o_ref, st_ref, acc_ref = rest
    else:
        o_ref, st_ref = rest
        acc_ref = o_ref.at[0]
    c, s = GLA_CHUNK, GLA_SUB
    nsub = c // s
    dk = GLA_DK

    @pl.when(pl.program_id(2) == 0)
    def _():
        st_ref[...] = jnp.zeros_like(st_ref)

    def lanes(x):
        return jnp.concatenate([x[ci * c:(ci + 1) * c] for ci in range(n_chunks)], axis=1)

    z = _dot(glr_ref[0].astype(BF16), wgk_ref[...]) + bgk_ref[...]
    g = lanes((jnp.minimum(z, 0.0) - jnp.log(1.0 + jnp.exp(-jnp.abs(z)))) * (1.0 / GLA_TAU))
    g_hi = g.astype(BF16)
    g_lo = (g - g_hi.astype(F32)).astype(BF16)
    cm = cm_ref[...].astype(BF16)
    cums = _dot(cm, g_hi) + _dot(cm, g_lo)
    bc, bref = cums[:c], cums[c:]
    q = lanes(q_ref[0]) * (GLA_DK ** -0.5)
    k = lanes(k_ref[0])
    qe = q * jnp.exp(bc - bref)
    qi = (qe * jnp.exp(bref)).astype(BF16)
    qe = qe.astype(BF16)
    bl = bc[0:1] if reverse else bc[c - 1:c]
    dec = jnp.exp(bl)
    kl = (k * jnp.exp(bl - bc)).astype(BF16)
    row = lax.broadcasted_iota(jnp.int32, (c, 1), 0)
    ksub = []
    for si in range(nsub):
        ok = (row >= si * s) if reverse else (row < (si + 1) * s)
        ksub.append(jnp.where(ok, k * jnp.exp(bref[si * s:si * s + 1] - bc), 0.0).astype(BF16))
    rr = lax.broadcasted_iota(jnp.int32, (c, nsub * c), 0)
    cc = lax.broadcasted_iota(jnp.int32, (c, nsub * c), 1)
    causal = (cc % c > rr) if reverse else (cc % c <= rr)
    keep = jnp.logical_and(cc // c == rr // s, causal)
    vs = [v_ref[0, ci * c:(ci + 1) * c, :].astype(BF16) for ci in range(n_chunks)]
    atts = []
    for ci in range(n_chunks):
        ls = slice(ci * dk, (ci + 1) * dk)
        kcat = jnp.concatenate([ks[:, ls] for ks in ksub], axis=0)
        atts.append(_dot_nt(qe[:, ls], kcat))
    atts = [jnp.where(keep, a, 0.0).astype(BF16) for a in atts]
    for ci in range(n_chunks):
        acc_ref[ci * c:(ci + 1) * c, :] = _dot(atts[ci], jnp.concatenate([vs[ci]] * nsub, axis=0))
    uts = [_dot_tn(vs[ci], kl[:, ci * dk:(ci + 1) * dk]) for ci in range(n_chunks)]

    st = st_ref[...]
    order = range(n_chunks - 1, -1, -1) if reverse else range(n_chunks)
    for ci in order:
        ls = slice(ci * dk, (ci + 1) * dk)
        inter = _dot_nt(qi[:, ls], st.astype(BF16))
        acc_ref[ci * c:(ci + 1) * c, :] = acc_ref[ci * c:(ci + 1) * c, :] + inter
        st = st * dec[:, ls] + uts[ci]
    st_ref[...] = st

    if reverse:
        y = ofwd_ref[0] + acc_ref[...]
        y = y * lax.rsqrt(jnp.mean(y * y, axis=-1, keepdims=True) + LN_EPS) * ng_ref[...]
        r = r_ref[0]
        o_ref[0] = (y * (r * _sigmoid(r))).astype(o_ref.dtype)


def _gla_dir(pa, cm, wgk, bgk, reverse, ofwd=None, norm_g=None):
    bsz, t, _ = pa.shape
    tt = min(2048, t)
    nt = t // tt

    def ti(i):
        return (nt - 1 - i) if reverse else i

    in_specs = [pl.BlockSpec((1, tt, GLA_DK), lambda b, h, i: (b, ti(i), h)),
                pl.BlockSpec((1, tt, GLA_DK), lambda b, h, i: (b, ti(i), GLA_HEADS + h)),
                pl.BlockSpec((1, tt, GLA_DV), lambda b, h, i: (b, ti(i), GLA_HEADS + h)),
                pl.BlockSpec((1, tt, 128), lambda b, h, i: (b, ti(i), 3072 // 128)),
                pl.BlockSpec((2 * GLA_CHUNK, GLA_CHUNK), lambda b, h, i: (0, 0)),
                pl.BlockSpec((128, GLA_DK), lambda b, h, i: (0, h)),
                pl.BlockSpec((1, GLA_DK), lambda b, h, i: (0, h))]
    args = [pa, pa, pa, pa, cm, wgk, bgk]
    scratch = [pltpu.VMEM((GLA_DV, GLA_DK), F32)]
    if reverse:
        in_specs += [pl.BlockSpec((1, tt, GLA_DV), lambda b, h, i: (b, ti(i), h)),
                     pl.BlockSpec((1, tt, GLA_DV), lambda b, h, i: (b, ti(i), 2 * GLA_HEADS + h)),
                     pl.BlockSpec((1, GLA_DV), lambda b, h, i: (0, 0))]
        args += [ofwd, pa, norm_g]
        scratch += [pltpu.VMEM((tt, GLA_DV), F32)]
        out_dtype = BF16
    else:
        out_dtype = F32
    return pl.pallas_call(
        functools.partial(_gla_kernel, reverse=reverse, n_chunks=tt // GLA_CHUNK),
        grid=(bsz, GLA_HEADS, nt),
        in_specs=in_specs,
        out_specs=pl.BlockSpec((1, tt, GLA_DV), lambda b, h, i: (b, ti(i), h)),
        out_shape=jax.ShapeDtypeStruct((bsz, t, GLA_HEADS * GLA_DV), out_dtype),
        scratch_shapes=scratch,
        compiler_params=_cparams(("parallel", "parallel", "arbitrary")),
        name="gla_bwd" if reverse else "gla_fwd",
    )(*args)


def _na_bias_table(rpb):
    qc = np.arange(GRID_W)[:, None]
    kc = np.arange(GRID_W)[None, :]
    cs = np.clip(qc - NA_KC // 2, 0, GRID_W - NA_KC)
    allowed = (kc >= cs) & (kc < cs + NA_KC)
    dc = np.clip(kc - qc + NA_KC - 1, 0, 2 * NA_KC - 2)
    onehot = (dc[None] == np.arange(2 * NA_KC - 1)[:, None, None]).astype(np.float32)
    tab = jnp.einsum("hrd,dqk->hrqk", rpb.astype(F32), jnp.asarray(onehot), precision=lax.Precision.HIGHEST)
    tab = jnp.where(jnp.asarray(allowed)[None, None], tab, NEG)
    tab = jnp.stack([tab[:, NA_KR - 1 - d:2 * NA_KR - 1 - d] for d in range(NA_KR)], axis=1)
    tab = tab.transpose(0, 1, 3, 2, 4).reshape(NA_HEADS // 2, 2, NA_KR, GRID_W, NA_KR * GRID_W)
    return tab.transpose(0, 2, 1, 3, 4).reshape(NA_HEADS // 2, NA_KR, 2 * GRID_W, NA_KR * GRID_W)


def _na_kernel(q_ref, k_ref, v_ref, bias_ref, o_ref, *, rows):
    lane = lax.broadcasted_iota(jnp.int32, (GRID_W, 128), 1)
    low = lane < NA_DH
    win = NA_KR * GRID_W
    nr = NA_ROW_UNROLL

    def body(it, carry):
        q0s, scs, vws = [], [], []
        for u in range(nr):
            r = it * nr + u
            rs = jnp.clip(r - NA_KR // 2, 0, rows - NA_KR)
            q0 = pl.multiple_of(r * GRID_W, GRID_W)
            k0 = pl.multiple_of(rs * GRID_W, GRID_W)
            q = q_ref[0, pl.ds(q0, GRID_W), :]
            zq = jnp.zeros_like(q)
            q2 = jnp.concatenate([jnp.where(low, q, zq), jnp.where(low, zq, q)], axis=0)
            scs.append(_dot_nt(q2, k_ref[0, pl.ds(k0, win), :]) + bias_ref[0, r - rs])
            vws.append(v_ref[0, pl.ds(k0, win), :])
            q0s.append(q0)
        ms = [jnp.max(sc, axis=-1, keepdims=True) for sc in scs]
        es = [jnp.exp(sc - m) for sc, m in zip(scs, ms)]
        zs = [jnp.sum(e, axis=-1, keepdims=True) for e in es]
        os_ = [_dot(e.astype(BF16), vw) / z for e, vw, z in zip(es, vws, zs)]
        for q0, o in zip(q0s, os_):
            o_ref[0, pl.ds(q0, GRID_W), :] = jnp.where(low, o[:GRID_W], o[GRID_W:]).astype(o_ref.dtype)
        return carry

    lax.fori_loop(0, rows // nr, body, 0)


def _na(pb, bias):
    bsz, t, _ = pb.shape
    rows = t // GRID_W
    assert rows >= NA_KR and rows % NA_ROW_UNROLL == 0
    npair = NA_HEADS // 2
    return pl.pallas_call(
        functools.partial(_na_kernel, rows=rows),
        grid=(bsz, npair),
        in_specs=[pl.BlockSpec((1, t, 128), lambda b, j: (b, 0, j)),
                  pl.BlockSpec((1, t, 128), lambda b, j: (b, 0, npair + j)),
                  pl.BlockSpec((1, t, 128), lambda b, j: (b, 0, 2 * npair + j)),
                  pl.BlockSpec((1, NA_KR, 2 * GRID_W, NA_KR * GRID_W), lambda b, j: (j, 0, 0, 0))],
        out_specs=pl.BlockSpec((1, t, 128), lambda b, j: (b, 0, j)),
        out_shape=jax.ShapeDtypeStruct((bsz, t, D_MODEL), BF16),
        compiler_params=_cparams(("parallel", "parallel")),
        name="na_attn",
    )(pb, pb, pb, bias)


def _diff_kernel(q_ref, k_ref, v_ref, lam_ref, ng_ref, o_ref, vt_ref, *, lam_init):
    dv = 2 * DIFF_D

    @pl.when(pl.program_id(2) == 0)
    def _():
        vt_ref[0:dv, :] = v_ref[0].astype(F32).T.astype(BF16)
        vt_ref[dv:, :] = jnp.ones((8, vt_ref.shape[1]), BF16)

    lv = lam_ref[...]
    l1 = jnp.sum(lv[0:1] * lv[1:2], axis=-1, keepdims=True)
    l2 = jnp.sum(lv[2:3] * lv[3:4], axis=-1, keepdims=True)
    lam = jnp.exp(l1) - jnp.exp(l2) + lam_init
    q = q_ref[0]
    k = k_ref[0]
    low = lax.broadcasted_iota(jnp.int32, q.shape, 1) < DIFF_D
    zq = jnp.zeros_like(q)
    t = k.shape[0]
    kc = min(DIFF_KEY_CHUNK, t)
    nc = t // kc
    qms = (jnp.where(low, q, zq), jnp.where(low, zq, q))
    st = [dict(m=None, acc=None) for _ in qms]
    sc = [[None] * nc for _ in qms]
    for p_, qm in enumerate(qms):
        sc[p_][0] = _dot_nt(k[:kc], qm)
    for c in range(nc):
        for p_, qm in enumerate(qms):
            if c + 1 < nc:
                sc[p_][c + 1] = _dot_nt(k[(c + 1) * kc:(c + 2) * kc], qm)
            s_ = sc[p_][c]
            d = st[p_]
            cm = jnp.max(s_, axis=0, keepdims=True)
            vc = vt_ref[:, c * kc:(c + 1) * kc]
            if c == 0:
                d["m"] = cm
                d["acc"] = _dot(vc, jnp.exp2(s_ - cm).astype(BF16))
            else:
                m_new = jnp.maximum(d["m"], cm)
                d["acc"] = jnp.exp2(d["m"] - m_new) * d["acc"] + _dot(vc, jnp.exp2(s_ - m_new).astype(BF16))
                d["m"] = m_new
    outs = [d["acc"][:dv] / d["acc"][dv:dv + 1] for d in st]
    o = outs[0] - lam * outs[1]
    y = o * lax.rsqrt(jnp.mean(o * o, axis=0, keepdims=True) + LN_EPS)
    o_ref[0] = (y.T * (ng_ref[...] * (1.0 - lam_init))).astype(o_ref.dtype)


def _diff(pb, lam_vec, norm_g, lam_init):
    bsz, t, _ = pb.shape
    tq = min(1024, t)
    base = (NA_HEADS * NA_DH * 3) // 128
    return pl.pallas_call(
        functools.partial(_diff_kernel, lam_init=lam_init),
        grid=(bsz, DIFF_HEADS, t // tq),
        in_specs=[pl.BlockSpec((1, tq, 128), lambda b, h, i: (b, i, base + h)),
                  pl.BlockSpec((1, t, 128), lambda b, h, i: (b, 0, base + DIFF_HEADS + h)),
                  pl.BlockSpec((1, t, 128), lambda b, h, i: (b, 0, base + 2 * DIFF_HEADS + h)),
                  pl.BlockSpec((4, DIFF_D), lambda b, h, i: (0, 0)),
                  pl.BlockSpec((1, 2 * DIFF_D), lambda b, h, i: (0, 0))],
        out_specs=pl.BlockSpec((1, tq, 128), lambda b, h, i: (b, i, h)),
        out_shape=jax.ShapeDtypeStruct((bsz, t, D_MODEL), BF16),
        scratch_shapes=[pltpu.VMEM((2 * DIFF_D + 8, t), BF16)],
        compiler_params=_cparams(("parallel", "parallel", "arbitrary")),
        name="diff_attn",
    )(pb, pb, pb, lam_vec, norm_g)


def _merge_kernel(x_ref, mod_ref, ya_ref, yb_ref, yc_ref, wgt_ref, wbr_ref, wo_ref, g_ref, b_ref, o_ref):
    x = x_ref[0]
    h = (x * (1.0 + mod_ref[0, 1:2, :]) + mod_ref[0, 0:1, :]).astype(BF16)
    merged = None
    for n, y_ref in enumerate((ya_ref, yb_ref, yc_ref)):
        gt = _dot(h, wgt_ref[:, n * D_MODEL:(n + 1) * D_MODEL])
        br = _dot(y_ref[0], wbr_ref[n])
        term = _sigmoid(gt) * br
        merged = term if merged is None else merged + term
    out = _dot(merged.astype(BF16), wo_ref[...])
    u = DN_ALPHA * x + mod_ref[0, 2:3, :] * out
    o_ref[0] = _layer_norm(u, g_ref[...], b_ref[...])


def _const_spec(shape, nidx):
    zeros = (0,) * len(shape)
    if nidx == 2:
        return pl.BlockSpec(shape, lambda b, i: zeros, pipeline_mode=pl.Buffered(1))
    return pl.BlockSpec(shape, lambda b, i, e: zeros, pipeline_mode=pl.Buffered(1))


def _merge(x, mod, ya, yb, yc, wgt, wbr, wo, g, b):
    bsz, t, _ = x.shape
    tm = min(512, t)
    tok = pl.BlockSpec((1, tm, D_MODEL), lambda bb, i: (bb, i, 0))
    return pl.pallas_call(
        _merge_kernel,
        grid=(bsz, t // tm),
        in_specs=[tok, pl.BlockSpec((1, 3, D_MODEL), lambda bb, i: (bb, 0, 0)), tok, tok, tok,
                  _const_spec((D_MODEL, 3 * D_MODEL), 2), _const_spec((3, D_MODEL, D_MODEL), 2),
                  _const_spec((D_MODEL, D_MODEL), 2), _const_spec((1, D_MODEL), 2), _const_spec((1, D_MODEL), 2)],
        out_specs=tok,
        out_shape=jax.ShapeDtypeStruct((bsz, t, D_MODEL), F32),
        compiler_params=_cparams(("parallel", "parallel")),
        name="merge_ln",
    )(x, mod, ya, yb, yc, wgt, wbr, wo, g, b)


def _sort_network(n):
    pairs = []
    p = 1
    while p < n:
        k = p
        while k >= 1:
            for jj in range(k % p, n - k, 2 * k):
                for ii in range(min(k, n - jj - k)):
                    if (ii + jj) // (2 * p) == (ii + jj + k) // (2 * p):
                        pairs.append((ii + jj, ii + jj + k))
            k //= 2
        p *= 2
    return pairs


_SORT16 = _sort_network(16)


def _top_rows(x, n):
    rows_in, lanes = x.shape
    groups = [x[8 * g:8 * g + 8] for g in range(rows_in // 8)]
    groups += [jnp.full((8, lanes), NEG, F32)] * (16 - len(groups))
    for i, j in _SORT16:
        hi = jnp.maximum(groups[i], groups[j])
        groups[j] = jnp.minimum(groups[i], groups[j])
        groups[i] = hi
    rows = []
    for r in range(n):
        m = jnp.max(groups[0], axis=0, keepdims=True)
        rows.append(m)
        hit = groups[0] >= m
        for v in range(min(16, n - r - 1)):
            nxt = groups[v + 1] if v + 1 < 16 else NEG
            groups[v] = jnp.where(hit, nxt, groups[v])
    return rows


def _peer_route_kernel(x_ref, mod_ref, wq_ref, sk_ref, h_ref, pj_ref, pb_ref):
    x = x_ref[0]
    h = (x * (1.0 + mod_ref[0, 1:2, :]) + mod_ref[0, 0:1, :]).astype(BF16)
    h_ref[0] = h
    q = _dot(h, wq_ref[...]).astype(BF16)
    kk = PEER_TOPK
    for hd in range(PEER_HEADS):
        qh = q[:, hd * PEER_DKEY:(hd + 1) * PEER_DKEY]
        s1 = _dot_nt(sk_ref[2 * hd], qh)
        s2 = _dot_nt(sk_ref[2 * hd + 1], qh)
        a = _top_rows(s1, kk + 1)
        b = _top_rows(s2, kk + 1)
        b_lo = jnp.concatenate(b[:8], axis=0)
        cand = [a[i] + b_lo for i in range(8)]
        cand.append(a[0] + jnp.concatenate(b[8:16], axis=0))
        cand.append(jnp.concatenate(a[8:16], axis=0) + b[0])
        cand.append(jnp.concatenate([a[0] + b[16], a[16] + b[0]] + [jnp.full_like(a[0], NEG)] * 6, axis=0))
        cand = jnp.concatenate(cand, axis=0)
        top = _top_rows(cand, kk + 1)
        thr = 0.5 * (top[kk - 1] + top[kk])
        mx = a[0] + b[0]
        zs = jnp.sum(jnp.where(cand >= thr, jnp.exp(cand - mx), 0.0), axis=0, keepdims=True)
        t1 = thr - s1
        cnt = jnp.zeros_like(s1)
        rank2 = jnp.zeros_like(s2)
        for c in range(kk):
            cnt = jnp.where(b[c] >= t1, float(c + 1), cnt)
        for c in range(kk + 1):
            rank2 = jnp.where(s2 < b[c], float(c + 1), rank2)
        e1 = jnp.exp(s1 - a[0]) / zs
        for c in range(cnt.shape[1] // 128):
            cs = slice(c * 128, (c + 1) * 128)
            pj_ref[0, hd, 0, :, c] = cnt[:, cs].reshape(PEER_NKEYS // 8, 8, 128)
            pj_ref[0, hd, 1, :, c] = e1[:, cs].reshape(PEER_NKEYS // 8, 8, 128)
        pb_ref[0, hd, 0] = rank2.astype(BF16)
        pb_ref[0, hd, 1] = jnp.exp(s2 - b[0]).astype(BF16)


def _peer_route(x1, mod, wq, sk):
    bsz, t, _ = x1.shape
    tp = min(256, t)
    pspec = pl.BlockSpec((1, PEER_HEADS, 2, PEER_NKEYS, tp), lambda b, i: (b, 0, 0, 0, i))
    jspec = pl.BlockSpec((1, PEER_HEADS, 2, PEER_NKEYS // 8, tp // 128, 8, 128), lambda b, i: (b, 0, 0, 0, i, 0, 0))
    return pl.pallas_call(
        _peer_route_kernel,
        grid=(bsz, t // tp),
        in_specs=[pl.BlockSpec((1, tp, D_MODEL), lambda b, i: (b, i, 0)),
                  pl.BlockSpec((1, 2, D_MODEL), lambda b, i: (b, 0, 0)),
                  _const_spec((D_MODEL, PEER_HEADS * PEER_DKEY), 2),
                  _const_spec((2 * PEER_HEADS, PEER_NKEYS, PEER_DKEY), 2)],
        out_specs=[pl.BlockSpec((1, tp, D_MODEL), lambda b, i: (b, i, 0)), jspec, pspec],
        out_shape=[jax.ShapeDtypeStruct((bsz, t, D_MODEL), BF16),
                   jax.ShapeDtypeStruct((bsz, PEER_HEADS, 2, PEER_NKEYS // 8, t // 128, 8, 128), F32),
                   jax.ShapeDtypeStruct((bsz, PEER_HEADS, 2, PEER_NKEYS, t), BF16)],
        compiler_params=_cparams(("parallel", "parallel")),
        name="peer_route",
    )(x1, mod, wq, sk)


PEER_LANE_CHUNK = 256
PEER_EB = 512


def _bcast_rows(pj_ref, hd, which, r0, cc, lc):
    parts = [pj_ref[0, hd, which, 0, cc * (lc // 128) + c, pl.ds(r0, 16, stride=0), :] for c in range(lc // 128)]
    return jnp.concatenate(parts, axis=1).astype(BF16)


def _peer_dense_kernel(h_ref, u_ref, vt_ref, pj_ref, pb_ref, x_ref, mod_ref, g_ref, b_ref, o_ref,
                       acc_ref, act0_ref, act1_ref, w0_ref, w1_ref, *, eb, nblk):
    e = pl.program_id(2)
    nk = PEER_NKEYS
    tm = acc_ref.shape[1]
    lc = PEER_LANE_CHUNK
    assert tm % lc == 0 and eb == 4 * nk
    zero = jnp.zeros((), BF16)

    @pl.when(e == 0)
    def _():
        acc_ref[...] = jnp.zeros_like(acc_ref)
        act1_ref[...] = jnp.zeros_like(act1_ref)
        w0_ref[...] = jnp.zeros_like(w0_ref)
        w1_ref[...] = jnp.zeros_like(w1_ref)

    def step(par, act_in, act_out, w_in, w_out):
        nj = eb // nk
        rc = nk // 2
        for cc in range(tm // lc):
            ls = slice(cc * lc, (cc + 1) * lc)
            for ri in range(2):
                rr = slice(ri * rc, (ri + 1) * rc)
                for jp in range(nj // 2):
                    if ri == 0:
                        us = slice(jp * (eb // 2), (jp + 1) * (eb // 2))
                        act_out[us, ls] = _dot_nt(u_ref[us, :], h_ref[0, ls, :])
                    else:
                        ds_ = slice(jp * (D_MODEL // 2), (jp + 1) * (D_MODEL // 2))
                        acc_ref[ds_, ls] += _dot(vt_ref[0, ds_, :], w_in[:, ls])
                    w = [None, None]
                    for hd in range(PEER_HEADS):
                        rank = pb_ref[0, hd, 0, rr, ls]
                        e2 = pb_ref[0, hd, 1, rr, ls]
                        for k in range(2):
                            r0 = (1 - par) * nj + jp * 2 + k
                            cnt = _bcast_rows(pj_ref, hd, 0, r0, cc, lc)
                            e1 = _bcast_rows(pj_ref, hd, 1, r0, cc, lc)
                            cnt = jnp.concatenate([cnt] * (rc // 16), axis=0)
                            e1 = jnp.concatenate([e1] * (rc // 16), axis=0)
                            term = jnp.where(rank < cnt, e2, zero) * e1
                            w[k] = term if w[k] is None else w[k] + term
                    for k in range(2):
                        jj = jp * 2 + k
                        rs = slice(jj * nk + ri * rc, jj * nk + (ri + 1) * rc)
                        a = act_in[rs, ls]
                        gelu = 0.5 * a * (1.0 + lax.erf(a * (2.0 ** -0.5)))
                        w_out[rs, ls] = w[k] * gelu.astype(BF16)

    @pl.when(e % 2 == 0)
    def _():
        step(0, act1_ref, act0_ref, w0_ref, w1_ref)

    @pl.when(e % 2 == 1)
    def _():
        step(1, act0_ref, act1_ref, w1_ref, w0_ref)

    @pl.when(e == pl.num_programs(2) - 1)
    def _():
        f = acc_ref[...].T
        x = x_ref[0]
        u = DN_ALPHA * x + mod_ref[0, 0:1, :] * f
        o_ref[0] = _layer_norm(u, g_ref[...], b_ref[...])


def _peer_dense(h2, u_tab, vt_tab, pj, pb, x1, mod, g, b):
    bsz, t, _ = x1.shape
    tm = min(1024, t)
    eb = PEER_EB
    tok = pl.BlockSpec((1, tm, D_MODEL), lambda bb, i, e: (bb, i, 0))
    pspec = pl.BlockSpec((1, PEER_HEADS, 2, PEER_NKEYS, tm), lambda bb, i, e: (bb, 0, 0, 0, i))
    nblk = PEER_N // eb
    assert 2 * (eb // PEER_NKEYS) == 8
    jspec = pl.BlockSpec((1, PEER_HEADS, 2, 1, tm // 128, 8, 128),
                         lambda bb, i, e: (bb, 0, 0, jnp.clip(e - 1, 0, nblk - 1) // 2, i, 0, 0))
    return pl.pallas_call(
        functools.partial(_peer_dense_kernel, eb=eb, nblk=nblk),
        grid=(bsz, t // tm, nblk + 2),
        in_specs=[tok,
                  pl.BlockSpec((eb, D_MODEL), lambda bb, i, e: (jnp.minimum(e, nblk - 1), 0)),
                  pl.BlockSpec((1, D_MODEL, eb), lambda bb, i, e: (jnp.clip(e - 2, 0, nblk - 1), 0, 0)),
                  jspec, pspec,
                  tok,
                  pl.BlockSpec((1, 1, D_MODEL), lambda bb, i, e: (bb, 0, 0)),
                  _const_spec((1, D_MODEL), 3), _const_spec((1, D_MODEL), 3)],
        out_specs=tok,
        out_shape=jax.ShapeDtypeStruct((bsz, t, D_MODEL), F32),
        scratch_shapes=[pltpu.VMEM((D_MODEL, tm), F32),
                        pltpu.VMEM((eb, tm), F32), pltpu.VMEM((eb, tm), F32),
                        pltpu.VMEM((eb, tm), BF16), pltpu.VMEM((eb, tm), BF16)],
        compiler_params=_cparams(("parallel", "parallel", "arbitrary")),
        name="peer_dense",
    )(h2, u_tab, vt_tab, pj, pb, x1, mod, g, b)


def _rope_tables(t):
    half = DIFF_D // 2
    inv = jnp.exp(-math.log(ROPE_THETA) * jnp.arange(half, dtype=F32) / half)
    ang = jnp.arange(t, dtype=F32)[:, None] * inv[None, :]
    cos, sin = jnp.cos(ang), jnp.sin(ang)
    cos128 = jnp.tile(cos, (1, 4))
    sin128 = jnp.tile(jnp.concatenate([-sin, sin], axis=1), (1, 2))
    return cos128, sin128


def _layer_params(l, w_in, gla_w_gk, gla_b_gk, gla_norm_g, na_rpb, diff_lambda, diff_norm_g, w_br, w_o,
                  ln1_g, ln1_b, peer_wq, peer_subkeys, peer_u, peer_v, ln2_g, ln2_b):
    w = w_in[l]
    w_gla = jnp.pad(w[:, :3104], ((0, 0), (0, GLA_COLS - 3104))).astype(BF16)
    w_att = w[:, 3104:3104 + ATT_COLS]
    qscale = np.ones((ATT_COLS,), np.float32)
    qscale[0:1024] = NA_DH ** -0.5
    qscale[3072:4096] = DIFF_D ** -0.5 * math.log2(math.e)
    w_att = (w_att * qscale).astype(BF16)
    w_gt = w[:, 3104 + ATT_COLS:].astype(BF16)
    wgk = jnp.zeros((2, 128, GLA_HEADS * GLA_DK), F32)
    wgk = wgk.at[0, :GLA_RANK].set(gla_w_gk[l, 0]).at[1, GLA_RANK:2 * GLA_RANK].set(gla_w_gk[l, 1]).astype(BF16)
    sk = peer_subkeys[l]
    half = PEER_DKEY // 2
    skp = jnp.zeros((PEER_HEADS, 2, PEER_NKEYS, PEER_DKEY), F32)
    skp = skp.at[:, 0, :, :half].set(sk[:, 0]).at[:, 1, :, half:].set(sk[:, 1])
    return dict(
        w_gla=w_gla, w_att=w_att, w_gt=w_gt, wgk=wgk, bgk=gla_b_gk[l],
        gla_g=gla_norm_g[l][None], na_bias=_na_bias_table(na_rpb[l]),
        lam=diff_lambda[l], diff_g=diff_norm_g[l][None],
        w_br=w_br[l].astype(BF16), w_o=w_o[l].astype(BF16),
        ln1_g=ln1_g[l][None], ln1_b=ln1_b[l][None],
        wq=peer_wq[l].astype(BF16), sk=skp.reshape(2 * PEER_HEADS, PEER_NKEYS, PEER_DKEY).astype(BF16),
        u=peer_u[l].astype(BF16),
        vt=peer_v[l].astype(BF16).reshape(PEER_N // PEER_EB, PEER_EB, D_MODEL).transpose(0, 2, 1),
        ln2_g=ln2_g[l][None], ln2_b=ln2_b[l][None],
        lam_init=0.8 - 0.6 * math.exp(-0.3 * l),
    )


def _layer(x, mod, p, rope, cms):
    pa = _proj(x, mod[:, 0:2], p["w_gla"], F32, tn=640, name="proj_gla")
    pb = _proj(x, mod[:, 0:2], p["w_att"], BF16, tn=1024, rope=rope + ((3, 4),), name="proj_att")
    ofwd = _gla_dir(pa, cms[0], p["wgk"][0], p["bgk"][0:1], reverse=False)
    ya = _gla_dir(pa, cms[1], p["wgk"][1], p["bgk"][1:2], reverse=True, ofwd=ofwd, norm_g=p["gla_g"])
    yb = _na(pb, p["na_bias"])
    yc = _diff(pb, p["lam"], p["diff_g"], p["lam_init"])
    x1 = _merge(x, mod[:, 0:3], ya, yb, yc, p["w_gt"], p["w_br"], p["w_o"], p["ln1_g"], p["ln1_b"])
    h2, pj, pb2 = _peer_route(x1, mod[:, 3:5], p["wq"], p["sk"])
    return _peer_dense(h2, p["u"], p["vt"], pj, pb2, x1, mod[:, 5:6], p["ln2_g"], p["ln2_b"])


def kernel(x_prompt, x_sample, c_prompt, c_sample, w_ada, b_ada, w_in, gla_w_gk, gla_b_gk, gla_norm_g, na_rpb, diff_lambda, diff_norm_g, w_br, w_o, ln1_g, ln1_b, peer_wq, peer_subkeys, peer_u, peer_v, ln2_g, ln2_b):
    nb = x_prompt.shape[0]
    c_all = jnp.concatenate([c_prompt, c_sample], axis=0)
    cms = (jnp.asarray(_gla_masks(False)), jnp.asarray(_gla_masks(True)))
    ropes = {x.shape[1]: _rope_tables(x.shape[1]) for x in (x_prompt, x_sample)}
    xs = [x_prompt, x_sample]
    for l in range(DEPTH):
        p = _layer_params(l, w_in, gla_w_gk, gla_b_gk, gla_norm_g, na_rpb, diff_lambda, diff_norm_g, w_br, w_o,
                          ln1_g, ln1_b, peer_wq, peer_subkeys, peer_u, peer_v, ln2_g, ln2_b)
        mod_all = _ada(c_all, w_ada[l].astype(BF16), b_ada[l][None]).reshape(c_all.shape[0], 6, D_MODEL)
        mods = [mod_all[:nb], mod_all[nb:]]
        xs = [_layer(x, m, p, ropes[x.shape[1]], cms) for x, m in zip(xs, mods)]
    return (xs[0], xs[1])
```

```python
import functools
import math

import numpy as np
import jax
import jax.numpy as jnp
from jax import lax
from jax.experimental import pallas as pl
from jax.experimental.pallas import tpu as pltpu

F32 = jnp.float32
BF16 = jnp.bfloat16

D_MODEL = 1024
DEPTH = 2
GRID_W = 64
GLA_HEADS, GLA_DK, GLA_DV, GLA_RANK, GLA_TAU, GLA_CHUNK = 4, 128, 256, 16, 16.0, 64
GLA_SUB = 16
NA_HEADS, NA_DH, NA_KR, NA_KC = 16, 64, 8, 16
NA_ROW_UNROLL = 8
DIFF_HEADS, DIFF_D = 8, 64
DIFF_KEY_CHUNK = 512
ROPE_THETA = 10000.0
PEER_HEADS, PEER_NKEYS, PEER_DKEY, PEER_TOPK = 8, 128, 128, 16
PEER_N = PEER_NKEYS * PEER_NKEYS
DN_ALPHA = (2 * DEPTH) ** 0.25
LN_EPS = 1e-5
NEG = -1e30

GLA_COLS = 3200
ATT_COLS = 6144
VMEM_LIMIT = 56 * 1024 * 1024


def _cparams(sem):
    return pltpu.CompilerParams(dimension_semantics=sem, vmem_limit_bytes=VMEM_LIMIT)


def _dot(a, b):
    return jnp.dot(a, b, preferred_element_type=F32)


def _dot_nt(a, b):
    return lax.dot_general(a, b, (((1,), (1,)), ((), ())), preferred_element_type=F32)


def _dot_tn(a, b):
    return lax.dot_general(a, b, (((0,), (0,)), ((), ())), preferred_element_type=F32)


def _sigmoid(x):
    return 1.0 / (1.0 + jnp.exp(-x))


def _layer_norm(u, g, b):
    mu = jnp.mean(u, axis=-1, keepdims=True)
    d = u - mu
    var = jnp.mean(d * d, axis=-1, keepdims=True)
    return d * lax.rsqrt(var + LN_EPS) * g + b


def _ada_kernel(c_ref, w_ref, b_ref, o_ref):
    c = c_ref[...]
    s = (c * _sigmoid(c)).astype(BF16)
    o_ref[...] = _dot(s, w_ref[...]) + b_ref[...]


def _ada(c, w, b):
    n = c.shape[0]
    cols = w.shape[1]
    return pl.pallas_call(
        _ada_kernel,
        grid=(cols // D_MODEL,),
        in_specs=[pl.BlockSpec((n, D_MODEL), lambda j: (0, 0)),
                  pl.BlockSpec((D_MODEL, D_MODEL), lambda j: (0, j)),
                  pl.BlockSpec((1, D_MODEL), lambda j: (0, j))],
        out_specs=pl.BlockSpec((n, D_MODEL), lambda j: (0, j)),
        out_shape=jax.ShapeDtypeStruct((n, cols), F32),
        compiler_params=_cparams(("arbitrary",)),
        name="ada_mod",
    )(c, w, b)


def _proj_kernel(x_ref, mod_ref, w_ref, *rest, rope_tiles):
    if rope_tiles:
        cos_ref, sin_ref, o_ref, h_ref = rest
    else:
        o_ref, h_ref = rest
    j = pl.program_id(2)

    @pl.when(j == 0)
    def _():
        x = x_ref[0]
        h_ref[...] = (x * (1.0 + mod_ref[0, 1:2, :]) + mod_ref[0, 0:1, :]).astype(BF16)

    acc = _dot(h_ref[...], w_ref[...])
    if not rope_tiles:
        o_ref[0] = acc.astype(o_ref.dtype)
        return

    is_rope = functools.reduce(jnp.logical_or, [j == t for t in rope_tiles])

    @pl.when(is_rope)
    def _():
        cos = cos_ref[...]
        sin = sin_ref[...]
        lane = lax.broadcasted_iota(jnp.int32, cos.shape, 1)
        first = (lane % DIFF_D) < (DIFF_D // 2)
        for c in range(acc.shape[1] // 128):
            xc = acc[:, c * 128:(c + 1) * 128]
            rot = jnp.where(first, pltpu.roll(xc, 128 - DIFF_D // 2, 1), pltpu.roll(xc, DIFF_D // 2, 1))
            o_ref[0, :, c * 128:(c + 1) * 128] = (xc * cos + rot * sin).astype(o_ref.dtype)

    @pl.when(jnp.logical_not(is_rope))
    def _():
        o_ref[0] = acc.astype(o_ref.dtype)


def _proj(x, mod, w, out_dtype, tn, rope=None, name="proj"):
    bsz, t, _ = x.shape
    cols = w.shape[1]
    tm = min(1024, t)
    in_specs = [pl.BlockSpec((1, tm, D_MODEL), lambda b, i, j: (b, i, 0)),
                pl.BlockSpec((1, 2, D_MODEL), lambda b, i, j: (b, 0, 0)),
                pl.BlockSpec((D_MODEL, tn), lambda b, i, j: (0, j))]
    args = [x, mod, w]
    rope_tiles = ()
    if rope is not None:
        cos, sin, rope_tiles = rope
        in_specs += [pl.BlockSpec((tm, 128), lambda b, i, j: (i, 0)),
                     pl.BlockSpec((tm, 128), lambda b, i, j: (i, 0))]
        args += [cos, sin]
    return pl.pallas_call(
        functools.partial(_proj_kernel, rope_tiles=tuple(rope_tiles)),
        grid=(bsz, t // tm, cols // tn),
        in_specs=in_specs,
        out_specs=pl.BlockSpec((1, tm, tn), lambda b, i, j: (b, i, j)),
        out_shape=jax.ShapeDtypeStruct((bsz, t, cols), out_dtype),
        scratch_shapes=[pltpu.VMEM((tm, D_MODEL), BF16)],
        compiler_params=_cparams(("parallel", "parallel", "arbitrary")),
        name=name,
    )(*args)


def _gla_masks(reverse):
    c, s = GLA_CHUNK, GLA_SUB
    i = np.arange(c)[:, None]
    j = np.arange(c)[None, :]
    if not reverse:
        cum = (j <= i)
        ref = (j < (i // s) * s)
    else:
        cum = (j >= i)
        ref = (j >= (i // s + 1) * s)
    return np.concatenate([cum, ref], axis=0).astype(np.float32)


def _gla_kernel(q_ref, k_ref, v_ref, glr_ref, cm_ref, wgk_ref, bgk_ref, *rest, reverse, n_chunks):
    if reverse:
        ofwd_ref, r_ref, ng_ref, o_ref, st_ref, acc_ref = rest
    else:
        o_ref, st_ref = rest
        acc_ref = o_ref.at[0]
    c, s = GLA_CHUNK, GLA_SUB
    nsub = c // s
    dk = GLA_DK

    @pl.when(pl.program_id(2) == 0)
    def _():
        st_ref[...] = jnp.zeros_like(st_ref)

    def lanes(x):
        return jnp.concatenate([x[ci * c:(ci + 1) * c] for ci in range(n_chunks)], axis=1)

    z = _dot(glr_ref[0].astype(BF16), wgk_ref[...]) + bgk_ref[...]
    g = lanes((jnp.minimum(z, 0.0) - jnp.log(1.0 + jnp.exp(-jnp.abs(z)))) * (1.0 / GLA_TAU))
    g_hi = g.astype(BF16)
    g_lo = (g - g_hi.astype(F32)).astype(BF16)
    cm = cm_ref[...].astype(BF16)
    cums = _dot(cm, g_hi) + _dot(cm, g_lo)
    bc, bref = cums[:c], cums[c:]
    q = lanes(q_ref[0]) * (GLA_DK ** -0.5)
    k = lanes(k_ref[0])
    qe = q * jnp.exp(bc - bref)
    qi = (qe * jnp.exp(bref)).astype(BF16)
    qe = qe.astype(BF16)
    bl = bc[0:1] if reverse else bc[c - 1:c]
    dec = jnp.exp(bl)
    kl = (k * jnp.exp(bl - bc)).astype(BF16)
    row = lax.broadcasted_iota(jnp.int32, (c, 1), 0)
    ksub = []
    for si in range(nsub):
        ok = (row >= si * s) if reverse else (row < (si + 1) * s)
        ksub.append(jnp.where(ok, k * jnp.exp(bref[si * s:si * s + 1] - bc), 0.0).astype(BF16))
    rr = lax.broadcasted_iota(jnp.int32, (c, nsub * c), 0)
    cc = lax.broadcasted_iota(jnp.int32, (c, nsub * c), 1)
    causal = (cc % c > rr) if reverse else (cc % c <= rr)
    keep = jnp.logical_and(cc // c == rr // s, causal)
    vs = [v_ref[0, ci * c:(ci + 1) * c, :].astype(BF16) for ci in range(n_chunks)]
    atts = []
    for ci in range(n_chunks):
        ls = slice(ci * dk, (ci + 1) * dk)
        kcat = jnp.concatenate([ks[:, ls] for ks in ksub], axis=0)
        atts.append(_dot_nt(qe[:, ls], kcat))
    atts = [jnp.where(keep, a, 0.0).astype(BF16) for a in atts]
    for ci in range(n_chunks):
        acc_ref[ci * c:(ci + 1) * c, :] = _dot(atts[ci], jnp.concatenate([vs[ci]] * nsub, axis=0))
    uts = [_dot_tn(vs[ci], kl[:, ci * dk:(ci + 1) * dk]) for ci in range(n_chunks)]

    st = st_ref[...]
    order = range(n_chunks - 1, -1, -1) if reverse else range(n_chunks)
    for ci in order:
        ls = slice(ci * dk, (ci + 1) * dk)
        inter = _dot_nt(qi[:, ls], st.astype(BF16))
        acc_ref[ci * c:(ci + 1) * c, :] = acc_ref[ci * c:(ci + 1) * c, :] + inter
        st = st * dec[:, ls] + uts[ci]
    st_ref[...] = st

    if reverse:
        y = ofwd_ref[0] + acc_ref[...]
        y = y * lax.rsqrt(jnp.mean(y * y, axis=-1, keepdims=True) + LN_EPS) * ng_ref[...]
        r = r_ref[0]
        o_ref[0] = (y * (r * _sigmoid(r))).astype(o_ref.dtype)


def _gla_dir(pa, cm, wgk, bgk, reverse, ofwd=None, norm_g=None):
    bsz, t, _ = pa.shape
    tt = min(2048, t)
    nt = t // tt

    def ti(i):
        return (nt - 1 - i) if reverse else i

    in_specs = [pl.BlockSpec((1, tt, GLA_DK), lambda b, h, i: (b, ti(i), h)),
                pl.BlockSpec((1, tt, GLA_DK), lambda b, h, i: (b, ti(i), GLA_HEADS + h)),
                pl.BlockSpec((1, tt, GLA_DV), lambda b, h, i: (b, ti(i), GLA_HEADS + h)),
                pl.BlockSpec((1, tt, 128), lambda b, h, i: (b, ti(i), 3072 // 128)),
                pl.BlockSpec((2 * GLA_CHUNK, GLA_CHUNK), lambda b, h, i: (0, 0)),
                pl.BlockSpec((128, GLA_DK), lambda b, h, i: (0, h)),
                pl.BlockSpec((1, GLA_DK), lambda b, h, i: (0, h))]
    args = [pa, pa, pa, pa, cm, wgk, bgk]
    scratch = [pltpu.VMEM((GLA_DV, GLA_DK), F32)]
    if reverse:
        in_specs += [pl.BlockSpec((1, tt, GLA_DV), lambda b, h, i: (b, ti(i), h)),
                     pl.BlockSpec((1, tt, GLA_DV), lambda b, h, i: (b, ti(i), 2 * GLA_HEADS + h)),
                     pl.BlockSpec((1, GLA_DV), lambda b, h, i: (0, 0))]
        args += [ofwd, pa, norm_g]
        scratch += [pltpu.VMEM((tt, GLA_DV), F32)]
        out_dtype = BF16
    else:
        out_dtype = F32
    return pl.pallas_call(
        functools.partial(_gla_kernel, reverse=reverse, n_chunks=tt // GLA_CHUNK),
        grid=(bsz, GLA_HEADS, nt),
        in_specs=in_specs,
        out_specs=pl.BlockSpec((1, tt, GLA_DV), lambda b, h, i: (b, ti(i), h)),
        out_shape=jax.ShapeDtypeStruct((bsz, t, GLA_HEADS * GLA_DV), out_dtype),
        scratch_shapes=scratch,
        compiler_params=_cparams(("parallel", "parallel", "arbitrary")),
        name="gla_bwd" if reverse else "gla_fwd",
    )(*args)


def _na_bias_table(rpb):
    qc = np.arange(GRID_W)[:, None]
    kc = np.arange(GRID_W)[None, :]
    cs = np.clip(qc - NA_KC // 2, 0, GRID_W - NA_KC)
    allowed = (kc >= cs) & (kc < cs + NA_KC)
    dc = np.clip(kc - qc + NA_KC - 1, 0, 2 * NA_KC - 2)
    onehot = (dc[None] == np.arange(2 * NA_KC - 1)[:, None, None]).astype(np.float32)
    tab = jnp.einsum("hrd,dqk->hrqk", rpb.astype(F32), jnp.asarray(onehot), precision=lax.Precision.HIGHEST)
    tab = jnp.where(jnp.asarray(allowed)[None, None], tab, NEG)
    tab = jnp.stack([tab[:, NA_KR - 1 - d:2 * NA_KR - 1 - d] for d in range(NA_KR)], axis=1)
    tab = tab.transpose(0, 1, 3, 2, 4).reshape(NA_HEADS // 2, 2, NA_KR, GRID_W, NA_KR * GRID_W)
    return tab.transpose(0, 2, 1, 3, 4).reshape(NA_HEADS // 2, NA_KR, 2 * GRID_W, NA_KR * GRID_W)


def _na_kernel(q_ref, k_ref, v_ref, bias_ref, o_ref, *, rows):
    lane = lax.broadcasted_iota(jnp.int32, (GRID_W, 128), 1)
    low = lane < NA_DH
    win = NA_KR * GRID_W
    nr = NA_ROW_UNROLL

    def body(it, carry):
        q0s, scs, vws = [], [], []
        for u in range(nr):
            r = it * nr + u
            rs = jnp.clip(r - NA_KR // 2, 0, rows - NA_KR)
            q0 = pl.multiple_of(r * GRID_W, GRID_W)
            k0 = pl.multiple_of(rs * GRID_W, GRID_W)
            q = q_ref[0, pl.ds(q0, GRID_W), :]
            zq = jnp.zeros_like(q)
            q2 = jnp.concatenate([jnp.where(low, q, zq), jnp.where(low, zq, q)], axis=0)
            scs.append(_dot_nt(q2, k_ref[0, pl.ds(k0, win), :]) + bias_ref[0, r - rs])
            vws.append(v_ref[0, pl.ds(k0, win), :])
            q0s.append(q0)
        ms = [jnp.max(sc, axis=-1, keepdims=True) for sc in scs]
        es = [jnp.exp(sc - m) for sc, m in zip(scs, ms)]
        zs = [jnp.sum(e, axis=-1, keepdims=True) for e in es]
        os_ = [_dot(e.astype(BF16), vw) / z for e, vw, z in zip(es, vws, zs)]
        for q0, o in zip(q0s, os_):
            o_ref[0, pl.ds(q0, GRID_W), :] = jnp.where(low, o[:GRID_W], o[GRID_W:]).astype(o_ref.dtype)
        return carry

    lax.fori_loop(0, rows // nr, body, 0)


def _na(pb, bias):
    bsz, t, _ = pb.shape
    rows = t // GRID_W
    assert rows >= NA_KR and rows % NA_ROW_UNROLL == 0
    npair = NA_HEADS // 2
    return pl.pallas_call(
        functools.partial(_na_kernel, rows=rows),
        grid=(bsz, npair),
        in_specs=[pl.BlockSpec((1, t, 128), lambda b, j: (b, 0, j)),
                  pl.BlockSpec((1, t, 128), lambda b, j: (b, 0, npair + j)),
                  pl.BlockSpec((1, t, 128), lambda b, j: (b, 0, 2 * npair + j)),
                  pl.BlockSpec((1, NA_KR, 2 * GRID_W, NA_KR * GRID_W), lambda b, j: (j, 0, 0, 0))],
        out_specs=pl.BlockSpec((1, t, 128), lambda b, j: (b, 0, j)),
        out_shape=jax.ShapeDtypeStruct((bsz, t, D_MODEL), BF16),
        compiler_params=_cparams(("parallel", "parallel")),
        name="na_attn",
    )(pb, pb, pb, bias)


def _diff_kernel(q_ref, k_ref, v_ref, lam_ref, ng_ref, o_ref, vt_ref, *, lam_init):
    dv = 2 * DIFF_D

    @pl.when(pl.program_id(2) == 0)
    def _():
        vt_ref[0:dv, :] = v_ref[0].astype(F32).T.astype(BF16)
        vt_ref[dv:, :] = jnp.ones((8, vt_ref.shape[1]), BF16)

    lv = lam_ref[...]
    l1 = jnp.sum(lv[0:1] * lv[1:2], axis=-1, keepdims=True)
    l2 = jnp.sum(lv[2:3] * lv[3:4], axis=-1, keepdims=True)
    lam = jnp.exp(l1) - jnp.exp(l2) + lam_init
    q = q_ref[0]
    k = k_ref[0]
    low = lax.broadcasted_iota(jnp.int32, q.shape, 1) < DIFF_D
    zq = jnp.zeros_like(q)
    t = k.shape[0]
    kc = min(DIFF_KEY_CHUNK, t)
    nc = t // kc
    qms = (jnp.where(low, q, zq), jnp.where(low, zq, q))
    st = [dict(m=None, acc=None) for _ in qms]
    sc = [[None] * nc for _ in qms]
    for p_, qm in enumerate(qms):
        sc[p_][0] = _dot_nt(k[:kc], qm)
    for c in range(nc):
        for p_, qm in enumerate(qms):
            if c + 1 < nc:
                sc[p_][c + 1] = _dot_nt(k[(c + 1) * kc:(c + 2) * kc], qm)
            s_ = sc[p_][c]
            d = st[p_]
            cm = jnp.max(s_, axis=0, keepdims=True)
            vc = vt_ref[:, c * kc:(c + 1) * kc]
            if c == 0:
                d["m"] = cm
                d["acc"] = _dot(vc, jnp.exp2(s_ - cm).astype(BF16))
            else:
                m_new = jnp.maximum(d["m"], cm)
                d["acc"] = jnp.exp2(d["m"] - m_new) * d["acc"] + _dot(vc, jnp.exp2(s_ - m_new).astype(BF16))
                d["m"] = m_new
    outs = [d["acc"][:dv] / d["acc"][dv:dv + 1] for d in st]
    o = outs[0] - lam * outs[1]
    y = o * lax.rsqrt(jnp.mean(o * o, axis=0, keepdims=True) + LN_EPS)
    o_ref[0] = (y.T * (ng_ref[...] * (1.0 - lam_init))).astype(o_ref.dtype)


def _diff(pb, lam_vec, norm_g, lam_init):
    bsz, t, _ = pb.shape
    tq = min(1024, t)
    base = (NA_HEADS * NA_DH * 3) // 128
    return pl.pallas_call(
        functools.partial(_diff_kernel, lam_init=lam_init),
        grid=(bsz, DIFF_HEADS, t // tq),
        in_specs=[pl.BlockSpec((1, tq, 128), lambda b, h, i: (b, i, base + h)),
                  pl.BlockSpec((1, t, 128), lambda b, h, i: (b, 0, base + DIFF_HEADS + h)),
                  pl.BlockSpec((1, t, 128), lambda b, h, i: (b, 0, base + 2 * DIFF_HEADS + h)),
                  pl.BlockSpec((4, DIFF_D), lambda b, h, i: (0, 0)),
                  pl.BlockSpec((1, 2 * DIFF_D), lambda b, h, i: (0, 0))],
        out_specs=pl.BlockSpec((1, tq, 128), lambda b, h, i: (b, i, h)),
        out_shape=jax.ShapeDtypeStruct((bsz, t, D_MODEL), BF16),
        scratch_shapes=[pltpu.VMEM((2 * DIFF_D + 8, t), BF16)],
        compiler_params=_cparams(("parallel", "parallel", "arbitrary")),
        name="diff_attn",
    )(pb, pb, pb, lam_vec, norm_g)


def _merge_kernel(x_ref, mod_ref, ya_ref, yb_ref, yc_ref, wgt_ref, wbr_ref, wo_ref, g_ref, b_ref, o_ref):
    x = x_ref[0]
    h = (x * (1.0 + mod_ref[0, 1:2, :]) + mod_ref[0, 0:1, :]).astype(BF16)
    merged = None
    for n, y_ref in enumerate((ya_ref, yb_ref, yc_ref)):
        gt = _dot(h, wgt_ref[:, n * D_MODEL:(n + 1) * D_MODEL])
        br = _dot(y_ref[0], wbr_ref[n])
        term = _sigmoid(gt) * br
        merged = term if merged is None else merged + term
    out = _dot(merged.astype(BF16), wo_ref[...])
    u = DN_ALPHA * x + mod_ref[0, 2:3, :] * out
    o_ref[0] = _layer_norm(u, g_ref[...], b_ref[...])


def _const_spec(shape, nidx):
    zeros = (0,) * len(shape)
    if nidx == 2:
        return pl.BlockSpec(shape, lambda b, i: zeros, pipeline_mode=pl.Buffered(1))
    return pl.BlockSpec(shape, lambda b, i, e: zeros, pipeline_mode=pl.Buffered(1))


def _merge(x, mod, ya, yb, yc, wgt, wbr, wo, g, b):
    bsz, t, _ = x.shape
    tm = min(512, t)
    tok = pl.BlockSpec((1, tm, D_MODEL), lambda bb, i: (bb, i, 0))
    return pl.pallas_call(
        _merge_kernel,
        grid=(bsz, t // tm),
        in_specs=[tok, pl.BlockSpec((1, 3, D_MODEL), lambda bb, i: (bb, 0, 0)), tok, tok, tok,
                  _const_spec((D_MODEL, 3 * D_MODEL), 2), _const_spec((3, D_MODEL, D_MODEL), 2),
                  _const_spec((D_MODEL, D_MODEL), 2), _const_spec((1, D_MODEL), 2), _const_spec((1, D_MODEL), 2)],
        out_specs=tok,
        out_shape=jax.ShapeDtypeStruct((bsz, t, D_MODEL), F32),
        compiler_params=_cparams(("parallel", "parallel")),
        name="merge_ln",
    )(x, mod, ya, yb, yc, wgt, wbr, wo, g, b)


def _sort_network(n):
    pairs = []
    p = 1
    while p < n:
        k = p
        while k >= 1:
            for jj in range(k % p, n - k, 2 * k):
                for ii in range(min(k, n - jj - k)):
                    if (ii + jj) // (2 * p) == (ii + jj + k) // (2 * p):
                        pairs.append((ii + jj, ii + jj + k))
            k //= 2
        p *= 2
    return pairs


_SORT16 = _sort_network(16)


def _top_rows(x, n):
    rows_in, lanes = x.shape
    groups = [x[8 * g:8 * g + 8] for g in range(rows_in // 8)]
    groups += [jnp.full((8, lanes), NEG, F32)] * (16 - len(groups))
    for i, j in _SORT16:
        hi = jnp.maximum(groups[i], groups[j])
        groups[j] = jnp.minimum(groups[i], groups[j])
        groups[i] = hi
    rows = []
    for r in range(n):
        m = jnp.max(groups[0], axis=0, keepdims=True)
        rows.append(m)
        hit = groups[0] >= m
        for v in range(min(16, n - r - 1)):
            nxt = groups[v + 1] if v + 1 < 16 else NEG
            groups[v] = jnp.where(hit, nxt, groups[v])
    return rows


def _peer_route_kernel(x_ref, mod_ref, wq_ref, sk_ref, h_ref, pj_ref, pb_ref):
    x = x_ref[0]
    h = (x * (1.0 + mod_ref[0, 1:2, :]) + mod_ref[0, 0:1, :]).astype(BF16)
    h_ref[0] = h
    q = _dot(h, wq_ref[...]).astype(BF16)
    kk = PEER_TOPK
    for hd in range(PEER_HEADS):
        qh = q[:, hd * PEER_DKEY:(hd + 1) * PEER_DKEY]
        s1 = _dot_nt(sk_ref[2 * hd], qh)
        s2 = _dot_nt(sk_ref[2 * hd + 1], qh)
        a = _top_rows(s1, kk + 1)
        b = _top_rows(s2, kk + 1)
        b_lo = jnp.concatenate(b[:8], axis=0)
        cand = [a[i] + b_lo for i in range(8)]
        cand.append(a[0] + jnp.concatenate(b[8:16], axis=0))
        cand.append(jnp.concatenate(a[8:16], axis=0) + b[0])
        cand.append(jnp.concatenate([a[0] + b[16], a[16] + b[0]] + [jnp.full_like(a[0], NEG)] * 6, axis=0))
        cand = jnp.concatenate(cand, axis=0)
        top = _top_rows(cand, kk + 1)
        thr = 0.5 * (top[kk - 1] + top[kk])
        mx = a[0] + b[0]
        zs = jnp.sum(jnp.where(cand >= thr, jnp.exp(cand - mx), 0.0), axis=0, keepdims=True)
        t1 = thr - s1
        cnt = jnp.zeros_like(s1)
        rank2 = jnp.zeros_like(s2)
        for c in range(kk):
            cnt = jnp.where(b[c] >= t1, float(c + 1), cnt)
        for c in range(kk + 1):
            rank2 = jnp.where(s2 < b[c], float(c + 1), rank2)
        e1 = jnp.exp(s1 - a[0]) / zs
        for c in range(cnt.shape[1] // 128):
            cs = slice(c * 128, (c + 1) * 128)
            pj_ref[0, hd, 0, :, c] = cnt[:, cs].reshape(PEER_NKEYS // 8, 8, 128)
            pj_ref[0, hd, 1, :, c] = e1[:, cs].reshape(PEER_NKEYS // 8, 8, 128)
        pb_ref[0, hd, 0] = rank2.astype(BF16)
        pb_ref[0, hd, 1] = jnp.exp(s2 - b[0]).astype(BF16)


def _peer_route(x1, mod, wq, sk):
    bsz, t, _ = x1.shape
    tp = min(256, t)
    pspec = pl.BlockSpec((1, PEER_HEADS, 2, PEER_NKEYS, tp), lambda b, i: (b, 0, 0, 0, i))
    jspec = pl.BlockSpec((1, PEER_HEADS, 2, PEER_NKEYS // 8, tp // 128, 8, 128), lambda b, i: (b, 0, 0, 0, i, 0, 0))
    return pl.pallas_call(
        _peer_route_kernel,
        grid=(bsz, t // tp),
        in_specs=[pl.BlockSpec((1, tp, D_MODEL), lambda b, i: (b, i, 0)),
                  pl.BlockSpec((1, 2, D_MODEL), lambda b, i: (b, 0, 0)),
                  _const_spec((D_MODEL, PEER_HEADS * PEER_DKEY), 2),
                  _const_spec((2 * PEER_HEADS, PEER_NKEYS, PEER_DKEY), 2)],
        out_specs=[pl.BlockSpec((1, tp, D_MODEL), lambda b, i: (b, i, 0)), jspec, pspec],
        out_shape=[jax.ShapeDtypeStruct((bsz, t, D_MODEL), BF16),
                   jax.ShapeDtypeStruct((bsz, PEER_HEADS, 2, PEER_NKEYS // 8, t // 128, 8, 128), F32),
                   jax.ShapeDtypeStruct((bsz, PEER_HEADS, 2, PEER_NKEYS, t), BF16)],
        compiler_params=_cparams(("parallel", "parallel")),
        name="peer_route",
    )(x1, mod, wq, sk)


PEER_LANE_CHUNK = 256
PEER_EB = 512


def _bcast_rows(pj_ref, hd, which, r0, cc, lc):
    parts = [pj_ref[0, hd, which, 0, cc * (lc // 128) + c, pl.ds(r0, 16, stride=0), :] for c in range(lc // 128)]
    return jnp.concatenate(parts, axis=1).astype(BF16)


def _peer_dense_kernel(h_ref, u_ref, vt_ref, pj_ref, pb_ref, x_ref, mod_ref, g_ref, b_ref, o_ref,
                       acc_ref, act0_ref, act1_ref, w0_ref, w1_ref, *, eb, nblk, nitems):
    e = pl.program_id(0)
    blk3 = jnp.clip(e - 2, 0, nitems - 1) % nblk
    nk = PEER_NKEYS
    tm = acc_ref.shape[1]
    lc = PEER_LANE_CHUNK
    assert tm % lc == 0 and eb == 4 * nk
    zero = jnp.zeros((), BF16)

    @pl.when(e == 0)
    def _():
        act1_ref[...] = jnp.zeros_like(act1_ref)
        w0_ref[...] = jnp.zeros_like(w0_ref)
        w1_ref[...] = jnp.zeros_like(w1_ref)

    @pl.when(blk3 == 0)
    def _():
        acc_ref[...] = jnp.zeros_like(acc_ref)

    def step(par, act_in, act_out, w_in, w_out):
        nj = eb // nk
        rc = nk // 2
        for cc in range(tm // lc):
            ls = slice(cc * lc, (cc + 1) * lc)
            for ri in range(2):
                rr = slice(ri * rc, (ri + 1) * rc)
                for jp in range(nj // 2):
                    if ri == 0:
                        us = slice(jp * (eb // 2), (jp + 1) * (eb // 2))
                        act_out[us, ls] = _dot_nt(u_ref[us, :], h_ref[0, ls, :])
                    else:
                        ds_ = slice(jp * (D_MODEL // 2), (jp + 1) * (D_MODEL // 2))
                        acc_ref[ds_, ls] += _dot(vt_ref[0, ds_, :], w_in[:, ls])
                    w = [None, None]
                    for hd in range(PEER_HEADS):
                        rank = pb_ref[0, hd, 0, rr, ls]
                        e2 = pb_ref[0, hd, 1, rr, ls]
                        for k in range(2):
                            r0 = (1 - par) * nj + jp * 2 + k
                            cnt = _bcast_rows(pj_ref, hd, 0, r0, cc, lc)
                            e1 = _bcast_rows(pj_ref, hd, 1, r0, cc, lc)
                            cnt = jnp.concatenate([cnt] * (rc // 16), axis=0)
                            e1 = jnp.concatenate([e1] * (rc // 16), axis=0)
                            term = jnp.where(rank < cnt, e2, zero) * e1
                            w[k] = term if w[k] is None else w[k] + term
                    for k in range(2):
                        jj = jp * 2 + k
                        rs = slice(jj * nk + ri * rc, jj * nk + (ri + 1) * rc)
                        a = act_in[rs, ls]
                        gelu = 0.5 * a * (1.0 + lax.erf(a * (2.0 ** -0.5)))
                        w_out[rs, ls] = w[k] * gelu.astype(BF16)

    @pl.when(e % 2 == 0)
    def _():
        step(0, act1_ref, act0_ref, w0_ref, w1_ref)

    @pl.when(e % 2 == 1)
    def _():
        step(1, act0_ref, act1_ref, w1_ref, w0_ref)

    @pl.when(blk3 == nblk - 1)
    def _():
        f = acc_ref[...].T
        x = x_ref[0]
        u = DN_ALPHA * x + mod_ref[0, 0:1, :] * f
        o_ref[0] = _layer_norm(u, g_ref[...], b_ref[...])


def _peer_dense(h2, u_tab, vt_tab, pj, pb, x1, mod, g, b):
    bsz, t, _ = x1.shape
    tm = min(1024, t)
    eb = PEER_EB
    nblk = PEER_N // eb
    ntile = t // tm
    nitems = bsz * ntile * nblk
    assert 2 * (eb // PEER_NKEYS) == 8 and nblk % 2 == 0

    def item(e, lag):
        it = jnp.clip(e - lag, 0, nitems - 1)
        tile = it // nblk
        return tile // ntile, tile % ntile, it % nblk

    def tok(lag):
        return pl.BlockSpec((1, tm, D_MODEL), lambda e: item(e, lag)[:2] + (0,))

    def const(shape):
        return pl.BlockSpec(shape, lambda e: (0,) * len(shape), pipeline_mode=pl.Buffered(1))

    return pl.pallas_call(
        functools.partial(_peer_dense_kernel, eb=eb, nblk=nblk, nitems=nitems),
        grid=(nitems + 2,),
        in_specs=[tok(0),
                  pl.BlockSpec((eb, D_MODEL), lambda e: (item(e, 0)[2], 0)),
                  pl.BlockSpec((1, D_MODEL, eb), lambda e: (item(e, 2)[2], 0, 0)),
                  pl.BlockSpec((1, PEER_HEADS, 2, 1, tm // 128, 8, 128),
                               lambda e: (item(e, 1)[0], 0, 0, item(e, 1)[2] // 2, item(e, 1)[1], 0, 0)),
                  pl.BlockSpec((1, PEER_HEADS, 2, PEER_NKEYS, tm), lambda e: (item(e, 1)[0], 0, 0, 0, item(e, 1)[1])),
                  tok(2),
                  pl.BlockSpec((1, 1, D_MODEL), lambda e: (item(e, 2)[0], 0, 0)),
                  const((1, D_MODEL)), const((1, D_MODEL))],
        out_specs=tok(2),
        out_shape=jax.ShapeDtypeStruct((bsz, t, D_MODEL), F32),
        scratch_shapes=[pltpu.VMEM((D_MODEL, tm), F32),
                        pltpu.VMEM((eb, tm), F32), pltpu.VMEM((eb, tm), F32),
                        pltpu.VMEM((eb, tm), BF16), pltpu.VMEM((eb, tm), BF16)],
        compiler_params=_cparams(("arbitrary",)),
        name="peer_dense",
    )(h2, u_tab, vt_tab, pj, pb, x1, mod, g, b)


def _rope_tables(t):
    half = DIFF_D // 2
    inv = jnp.exp(-math.log(ROPE_THETA) * jnp.arange(half, dtype=F32) / half)
    ang = jnp.arange(t, dtype=F32)[:, None] * inv[None, :]
    cos, sin = jnp.cos(ang), jnp.sin(ang)
    cos128 = jnp.tile(cos, (1, 4))
    sin128 = jnp.tile(jnp.concatenate([-sin, sin], axis=1), (1, 2))
    return cos128, sin128


def _layer_params(l, w_in, gla_w_gk, gla_b_gk, gla_norm_g, na_rpb, diff_lambda, diff_norm_g, w_br, w_o,
                  ln1_g, ln1_b, peer_wq, peer_subkeys, peer_u, peer_v, ln2_g, ln2_b):
    w = w_in[l]
    w_gla = jnp.pad(w[:, :3104], ((0, 0), (0, GLA_COLS - 3104))).astype(BF16)
    w_att = w[:, 3104:3104 + ATT_COLS]
    qscale = np.ones((ATT_COLS,), np.float32)
    qscale[0:1024] = NA_DH ** -0.5
    qscale[3072:4096] = DIFF_D ** -0.5 * math.log2(math.e)
    w_att = (w_att * qscale).astype(BF16)
    w_gt = w[:, 3104 + ATT_COLS:].astype(BF16)
    wgk = jnp.zeros((2, 128, GLA_HEADS * GLA_DK), F32)
    wgk = wgk.at[0, :GLA_RANK].set(gla_w_gk[l, 0]).at[1, GLA_RANK:2 * GLA_RANK].set(gla_w_gk[l, 1]).astype(BF16)
    sk = peer_subkeys[l]
    half = PEER_DKEY // 2
    skp = jnp.zeros((PEER_HEADS, 2, PEER_NKEYS, PEER_DKEY), F32)
    skp = skp.at[:, 0, :, :half].set(sk[:, 0]).at[:, 1, :, half:].set(sk[:, 1])
    return dict(
        w_gla=w_gla, w_att=w_att, w_gt=w_gt, wgk=wgk, bgk=gla_b_gk[l],
        gla_g=gla_norm_g[l][None], na_bias=_na_bias_table(na_rpb[l]),
        lam=diff_lambda[l], diff_g=diff_norm_g[l][None],
        w_br=w_br[l].astype(BF16), w_o=w_o[l].astype(BF16),
        ln1_g=ln1_g[l][None], ln1_b=ln1_b[l][None],
        wq=peer_wq[l].astype(BF16), sk=skp.reshape(2 * PEER_HEADS, PEER_NKEYS, PEER_DKEY).astype(BF16),
        u=peer_u[l].astype(BF16),
        vt=peer_v[l].astype(BF16).reshape(PEER_N // PEER_EB, PEER_EB, D_MODEL).transpose(0, 2, 1),
        ln2_g=ln2_g[l][None], ln2_b=ln2_b[l][None],
        lam_init=0.8 - 0.6 * math.exp(-0.3 * l),
    )


def _layer(x, mod, p, rope, cms):
    pa = _proj(x, mod[:, 0:2], p["w_gla"], F32, tn=640, name="proj_gla")
    pb = _proj(x, mod[:, 0:2], p["w_att"], BF16, tn=1024, rope=rope + ((3, 4),), name="proj_att")
    ofwd = _gla_dir(pa, cms[0], p["wgk"][0], p["bgk"][0:1], reverse=False)
    ya = _gla_dir(pa, cms[1], p["wgk"][1], p["bgk"][1:2], reverse=True, ofwd=ofwd, norm_g=p["gla_g"])
    yb = _na(pb, p["na_bias"])
    yc = _diff(pb, p["lam"], p["diff_g"], p["lam_init"])
    x1 = _merge(x, mod[:, 0:3], ya, yb, yc, p["w_gt"], p["w_br"], p["w_o"], p["ln1_g"], p["ln1_b"])
    h2, pj, pb2 = _peer_route(x1, mod[:, 3:5], p["wq"], p["sk"])
    return _peer_dense(h2, p["u"], p["vt"], pj, pb2, x1, mod[:, 5:6], p["ln2_g"], p["ln2_b"])


def kernel(x_prompt, x_sample, c_prompt, c_sample, w_ada, b_ada, w_in, gla_w_gk, gla_b_gk, gla_norm_g, na_rpb, diff_lambda, diff_norm_g, w_br, w_o, ln1_g, ln1_b, peer_wq, peer_subkeys, peer_u, peer_v, ln2_g, ln2_b):
    nb = x_prompt.shape[0]
    c_all = jnp.concatenate([c_prompt, c_sample], axis=0)
    cms = (jnp.asarray(_gla_masks(False)), jnp.asarray(_gla_masks(True)))
    ropes = {x.shape[1]: _rope_tables(x.shape[1]) for x in (x_prompt, x_sample)}
    xs = [x_prompt, x_sample]
    for l in range(DEPTH):
        p = _layer_params(l, w_in, gla_w_gk, gla_b_gk, gla_norm_g, na_rpb, diff_lambda, diff_norm_g, w_br, w_o,
                          ln1_g, ln1_b, peer_wq, peer_subkeys, peer_u, peer_v, ln2_g, ln2_b)
        mod_all = _ada(c_all, w_ada[l].astype(BF16), b_ada[l][None]).reshape(c_all.shape[0], 6, D_MODEL)
        mods = [mod_all[:nb], mod_all[nb:]]
        xs = [_layer(x, m, p, ropes[x.shape[1]], cms) for x, m in zip(xs, mods)]
    return (xs[0], xs[1])
```

```python
import functools
import math

import numpy as np
import jax
import jax.numpy as jnp
from jax import lax
from jax.experimental import pallas as pl
from jax.experimental.pallas import tpu as pltpu

F32 = jnp.float32
BF16 = jnp.bfloat16

D_MODEL = 1024
DEPTH = 2
GRID_W = 64
GLA_HEADS, GLA_DK, GLA_DV, GLA_RANK, GLA_TAU, GLA_CHUNK = 4, 128, 256, 16, 16.0, 64
GLA_SUB = 16
NA_HEADS, NA_DH, NA_KR, NA_KC = 16, 64, 8, 16
NA_ROW_UNROLL = 8
DIFF_HEADS, DIFF_D = 8, 64
DIFF_KEY_CHUNK = 512
ROPE_THETA = 10000.0
PEER_HEADS, PEER_NKEYS, PEER_DKEY, PEER_TOPK = 8, 128, 128, 16
PEER_N = PEER_NKEYS * PEER_NKEYS
DN_ALPHA = (2 * DEPTH) ** 0.25
LN_EPS = 1e-5
NEG = -1e30

GLA_COLS = 3200
ATT_COLS = 6144
VMEM_LIMIT = 56 * 1024 * 1024


def _cparams(sem):
    return pltpu.CompilerParams(dimension_semantics=sem, vmem_limit_bytes=VMEM_LIMIT)


def _dot(a, b):
    return jnp.dot(a, b, preferred_element_type=F32)


def _dot_nt(a, b):
    return lax.dot_general(a, b, (((1,), (1,)), ((), ())), preferred_element_type=F32)


def _dot_tn(a, b):
    return lax.dot_general(a, b, (((0,), (0,)), ((), ())), preferred_element_type=F32)


def _sigmoid(x):
    return 1.0 / (1.0 + jnp.exp(-x))


def _layer_norm(u, g, b):
    mu = jnp.mean(u, axis=-1, keepdims=True)
    d = u - mu
    var = jnp.mean(d * d, axis=-1, keepdims=True)
    return d * lax.rsqrt(var + LN_EPS) * g + b


def _ada_kernel(c_ref, w_ref, b_ref, o_ref):
    c = c_ref[...]
    s = (c * _sigmoid(c)).astype(BF16)
    o_ref[...] = _dot(s, w_ref[...]) + b_ref[...]


def _ada(c, w, b):
    n = c.shape[0]
    cols = w.shape[1]
    return pl.pallas_call(
        _ada_kernel,
        grid=(cols // D_MODEL,),
        in_specs=[pl.BlockSpec((n, D_MODEL), lambda j: (0, 0)),
                  pl.BlockSpec((D_MODEL, D_MODEL), lambda j: (0, j)),
                  pl.BlockSpec((1, D_MODEL), lambda j: (0, j))],
        out_specs=pl.BlockSpec((n, D_MODEL), lambda j: (0, j)),
        out_shape=jax.ShapeDtypeStruct((n, cols), F32),
        compiler_params=_cparams(("arbitrary",)),
        name="ada_mod",
    )(c, w, b)


PROJ_COL_CHUNK = 256


def _proj_kernel(x_ref, mod_ref, w_ref, *rest, rope_tiles):
    if rope_tiles:
        cos_ref, sin_ref, o_ref, h_ref = rest
    else:
        o_ref, h_ref = rest
    j = pl.program_id(2)

    @pl.when(j == 0)
    def _():
        x = x_ref[0]
        h_ref[...] = (x * (1.0 + mod_ref[0, 1:2, :]) + mod_ref[0, 0:1, :]).astype(BF16)

    tn = w_ref.shape[1]
    bounds = list(range(0, tn, PROJ_COL_CHUNK)) + [tn]

    def body(with_rope):
        h = h_ref[...]
        if with_rope:
            cos = cos_ref[...]
            sin = sin_ref[...]
        accs = {0: _dot(h, w_ref[:, bounds[0]:bounds[1]])}
        for c in range(len(bounds) - 1):
            if c + 2 < len(bounds):
                accs[c + 1] = _dot(h, w_ref[:, bounds[c + 1]:bounds[c + 2]])
            acc = accs.pop(c)
            if not with_rope:
                o_ref[0, :, bounds[c]:bounds[c + 1]] = acc.astype(o_ref.dtype)
                continue
            for p in range(acc.shape[1] // 128):
                xc = acc[:, p * 128:(p + 1) * 128]
                rot = pltpu.roll(xc, DIFF_D, 1)
                lo = bounds[c] + p * 128
                o_ref[0, :, lo:lo + 128] = (xc * cos + rot * sin).astype(o_ref.dtype)

    if not rope_tiles:
        body(False)
        return
    is_rope = functools.reduce(jnp.logical_or, [j == t for t in rope_tiles])
    pl.when(is_rope)(lambda: body(True))
    pl.when(jnp.logical_not(is_rope))(lambda: body(False))


def _proj(x, mod, w, out_dtype, tn, rope=None, name="proj"):
    bsz, t, _ = x.shape
    cols = w.shape[1]
    tm = min(1024, t)
    in_specs = [pl.BlockSpec((1, tm, D_MODEL), lambda b, i, j: (b, i, 0)),
                pl.BlockSpec((1, 2, D_MODEL), lambda b, i, j: (b, 0, 0)),
                pl.BlockSpec((D_MODEL, tn), lambda b, i, j: (0, j))]
    args = [x, mod, w]
    rope_tiles = ()
    if rope is not None:
        cos, sin, rope_tiles = rope
        in_specs += [pl.BlockSpec((tm, 128), lambda b, i, j: (i, 0)),
                     pl.BlockSpec((tm, 128), lambda b, i, j: (i, 0))]
        args += [cos, sin]
    return pl.pallas_call(
        functools.partial(_proj_kernel, rope_tiles=tuple(rope_tiles)),
        grid=(bsz, t // tm, cols // tn),
        in_specs=in_specs,
        out_specs=pl.BlockSpec((1, tm, tn), lambda b, i, j: (b, i, j)),
        out_shape=jax.ShapeDtypeStruct((bsz, t, cols), out_dtype),
        scratch_shapes=[pltpu.VMEM((tm, D_MODEL), BF16)],
        compiler_params=_cparams(("parallel", "parallel", "arbitrary")),
        name=name,
    )(*args)


def _gla_masks(reverse):
    c, s = GLA_CHUNK, GLA_SUB
    i = np.arange(c)[:, None]
    j = np.arange(c)[None, :]
    if not reverse:
        cum = (j <= i)
        ref = (j < (i // s) * s)
    else:
        cum = (j >= i)
        ref = (j >= (i // s + 1) * s)
    return np.concatenate([cum, ref], axis=0).astype(np.float32)


def _gla_kernel(q_ref, k_ref, v_ref, glr_ref, cm_ref, wgk_ref, bgk_ref, *rest, reverse, n_chunks):
    if reverse:
        ofwd_ref, r_ref, ng_ref, o_ref, st_ref, acc_ref = rest
    else:
        o_ref, st_ref = rest
        acc_ref = o_ref.at[0]
    c, s = GLA_CHUNK, GLA_SUB
    nsub = c // s
    dk = GLA_DK

    @pl.when(pl.program_id(2) == 0)
    def _():
        st_ref[...] = jnp.zeros_like(st_ref)

    def lanes(x):
        return jnp.concatenate([x[ci * c:(ci + 1) * c] for ci in range(n_chunks)], axis=1)

    z = _dot(glr_ref[0].astype(BF16), wgk_ref[...]) + bgk_ref[...]
    g = lanes((jnp.minimum(z, 0.0) - jnp.log(1.0 + jnp.exp(-jnp.abs(z)))) * (1.0 / GLA_TAU))
    g_hi = g.astype(BF16)
    g_lo = (g - g_hi.astype(F32)).astype(BF16)
    cm = cm_ref[...].astype(BF16)
    cums = _dot(cm, g_hi) + _dot(cm, g_lo)
    bc, bref = cums[:c], cums[c:]
    q = lanes(q_ref[0]) * (GLA_DK ** -0.5)
    k = lanes(k_ref[0])
    qe = q * jnp.exp(bc - bref)
    qi = (qe * jnp.exp(bref)).astype(BF16)
    qe = qe.astype(BF16)
    bl = bc[0:1] if reverse else bc[c - 1:c]
    dec = jnp.exp(bl)
    kl = (k * jnp.exp(bl - bc)).astype(BF16)
    row = lax.broadcasted_iota(jnp.int32, (c, 1), 0)
    ksub = []
    for si in range(nsub):
        ok = (row >= si * s) if reverse else (row < (si + 1) * s)
        ksub.append(jnp.where(ok, k * jnp.exp(bref[si * s:si * s + 1] - bc), 0.0).astype(BF16))
    rr = lax.broadcasted_iota(jnp.int32, (c, nsub * c), 0)
    cc = lax.broadcasted_iota(jnp.int32, (c, nsub * c), 1)
    causal = (cc % c > rr) if reverse else (cc % c <= rr)
    keep = jnp.logical_and(cc // c == rr // s, causal)
    vs = [v_ref[0, ci * c:(ci + 1) * c, :].astype(BF16) for ci in range(n_chunks)]
    atts = []
    for ci in range(n_chunks):
        ls = slice(ci * dk, (ci + 1) * dk)
        kcat = jnp.concatenate([ks[:, ls] for ks in ksub], axis=0)
        atts.append(_dot_nt(qe[:, ls], kcat))
    atts = [jnp.where(keep, a, 0.0).astype(BF16) for a in atts]
    for ci in range(n_chunks):
        acc_ref[ci * c:(ci + 1) * c, :] = _dot(atts[ci], jnp.concatenate([vs[ci]] * nsub, axis=0))
    uts = [_dot_tn(vs[ci], kl[:, ci * dk:(ci + 1) * dk]) for ci in range(n_chunks)]

    st = st_ref[...]
    order = range(n_chunks - 1, -1, -1) if reverse else range(n_chunks)
    for ci in order:
        ls = slice(ci * dk, (ci + 1) * dk)
        inter = _dot_nt(qi[:, ls], st.astype(BF16))
        acc_ref[ci * c:(ci + 1) * c, :] = acc_ref[ci * c:(ci + 1) * c, :] + inter
        st = st * dec[:, ls] + uts[ci]
    st_ref[...] = st

    if reverse:
        y = ofwd_ref[0] + acc_ref[...]
        y = y * lax.rsqrt(jnp.mean(y * y, axis=-1, keepdims=True) + LN_EPS) * ng_ref[...]
        r = r_ref[0]
        o_ref[0] = (y * (r * _sigmoid(r))).astype(o_ref.dtype)


def _gla_dir(pa, cm, wgk, bgk, reverse, ofwd=None, norm_g=None):
    bsz, t, _ = pa.shape
    tt = min(2048, t)
    nt = t // tt

    def ti(i):
        return (nt - 1 - i) if reverse else i

    in_specs = [pl.BlockSpec((1, tt, GLA_DK), lambda b, h, i: (b, ti(i), h)),
                pl.BlockSpec((1, tt, GLA_DK), lambda b, h, i: (b, ti(i), GLA_HEADS + h)),
                pl.BlockSpec((1, tt, GLA_DV), lambda b, h, i: (b, ti(i), GLA_HEADS + h)),
                pl.BlockSpec((1, tt, 128), lambda b, h, i: (b, ti(i), 3072 // 128)),
                pl.BlockSpec((2 * GLA_CHUNK, GLA_CHUNK), lambda b, h, i: (0, 0)),
                pl.BlockSpec((128, GLA_DK), lambda b, h, i: (0, h)),
                pl.BlockSpec((1, GLA_DK), lambda b, h, i: (0, h))]
    args = [pa, pa, pa, pa, cm, wgk, bgk]
    scratch = [pltpu.VMEM((GLA_DV, GLA_DK), F32)]
    if reverse:
        in_specs += [pl.BlockSpec((1, tt, GLA_DV), lambda b, h, i: (b, ti(i), h)),
                     pl.BlockSpec((1, tt, GLA_DV), lambda b, h, i: (b, ti(i), 2 * GLA_HEADS + h)),
                     pl.BlockSpec((1, GLA_DV), lambda b, h, i: (0, 0))]
        args += [ofwd, pa, norm_g]
        scratch += [pltpu.VMEM((tt, GLA_DV), F32)]
        out_dtype = BF16
    else:
        out_dtype = F32
    return pl.pallas_call(
        functools.partial(_gla_kernel, reverse=reverse, n_chunks=tt // GLA_CHUNK),
        grid=(bsz, GLA_HEADS, nt),
        in_specs=in_specs,
        out_specs=pl.BlockSpec((1, tt, GLA_DV), lambda b, h, i: (b, ti(i), h)),
        out_shape=jax.ShapeDtypeStruct((bsz, t, GLA_HEADS * GLA_DV), out_dtype),
        scratch_shapes=scratch,
        compiler_params=_cparams(("parallel", "parallel", "arbitrary")),
        name="gla_bwd" if reverse else "gla_fwd",
    )(*args)


def _na_bias_table(rpb):
    qc = np.arange(GRID_W)[:, None]
    kc = np.arange(GRID_W)[None, :]
    cs = np.clip(qc - NA_KC // 2, 0, GRID_W - NA_KC)
    allowed = (kc >= cs) & (kc < cs + NA_KC)
    dc = np.clip(kc - qc + NA_KC - 1, 0, 2 * NA_KC - 2)
    onehot = (dc[None] == np.arange(2 * NA_KC - 1)[:, None, None]).astype(np.float32)
    tab = jnp.einsum("hrd,dqk->hrqk", rpb.astype(F32), jnp.asarray(onehot), precision=lax.Precision.HIGHEST)
    tab = jnp.where(jnp.asarray(allowed)[None, None], tab, NEG)
    tab = jnp.stack([tab[:, NA_KR - 1 - d:2 * NA_KR - 1 - d] for d in range(NA_KR)], axis=1)
    tab = tab.transpose(0, 1, 3, 2, 4).reshape(NA_HEADS // 2, 2, NA_KR, GRID_W, NA_KR * GRID_W)
    return tab.transpose(0, 2, 1, 3, 4).reshape(NA_HEADS // 2, NA_KR, 2 * GRID_W, NA_KR * GRID_W)


def _na_kernel(q_ref, k_ref, v_ref, bias_ref, o_ref, *, rows):
    lane = lax.broadcasted_iota(jnp.int32, (GRID_W, 128), 1)
    low = lane < NA_DH
    win = NA_KR * GRID_W
    nr = NA_ROW_UNROLL

    def body(it, carry):
        q0s, scs, vws = [], [], []
        for u in range(nr):
            r = it * nr + u
            rs = jnp.clip(r - NA_KR // 2, 0, rows - NA_KR)
            q0 = pl.multiple_of(r * GRID_W, GRID_W)
            k0 = pl.multiple_of(rs * GRID_W, GRID_W)
            q = q_ref[0, pl.ds(q0, GRID_W), :]
            zq = jnp.zeros_like(q)
            q2 = jnp.concatenate([jnp.where(low, q, zq), jnp.where(low, zq, q)], axis=0)
            scs.append(_dot_nt(q2, k_ref[0, pl.ds(k0, win), :]) + bias_ref[0, r - rs])
            vws.append(v_ref[0, pl.ds(k0, win), :])
            q0s.append(q0)
        ms = [jnp.max(sc, axis=-1, keepdims=True) for sc in scs]
        es = [jnp.exp(sc - m) for sc, m in zip(scs, ms)]
        zs = [jnp.sum(e, axis=-1, keepdims=True) for e in es]
        os_ = [_dot(e.astype(BF16), vw) / z for e, vw, z in zip(es, vws, zs)]
        for q0, o in zip(q0s, os_):
            o_ref[0, pl.ds(q0, GRID_W), :] = jnp.where(low, o[:GRID_W], o[GRID_W:]).astype(o_ref.dtype)
        return carry

    lax.fori_loop(0, rows // nr, body, 0)


def _na(pb, bias):
    bsz, t, _ = pb.shape
    rows = t // GRID_W
    assert rows >= NA_KR and rows % NA_ROW_UNROLL == 0
    npair = NA_HEADS // 2
    return pl.pallas_call(
        functools.partial(_na_kernel, rows=rows),
        grid=(bsz, npair),
        in_specs=[pl.BlockSpec((1, t, 128), lambda b, j: (b, 0, j)),
                  pl.BlockSpec((1, t, 128), lambda b, j: (b, 0, npair + j)),
                  pl.BlockSpec((1, t, 128), lambda b, j: (b, 0, 2 * npair + j)),
                  pl.BlockSpec((1, NA_KR, 2 * GRID_W, NA_KR * GRID_W), lambda b, j: (j, 0, 0, 0))],
        out_specs=pl.BlockSpec((1, t, 128), lambda b, j: (b, 0, j)),
        out_shape=jax.ShapeDtypeStruct((bsz, t, D_MODEL), BF16),
        compiler_params=_cparams(("parallel", "parallel")),
        name="na_attn",
    )(pb, pb, pb, bias)


def _diff_kernel(q_ref, k_ref, v_ref, lam_ref, ng_ref, o_ref, vt_ref, *, lam_init):
    dv = 2 * DIFF_D

    @pl.when(pl.program_id(2) == 0)
    def _():
        vt_ref[0:dv, :] = v_ref[0].astype(F32).T.astype(BF16)
        vt_ref[dv:, :] = jnp.ones((8, vt_ref.shape[1]), BF16)

    lv = lam_ref[...]
    l1 = jnp.sum(lv[0:1] * lv[1:2], axis=-1, keepdims=True)
    l2 = jnp.sum(lv[2:3] * lv[3:4], axis=-1, keepdims=True)
    lam = jnp.exp(l1) - jnp.exp(l2) + lam_init
    q = q_ref[0]
    k = k_ref[0]
    low = lax.broadcasted_iota(jnp.int32, q.shape, 1) % DIFF_D < DIFF_D // 2
    zq = jnp.zeros_like(q)
    t = k.shape[0]
    kc = min(DIFF_KEY_CHUNK, t)
    nc = t // kc
    qms = (jnp.where(low, q, zq), jnp.where(low, zq, q))
    st = [dict(m=None, acc=None) for _ in qms]
    sc = [[None] * nc for _ in qms]
    for p_, qm in enumerate(qms):
        sc[p_][0] = _dot_nt(k[:kc], qm)
    for c in range(nc):
        for p_, qm in enumerate(qms):
            if c + 1 < nc:
                sc[p_][c + 1] = _dot_nt(k[(c + 1) * kc:(c + 2) * kc], qm)
            s_ = sc[p_][c]
            d = st[p_]
            cm = jnp.max(s_, axis=0, keepdims=True)
            vc = vt_ref[:, c * kc:(c + 1) * kc]
            if c == 0:
                d["m"] = cm
                d["acc"] = _dot(vc, jnp.exp2(s_ - cm).astype(BF16))
            else:
                m_new = jnp.maximum(d["m"], cm)
                d["acc"] = jnp.exp2(d["m"] - m_new) * d["acc"] + _dot(vc, jnp.exp2(s_ - m_new).astype(BF16))
                d["m"] = m_new
    outs = [d["acc"][:dv] / d["acc"][dv:dv + 1] for d in st]
    o = outs[0] - lam * outs[1]
    y = o * lax.rsqrt(jnp.mean(o * o, axis=0, keepdims=True) + LN_EPS)
    o_ref[0] = (y.T * (ng_ref[...] * (1.0 - lam_init))).astype(o_ref.dtype)


def _diff(pb, lam_vec, norm_g, lam_init):
    bsz, t, _ = pb.shape
    tq = min(1024, t)
    base = (NA_HEADS * NA_DH * 3) // 128
    return pl.pallas_call(
        functools.partial(_diff_kernel, lam_init=lam_init),
        grid=(bsz, DIFF_HEADS, t // tq),
        in_specs=[pl.BlockSpec((1, tq, 128), lambda b, h, i: (b, i, base + h)),
                  pl.BlockSpec((1, t, 128), lambda b, h, i: (b, 0, base + DIFF_HEADS + h)),
                  pl.BlockSpec((1, t, 128), lambda b, h, i: (b, 0, base + 2 * DIFF_HEADS + h)),
                  pl.BlockSpec((4, DIFF_D), lambda b, h, i: (0, 0)),
                  pl.BlockSpec((1, 2 * DIFF_D), lambda b, h, i: (0, 0))],
        out_specs=pl.BlockSpec((1, tq, 128), lambda b, h, i: (b, i, h)),
        out_shape=jax.ShapeDtypeStruct((bsz, t, D_MODEL), BF16),
        scratch_shapes=[pltpu.VMEM((2 * DIFF_D + 8, t), BF16)],
        compiler_params=_cparams(("parallel", "parallel", "arbitrary")),
        name="diff_attn",
    )(pb, pb, pb, lam_vec, norm_g)


def _merge_kernel(x_ref, mod_ref, ya_ref, yb_ref, yc_ref, wgt_ref, wbr_ref, wo_ref, g_ref, b_ref, o_ref):
    x = x_ref[0]
    h = (x * (1.0 + mod_ref[0, 1:2, :]) + mod_ref[0, 0:1, :]).astype(BF16)
    merged = None
    for n, y_ref in enumerate((ya_ref, yb_ref, yc_ref)):
        gt = _dot(h, wgt_ref[:, n * D_MODEL:(n + 1) * D_MODEL])
        br = _dot(y_ref[0], wbr_ref[n])
        term = _sigmoid(gt) * br
        merged = term if merged is None else merged + term
    out = _dot(merged.astype(BF16), wo_ref[...])
    u = DN_ALPHA * x + mod_ref[0, 2:3, :] * out
    o_ref[0] = _layer_norm(u, g_ref[...], b_ref[...])


def _const_spec(shape, nidx):
    zeros = (0,) * len(shape)
    if nidx == 2:
        return pl.BlockSpec(shape, lambda b, i: zeros, pipeline_mode=pl.Buffered(1))
    return pl.BlockSpec(shape, lambda b, i, e: zeros, pipeline_mode=pl.Buffered(1))


def _merge(x, mod, ya, yb, yc, wgt, wbr, wo, g, b):
    bsz, t, _ = x.shape
    tm = min(512, t)
    tok = pl.BlockSpec((1, tm, D_MODEL), lambda bb, i: (bb, i, 0))
    return pl.pallas_call(
        _merge_kernel,
        grid=(bsz, t // tm),
        in_specs=[tok, pl.BlockSpec((1, 3, D_MODEL), lambda bb, i: (bb, 0, 0)), tok, tok, tok,
                  _const_spec((D_MODEL, 3 * D_MODEL), 2), _const_spec((3, D_MODEL, D_MODEL), 2),
                  _const_spec((D_MODEL, D_MODEL), 2), _const_spec((1, D_MODEL), 2), _const_spec((1, D_MODEL), 2)],
        out_specs=tok,
        out_shape=jax.ShapeDtypeStruct((bsz, t, D_MODEL), F32),
        compiler_params=_cparams(("parallel", "parallel")),
        name="merge_ln",
    )(x, mod, ya, yb, yc, wgt, wbr, wo, g, b)


def _sort_network(n):
    pairs = []
    p = 1
    while p < n:
        k = p
        while k >= 1:
            for jj in range(k % p, n - k, 2 * k):
                for ii in range(min(k, n - jj - k)):
                    if (ii + jj) // (2 * p) == (ii + jj + k) // (2 * p):
                        pairs.append((ii + jj, ii + jj + k))
            k //= 2
        p *= 2
    return pairs


_SORT16 = _sort_network(16)


def _top_rows(x, n):
    rows_in, lanes = x.shape
    groups = [x[8 * g:8 * g + 8] for g in range(rows_in // 8)]
    groups += [jnp.full((8, lanes), NEG, F32)] * (16 - len(groups))
    for i, j in _SORT16:
        hi = jnp.maximum(groups[i], groups[j])
        groups[j] = jnp.minimum(groups[i], groups[j])
        groups[i] = hi
    rows = []
    for r in range(n):
        m = jnp.max(groups[0], axis=0, keepdims=True)
        rows.append(m)
        hit = groups[0] >= m
        for v in range(min(16, n - r - 1)):
            nxt = groups[v + 1] if v + 1 < 16 else NEG
            groups[v] = jnp.where(hit, nxt, groups[v])
    return rows


def _peer_route_kernel(x_ref, mod_ref, wq_ref, sk_ref, h_ref, pj_ref, pb_ref):
    x = x_ref[0]
    h = (x * (1.0 + mod_ref[0, 1:2, :]) + mod_ref[0, 0:1, :]).astype(BF16)
    h_ref[0] = h
    q = _dot(h, wq_ref[...]).astype(BF16)
    kk = PEER_TOPK
    for hd in range(PEER_HEADS):
        qh = q[:, hd * PEER_DKEY:(hd + 1) * PEER_DKEY]
        s1 = _dot_nt(sk_ref[2 * hd], qh)
        s2 = _dot_nt(sk_ref[2 * hd + 1], qh)
        a = _top_rows(s1, kk + 1)
        b = _top_rows(s2, kk + 1)
        b_lo = jnp.concatenate(b[:8], axis=0)
        cand = [a[i] + b_lo for i in range(8)]
        cand.append(a[0] + jnp.concatenate(b[8:16], axis=0))
        cand.append(jnp.concatenate(a[8:16], axis=0) + b[0])
        cand.append(jnp.concatenate([a[0] + b[16], a[16] + b[0]] + [jnp.full_like(a[0], NEG)] * 6, axis=0))
        cand = jnp.concatenate(cand, axis=0)
        top = _top_rows(cand, kk + 1)
        thr = 0.5 * (top[kk - 1] + top[kk])
        mx = a[0] + b[0]
        zs = jnp.sum(jnp.where(cand >= thr, jnp.exp(cand - mx), 0.0), axis=0, keepdims=True)
        t1 = thr - s1
        cnt = jnp.zeros_like(s1)
        rank2 = jnp.zeros_like(s2)
        for c in range(kk):
            cnt = jnp.where(b[c] >= t1, float(c + 1), cnt)
        for c in range(kk + 1):
            rank2 = jnp.where(s2 < b[c], float(c + 1), rank2)
        e1 = jnp.exp(s1 - a[0]) / zs
        for c in range(cnt.shape[1] // 128):
            cs = slice(c * 128, (c + 1) * 128)
            pj_ref[0, hd, 0, :, c] = cnt[:, cs].reshape(PEER_NKEYS // 8, 8, 128)
            pj_ref[0, hd, 1, :, c] = e1[:, cs].reshape(PEER_NKEYS // 8, 8, 128)
        pb_ref[0, hd, 0] = rank2.astype(BF16)
        pb_ref[0, hd, 1] = jnp.exp(s2 - b[0]).astype(BF16)


def _peer_route(x1, mod, wq, sk):
    bsz, t, _ = x1.shape
    tp = min(256, t)
    pspec = pl.BlockSpec((1, PEER_HEADS, 2, PEER_NKEYS, tp), lambda b, i: (b, 0, 0, 0, i))
    jspec = pl.BlockSpec((1, PEER_HEADS, 2, PEER_NKEYS // 8, tp // 128, 8, 128), lambda b, i: (b, 0, 0, 0, i, 0, 0))
    return pl.pallas_call(
        _peer_route_kernel,
        grid=(bsz, t // tp),
        in_specs=[pl.BlockSpec((1, tp, D_MODEL), lambda b, i: (b, i, 0)),
                  pl.BlockSpec((1, 2, D_MODEL), lambda b, i: (b, 0, 0)),
                  _const_spec((D_MODEL, PEER_HEADS * PEER_DKEY), 2),
                  _const_spec((2 * PEER_HEADS, PEER_NKEYS, PEER_DKEY), 2)],
        out_specs=[pl.BlockSpec((1, tp, D_MODEL), lambda b, i: (b, i, 0)), jspec, pspec],
        out_shape=[jax.ShapeDtypeStruct((bsz, t, D_MODEL), BF16),
                   jax.ShapeDtypeStruct((bsz, PEER_HEADS, 2, PEER_NKEYS // 8, t // 128, 8, 128), F32),
                   jax.ShapeDtypeStruct((bsz, PEER_HEADS, 2, PEER_NKEYS, t), BF16)],
        compiler_params=_cparams(("parallel", "parallel")),
        name="peer_route",
    )(x1, mod, wq, sk)


PEER_LANE_CHUNK = 256
PEER_EB = 512


def _bcast_rows(pj_ref, hd, which, r0, cc, lc):
    parts = [pj_ref[0, hd, which, 0, cc * (lc // 128) + c, pl.ds(r0, 16, stride=0), :] for c in range(lc // 128)]
    return jnp.concatenate(parts, axis=1).astype(BF16)


def _peer_dense_kernel(h_ref, u_ref, vt_ref, pj_ref, pb_ref, x_ref, mod_ref, g_ref, b_ref, o_ref,
                       acc_ref, act0_ref, act1_ref, w0_ref, w1_ref, *, eb, nblk, nitems):
    e = pl.program_id(0)
    blk3 = jnp.clip(e - 2, 0, nitems - 1) % nblk
    nk = PEER_NKEYS
    tm = acc_ref.shape[1]
    lc = PEER_LANE_CHUNK
    assert tm % lc == 0 and eb == 4 * nk
    zero = jnp.zeros((), BF16)

    @pl.when(e == 0)
    def _():
        act1_ref[...] = jnp.zeros_like(act1_ref)
        w0_ref[...] = jnp.zeros_like(w0_ref)
        w1_ref[...] = jnp.zeros_like(w1_ref)

    @pl.when(blk3 == 0)
    def _():
        acc_ref[...] = jnp.zeros_like(acc_ref)

    def step(par, act_in, act_out, w_in, w_out):
        nj = eb // nk
        rc = nk // 2
        for cc in range(tm // lc):
            ls = slice(cc * lc, (cc + 1) * lc)
            for ri in range(2):
                rr = slice(ri * rc, (ri + 1) * rc)
                for jp in range(nj // 2):
                    if ri == 0:
                        us = slice(jp * (eb // 2), (jp + 1) * (eb // 2))
                        act_out[us, ls] = _dot_nt(u_ref[us, :], h_ref[0, ls, :])
                    else:
                        ds_ = slice(jp * (D_MODEL // 2), (jp + 1) * (D_MODEL // 2))
                        acc_ref[ds_, ls] += _dot(vt_ref[0, ds_, :], w_in[:, ls])
                    w = [None, None]
                    for hd in range(PEER_HEADS):
                        rank = pb_ref[0, hd, 0, rr, ls]
                        e2 = pb_ref[0, hd, 1, rr, ls]
                        for k in range(2):
                            r0 = (1 - par) * nj + jp * 2 + k
                            cnt = _bcast_rows(pj_ref, hd, 0, r0, cc, lc)
                            e1 = _bcast_rows(pj_ref, hd, 1, r0, cc, lc)
                            cnt = jnp.concatenate([cnt] * (rc // 16), axis=0)
                            e1 = jnp.concatenate([e1] * (rc // 16), axis=0)
                            term = jnp.where(rank < cnt, e2, zero) * e1
                            w[k] = term if w[k] is None else w[k] + term
                    for k in range(2):
                        jj = jp * 2 + k
                        rs = slice(jj * nk + ri * rc, jj * nk + (ri + 1) * rc)
                        a = act_in[rs, ls]
                        gelu = 0.5 * a * (1.0 + lax.erf(a * (2.0 ** -0.5)))
                        w_out[rs, ls] = w[k] * gelu.astype(BF16)

    @pl.when(e % 2 == 0)
    def _():
        step(0, act1_ref, act0_ref, w0_ref, w1_ref)

    @pl.when(e % 2 == 1)
    def _():
        step(1, act0_ref, act1_ref, w1_ref, w0_ref)

    @pl.when(blk3 == nblk - 1)
    def _():
        f = acc_ref[...].T
        x = x_ref[0]
        u = DN_ALPHA * x + mod_ref[0, 0:1, :] * f
        o_ref[0] = _layer_norm(u, g_ref[...], b_ref[...])


def _peer_dense(h2, u_tab, vt_tab, pj, pb, x1, mod, g, b):
    bsz, t, _ = x1.shape
    tm = min(1024, t)
    eb = PEER_EB
    nblk = PEER_N // eb
    ntile = t // tm
    nitems = bsz * ntile * nblk
    assert 2 * (eb // PEER_NKEYS) == 8 and nblk % 2 == 0

    def item(e, lag):
        it = jnp.clip(e - lag, 0, nitems - 1)
        tile = it // nblk
        return tile // ntile, tile % ntile, it % nblk

    def tok(lag):
        return pl.BlockSpec((1, tm, D_MODEL), lambda e: item(e, lag)[:2] + (0,))

    def const(shape):
        return pl.BlockSpec(shape, lambda e: (0,) * len(shape), pipeline_mode=pl.Buffered(1))

    return pl.pallas_call(
        functools.partial(_peer_dense_kernel, eb=eb, nblk=nblk, nitems=nitems),
        grid=(nitems + 2,),
        in_specs=[tok(0),
                  pl.BlockSpec((eb, D_MODEL), lambda e: (item(e, 0)[2], 0)),
                  pl.BlockSpec((1, D_MODEL, eb), lambda e: (item(e, 2)[2], 0, 0)),
                  pl.BlockSpec((1, PEER_HEADS, 2, 1, tm // 128, 8, 128),
                               lambda e: (item(e, 1)[0], 0, 0, item(e, 1)[2] // 2, item(e, 1)[1], 0, 0)),
                  pl.BlockSpec((1, PEER_HEADS, 2, PEER_NKEYS, tm), lambda e: (item(e, 1)[0], 0, 0, 0, item(e, 1)[1])),
                  tok(2),
                  pl.BlockSpec((1, 1, D_MODEL), lambda e: (item(e, 2)[0], 0, 0)),
                  const((1, D_MODEL)), const((1, D_MODEL))],
        out_specs=tok(2),
        out_shape=jax.ShapeDtypeStruct((bsz, t, D_MODEL), F32),
        scratch_shapes=[pltpu.VMEM((D_MODEL, tm), F32),
                        pltpu.VMEM((eb, tm), F32), pltpu.VMEM((eb, tm), F32),
                        pltpu.VMEM((eb, tm), BF16), pltpu.VMEM((eb, tm), BF16)],
        compiler_params=_cparams(("arbitrary",)),
        name="peer_dense",
    )(h2, u_tab, vt_tab, pj, pb, x1, mod, g, b)


def _rope_tables(t):
    half = DIFF_D // 2
    inv = jnp.exp(-math.log(ROPE_THETA) * jnp.arange(half, dtype=F32) / half)
    ang = jnp.arange(t, dtype=F32)[:, None] * inv[None, :]
    cos, sin = jnp.cos(ang), jnp.sin(ang)
    cos128 = jnp.tile(cos, (1, 4))
    sin128 = jnp.concatenate([-sin, -sin, sin, sin], axis=1)
    return cos128, sin128


def _layer_params(l, w_in, gla_w_gk, gla_b_gk, gla_norm_g, na_rpb, diff_lambda, diff_norm_g, w_br, w_o,
                  ln1_g, ln1_b, peer_wq, peer_subkeys, peer_u, peer_v, ln2_g, ln2_b):
    w = w_in[l]
    w_gla = jnp.pad(w[:, :3104], ((0, 0), (0, GLA_COLS - 3104))).astype(BF16)
    w_att = w[:, 3104:3104 + ATT_COLS]
    qscale = np.ones((ATT_COLS,), np.float32)
    qscale[0:1024] = NA_DH ** -0.5
    qscale[3072:4096] = DIFF_D ** -0.5 * math.log2(math.e)
    half = DIFF_D // 2
    lane_src = np.concatenate([np.arange(0, half), np.arange(DIFF_D, DIFF_D + half),
                               np.arange(half, DIFF_D), np.arange(DIFF_D + half, 2 * DIFF_D)])
    cols = np.arange(ATT_COLS)
    for base in (3072, 4096):
        for hd in range(DIFF_HEADS):
            cols[base + hd * 128:base + (hd + 1) * 128] = base + hd * 128 + lane_src
    w_att = (w_att * qscale)[:, cols].astype(BF16)
    w_gt = w[:, 3104 + ATT_COLS:].astype(BF16)
    wgk = jnp.zeros((2, 128, GLA_HEADS * GLA_DK), F32)
    wgk = wgk.at[0, :GLA_RANK].set(gla_w_gk[l, 0]).at[1, GLA_RANK:2 * GLA_RANK].set(gla_w_gk[l, 1]).astype(BF16)
    sk = peer_subkeys[l]
    half = PEER_DKEY // 2
    skp = jnp.zeros((PEER_HEADS, 2, PEER_NKEYS, PEER_DKEY), F32)
    skp = skp.at[:, 0, :, :half].set(sk[:, 0]).at[:, 1, :, half:].set(sk[:, 1])
    return dict(
        w_gla=w_gla, w_att=w_att, w_gt=w_gt, wgk=wgk, bgk=gla_b_gk[l],
        gla_g=gla_norm_g[l][None], na_bias=_na_bias_table(na_rpb[l]),
        lam=diff_lambda[l], diff_g=diff_norm_g[l][None],
        w_br=w_br[l].astype(BF16), w_o=w_o[l].astype(BF16),
        ln1_g=ln1_g[l][None], ln1_b=ln1_b[l][None],
        wq=peer_wq[l].astype(BF16), sk=skp.reshape(2 * PEER_HEADS, PEER_NKEYS, PEER_DKEY).astype(BF16),
        u=peer_u[l].astype(BF16),
        vt=peer_v[l].astype(BF16).reshape(PEER_N // PEER_EB, PEER_EB, D_MODEL).transpose(0, 2, 1),
        ln2_g=ln2_g[l][None], ln2_b=ln2_b[l][None],
        lam_init=0.8 - 0.6 * math.exp(-0.3 * l),
    )


def _layer(x, mod, p, rope, cms):
    pa = _proj(x, mod[:, 0:2], p["w_gla"], F32, tn=640, name="proj_gla")
    pb = _proj(x, mod[:, 0:2], p["w_att"], BF16, tn=1024, rope=rope + ((3, 4),), name="proj_att")
    ofwd = _gla_dir(pa, cms[0], p["wgk"][0], p["bgk"][0:1], reverse=False)
    ya = _gla_dir(pa, cms[1], p["wgk"][1], p["bgk"][1:2], reverse=True, ofwd=ofwd, norm_g=p["gla_g"])
    yb = _na(pb, p["na_bias"])
    yc = _diff(pb, p["lam"], p["diff_g"], p["lam_init"])
    x1 = _merge(x, mod[:, 0:3], ya, yb, yc, p["w_gt"], p["w_br"], p["w_o"], p["ln1_g"], p["ln1_b"])
    h2, pj, pb2 = _peer_route(x1, mod[:, 3:5], p["wq"], p["sk"])
    return _peer_dense(h2, p["u"], p["vt"], pj, pb2, x1, mod[:, 5:6], p["ln2_g"], p["ln2_b"])


def kernel(x_prompt, x_sample, c_prompt, c_sample, w_ada, b_ada, w_in, gla_w_gk, gla_b_gk, gla_norm_g, na_rpb, diff_lambda, diff_norm_g, w_br, w_o, ln1_g, ln1_b, peer_wq, peer_subkeys, peer_u, peer_v, ln2_g, ln2_b):
    nb = x_prompt.shape[0]
    c_all = jnp.concatenate([c_prompt, c_sample], axis=0)
    cms = (jnp.asarray(_gla_masks(False)), jnp.asarray(_gla_masks(True)))
    ropes = {x.shape[1]: _rope_tables(x.shape[1]) for x in (x_prompt, x_sample)}
    xs = [x_prompt, x_sample]
    for l in range(DEPTH):
        p = _layer_params(l, w_in, gla_w_gk, gla_b_gk, gla_norm_g, na_rpb, diff_lambda, diff_norm_g, w_br, w_o,
                          ln1_g, ln1_b, peer_wq, peer_subkeys, peer_u, peer_v, ln2_g, ln2_b)
        mod_all = _ada(c_all, w_ada[l].astype(BF16), b_ada[l][None]).reshape(c_all.shape[0], 6, D_MODEL)
        mods = [mod_all[:nb], mod_all[nb:]]
        xs = [_layer(x, m, p, ropes[x.shape[1]], cms) for x, m in zip(xs, mods)]
    return (xs[0], xs[1])
```

```python
import functools
import math

import numpy as np
import jax
import jax.numpy as jnp
from jax import lax
from jax.experimental import pallas as pl
from jax.experimental.pallas import tpu as pltpu

F32 = jnp.float32
BF16 = jnp.bfloat16

D_MODEL = 1024
DEPTH = 2
GRID_W = 64
GLA_HEADS, GLA_DK, GLA_DV, GLA_RANK, GLA_TAU, GLA_CHUNK = 4, 128, 256, 16, 16.0, 64
GLA_SUB = 16
NA_HEADS, NA_DH, NA_KR, NA_KC = 16, 64, 8, 16
NA_ROW_UNROLL = 8
DIFF_HEADS, DIFF_D = 8, 64
DIFF_KEY_CHUNK = 512
ROPE_THETA = 10000.0
PEER_HEADS, PEER_NKEYS, PEER_DKEY, PEER_TOPK = 8, 128, 128, 16
PEER_N = PEER_NKEYS * PEER_NKEYS
DN_ALPHA = (2 * DEPTH) ** 0.25
LN_EPS = 1e-5
NEG = -1e30

GLA_COLS = 3328
ATT_COLS = 6144
VMEM_LIMIT = 56 * 1024 * 1024


def _cparams(sem):
    return pltpu.CompilerParams(dimension_semantics=sem, vmem_limit_bytes=VMEM_LIMIT)


def _dot(a, b):
    return jnp.dot(a, b, preferred_element_type=F32)


def _dot_nt(a, b):
    return lax.dot_general(a, b, (((1,), (1,)), ((), ())), preferred_element_type=F32)


def _dot_tn(a, b):
    return lax.dot_general(a, b, (((0,), (0,)), ((), ())), preferred_element_type=F32)


def _sigmoid(x):
    return 1.0 / (1.0 + jnp.exp(-x))


def _layer_norm(u, g, b):
    mu = jnp.mean(u, axis=-1, keepdims=True)
    d = u - mu
    var = jnp.mean(d * d, axis=-1, keepdims=True)
    return d * lax.rsqrt(var + LN_EPS) * g + b


def _ada_kernel(c_ref, w_ref, b_ref, o_ref):
    c = c_ref[...]
    s = (c * _sigmoid(c)).astype(BF16)
    o_ref[...] = _dot(s, w_ref[...]) + b_ref[...]


def _ada(c, w, b):
    n = c.shape[0]
    cols = w.shape[1]
    return pl.pallas_call(
        _ada_kernel,
        grid=(cols // D_MODEL,),
        in_specs=[pl.BlockSpec((n, D_MODEL), lambda j: (0, 0)),
                  pl.BlockSpec((D_MODEL, D_MODEL), lambda j: (0, j)),
                  pl.BlockSpec((1, D_MODEL), lambda j: (0, j))],
        out_specs=pl.BlockSpec((n, D_MODEL), lambda j: (0, j)),
        out_shape=jax.ShapeDtypeStruct((n, cols), F32),
        compiler_params=_cparams(("arbitrary",)),
        name="ada_mod",
    )(c, w, b)


PROJ_COL_CHUNK = 256


def _proj_kernel(x_ref, mod_ref, w_ref, *rest, rope_tiles):
    if rope_tiles:
        cos_ref, sin_ref, o_ref, h_ref = rest
    else:
        o_ref, h_ref = rest
    j = pl.program_id(2)

    @pl.when(j == 0)
    def _():
        x = x_ref[0]
        h_ref[...] = (x * (1.0 + mod_ref[0, 1:2, :]) + mod_ref[0, 0:1, :]).astype(BF16)

    tn = w_ref.shape[1]
    bounds = list(range(0, tn, PROJ_COL_CHUNK)) + [tn]

    def body(with_rope):
        h = h_ref[...]
        if with_rope:
            cos = cos_ref[...]
            sin = sin_ref[...]
        accs = {0: _dot(h, w_ref[:, bounds[0]:bounds[1]])}
        for c in range(len(bounds) - 1):
            if c + 2 < len(bounds):
                accs[c + 1] = _dot(h, w_ref[:, bounds[c + 1]:bounds[c + 2]])
            acc = accs.pop(c)
            if not with_rope:
                o_ref[0, :, bounds[c]:bounds[c + 1]] = acc.astype(o_ref.dtype)
                continue
            for p in range(acc.shape[1] // 128):
                xc = acc[:, p * 128:(p + 1) * 128]
                rot = pltpu.roll(xc, DIFF_D, 1)
                lo = bounds[c] + p * 128
                o_ref[0, :, lo:lo + 128] = (xc * cos + rot * sin).astype(o_ref.dtype)

    if not rope_tiles:
        body(False)
        return
    is_rope = functools.reduce(jnp.logical_or, [j == t for t in rope_tiles])
    pl.when(is_rope)(lambda: body(True))
    pl.when(jnp.logical_not(is_rope))(lambda: body(False))


def _proj(x, mod, w, out_dtype, tn, rope=None, name="proj"):
    bsz, t, _ = x.shape
    cols = w.shape[1]
    tm = min(1024, t)
    in_specs = [pl.BlockSpec((1, tm, D_MODEL), lambda b, i, j: (b, i, 0)),
                pl.BlockSpec((1, 2, D_MODEL), lambda b, i, j: (b, 0, 0)),
                pl.BlockSpec((D_MODEL, tn), lambda b, i, j: (0, j),
                             **({"pipeline_mode": pl.Buffered(1)} if tn == cols else {}))]
    args = [x, mod, w]
    rope_tiles = ()
    if rope is not None:
        cos, sin, rope_tiles = rope
        in_specs += [pl.BlockSpec((tm, 128), lambda b, i, j: (i, 0)),
                     pl.BlockSpec((tm, 128), lambda b, i, j: (i, 0))]
        args += [cos, sin]
    return pl.pallas_call(
        functools.partial(_proj_kernel, rope_tiles=tuple(rope_tiles)),
        grid=(bsz, t // tm, cols // tn),
        in_specs=in_specs,
        out_specs=pl.BlockSpec((1, tm, tn), lambda b, i, j: (b, i, j)),
        out_shape=jax.ShapeDtypeStruct((bsz, t, cols), out_dtype),
        scratch_shapes=[pltpu.VMEM((tm, D_MODEL), BF16)],
        compiler_params=_cparams(("parallel", "parallel", "arbitrary")),
        name=name,
    )(*args)


def _gla_masks(reverse):
    c, s = GLA_CHUNK, GLA_SUB
    i = np.arange(c)[:, None]
    j = np.arange(c)[None, :]
    if not reverse:
        cum = (j <= i)
        ref = (j < (i // s) * s)
    else:
        cum = (j >= i)
        ref = (j >= (i // s + 1) * s)
    return np.concatenate([cum, ref], axis=0).astype(np.float32)


def _gla_kernel(q_ref, k_ref, v_ref, glr_ref, cm_ref, wgk_ref, bgk_ref, *rest, reverse, n_chunks):
    if reverse:
        ofwd_ref, r_ref, ng_ref, o_ref, st_ref, acc_ref = rest
    else:
        o_ref, st_ref = rest
        acc_ref = o_ref.at[0]
    c, s = GLA_CHUNK, GLA_SUB
    nsub = c // s
    dk = GLA_DK

    @pl.when(pl.program_id(2) == 0)
    def _():
        st_ref[...] = jnp.zeros_like(st_ref)

    def lanes(x):
        return jnp.concatenate([x[ci * c:(ci + 1) * c] for ci in range(n_chunks)], axis=1)

    z = _dot(glr_ref[0].astype(BF16), wgk_ref[...]) + bgk_ref[...]
    g = lanes((jnp.minimum(z, 0.0) - jnp.log(1.0 + jnp.exp(-jnp.abs(z)))) * (1.0 / GLA_TAU))
    g_hi = g.astype(BF16)
    g_lo = (g - g_hi.astype(F32)).astype(BF16)
    cm = cm_ref[...].astype(BF16)
    cums = _dot(cm, g_hi) + _dot(cm, g_lo)
    bc, bref = cums[:c], cums[c:]
    q = lanes(q_ref[0]) * (GLA_DK ** -0.5)
    k = lanes(k_ref[0])
    qe = q * jnp.exp(bc - bref)
    qi = (qe * jnp.exp(bref)).astype(BF16)
    qe = qe.astype(BF16)
    bl = bc[0:1] if reverse else bc[c - 1:c]
    dec = jnp.exp(bl)
    kl = (k * jnp.exp(bl - bc)).astype(BF16)
    row = lax.broadcasted_iota(jnp.int32, (c, 1), 0)
    ksub = []
    for si in range(nsub):
        ok = (row >= si * s) if reverse else (row < (si + 1) * s)
        ksub.append(jnp.where(ok, k * jnp.exp(bref[si * s:si * s + 1] - bc), 0.0).astype(BF16))
    rr = lax.broadcasted_iota(jnp.int32, (c, nsub * c), 0)
    cc = lax.broadcasted_iota(jnp.int32, (c, nsub * c), 1)
    causal = (cc % c > rr) if reverse else (cc % c <= rr)
    keep = jnp.logical_and(cc // c == rr // s, causal)
    vs = [v_ref[0, ci * c:(ci + 1) * c, :].astype(BF16) for ci in range(n_chunks)]
    atts = []
    for ci in range(n_chunks):
        ls = slice(ci * dk, (ci + 1) * dk)
        kcat = jnp.concatenate([ks[:, ls] for ks in ksub], axis=0)
        atts.append(_dot_nt(qe[:, ls], kcat))
    atts = [jnp.where(keep, a, 0.0).astype(BF16) for a in atts]
    for ci in range(n_chunks):
        acc_ref[ci * c:(ci + 1) * c, :] = _dot(atts[ci], jnp.concatenate([vs[ci]] * nsub, axis=0))
    uts = [_dot_tn(vs[ci], kl[:, ci * dk:(ci + 1) * dk]) for ci in range(n_chunks)]

    st = st_ref[...]
    order = range(n_chunks - 1, -1, -1) if reverse else range(n_chunks)
    for ci in order:
        ls = slice(ci * dk, (ci + 1) * dk)
        inter = _dot_nt(qi[:, ls], st.astype(BF16))
        acc_ref[ci * c:(ci + 1) * c, :] = acc_ref[ci * c:(ci + 1) * c, :] + inter
        st = st * dec[:, ls] + uts[ci]
    st_ref[...] = st

    if reverse:
        y = ofwd_ref[0] + acc_ref[...]
        y = y * lax.rsqrt(jnp.mean(y * y, axis=-1, keepdims=True) + LN_EPS) * ng_ref[...]
        r = r_ref[0]
        o_ref[0] = (y * (r * _sigmoid(r))).astype(o_ref.dtype)


def _gla_dir(pa, cm, wgk, bgk, reverse, ofwd=None, norm_g=None):
    bsz, t, _ = pa.shape
    tt = min(2048, t)
    nt = t // tt

    def ti(i):
        return (nt - 1 - i) if reverse else i

    in_specs = [pl.BlockSpec((1, tt, GLA_DK), lambda b, h, i: (b, ti(i), h)),
                pl.BlockSpec((1, tt, GLA_DK), lambda b, h, i: (b, ti(i), GLA_HEADS + h)),
                pl.BlockSpec((1, tt, GLA_DV), lambda b, h, i: (b, ti(i), GLA_HEADS + h)),
                pl.BlockSpec((1, tt, 128), lambda b, h, i: (b, ti(i), 3072 // 128)),
                pl.BlockSpec((2 * GLA_CHUNK, GLA_CHUNK), lambda b, h, i: (0, 0)),
                pl.BlockSpec((128, GLA_DK), lambda b, h, i: (0, h)),
                pl.BlockSpec((1, GLA_DK), lambda b, h, i: (0, h))]
    args = [pa, pa, pa, pa, cm, wgk, bgk]
    scratch = [pltpu.VMEM((GLA_DV, GLA_DK), F32)]
    if reverse:
        in_specs += [pl.BlockSpec((1, tt, GLA_DV), lambda b, h, i: (b, ti(i), h)),
                     pl.BlockSpec((1, tt, GLA_DV), lambda b, h, i: (b, ti(i), 2 * GLA_HEADS + h)),
                     pl.BlockSpec((1, GLA_DV), lambda b, h, i: (0, 0))]
        args += [ofwd, pa, norm_g]
        scratch += [pltpu.VMEM((tt, GLA_DV), F32)]
        out_dtype = BF16
    else:
        out_dtype = F32
    return pl.pallas_call(
        functools.partial(_gla_kernel, reverse=reverse, n_chunks=tt // GLA_CHUNK),
        grid=(bsz, GLA_HEADS, nt),
        in_specs=in_specs,
        out_specs=pl.BlockSpec((1, tt, GLA_DV), lambda b, h, i: (b, ti(i), h)),
        out_shape=jax.ShapeDtypeStruct((bsz, t, GLA_HEADS * GLA_DV), out_dtype),
        scratch_shapes=scratch,
        compiler_params=_cparams(("parallel", "parallel", "arbitrary")),
        name="gla_bwd" if reverse else "gla_fwd",
    )(*args)


def _na_bias_table(rpb):
    qc = np.arange(GRID_W)[:, None]
    kc = np.arange(GRID_W)[None, :]
    cs = np.clip(qc - NA_KC // 2, 0, GRID_W - NA_KC)
    allowed = (kc >= cs) & (kc < cs + NA_KC)
    dc = np.clip(kc - qc + NA_KC - 1, 0, 2 * NA_KC - 2)
    onehot = (dc[None] == np.arange(2 * NA_KC - 1)[:, None, None]).astype(np.float32)
    tab = jnp.einsum("hrd,dqk->hrqk", rpb.astype(F32), jnp.asarray(onehot), precision=lax.Precision.HIGHEST)
    tab = jnp.where(jnp.asarray(allowed)[None, None], tab, NEG)
    tab = jnp.stack([tab[:, NA_KR - 1 - d:2 * NA_KR - 1 - d] for d in range(NA_KR)], axis=1)
    tab = tab.transpose(0, 1, 3, 2, 4).reshape(NA_HEADS // 2, 2, NA_KR, GRID_W, NA_KR * GRID_W)
    return tab.transpose(0, 2, 1, 3, 4).reshape(NA_HEADS // 2, NA_KR, 2 * GRID_W, NA_KR * GRID_W)


def _na_kernel(q_ref, k_ref, v_ref, bias_ref, o_ref, *, rows):
    lane = lax.broadcasted_iota(jnp.int32, (GRID_W, 128), 1)
    low = lane < NA_DH
    win = NA_KR * GRID_W
    nr = NA_ROW_UNROLL

    def body(it, carry):
        q0s, scs, vws = [], [], []
        for u in range(nr):
            r = it * nr + u
            rs = jnp.clip(r - NA_KR // 2, 0, rows - NA_KR)
            q0 = pl.multiple_of(r * GRID_W, GRID_W)
            k0 = pl.multiple_of(rs * GRID_W, GRID_W)
            q = q_ref[0, pl.ds(q0, GRID_W), :]
            zq = jnp.zeros_like(q)
            q2 = jnp.concatenate([jnp.where(low, q, zq), jnp.where(low, zq, q)], axis=0)
            scs.append(_dot_nt(q2, k_ref[0, pl.ds(k0, win), :]) + bias_ref[0, r - rs])
            vws.append(v_ref[0, pl.ds(k0, win), :])
            q0s.append(q0)
        ms = [jnp.max(sc, axis=-1, keepdims=True) for sc in scs]
        es = [jnp.exp(sc - m) for sc, m in zip(scs, ms)]
        zs = [jnp.sum(e, axis=-1, keepdims=True) for e in es]
        os_ = [_dot(e.astype(BF16), vw) / z for e, vw, z in zip(es, vws, zs)]
        for q0, o in zip(q0s, os_):
            o_ref[0, pl.ds(q0, GRID_W), :] = jnp.where(low, o[:GRID_W], o[GRID_W:]).astype(o_ref.dtype)
        return carry

    lax.fori_loop(0, rows // nr, body, 0)


def _na(pb, bias):
    bsz, t, _ = pb.shape
    rows = t // GRID_W
    assert rows >= NA_KR and rows % NA_ROW_UNROLL == 0
    npair = NA_HEADS // 2
    return pl.pallas_call(
        functools.partial(_na_kernel, rows=rows),
        grid=(bsz, npair),
        in_specs=[pl.BlockSpec((1, t, 128), lambda b, j: (b, 0, j)),
                  pl.BlockSpec((1, t, 128), lambda b, j: (b, 0, npair + j)),
                  pl.BlockSpec((1, t, 128), lambda b, j: (b, 0, 2 * npair + j)),
                  pl.BlockSpec((1, NA_KR, 2 * GRID_W, NA_KR * GRID_W), lambda b, j: (j, 0, 0, 0))],
        out_specs=pl.BlockSpec((1, t, 128), lambda b, j: (b, 0, j)),
        out_shape=jax.ShapeDtypeStruct((bsz, t, D_MODEL), BF16),
        compiler_params=_cparams(("parallel", "parallel")),
        name="na_attn",
    )(pb, pb, pb, bias)


def _diff_kernel(q_ref, k_ref, v_ref, lam_ref, ng_ref, o_ref, vt_ref, *, lam_init):
    dv = 2 * DIFF_D

    @pl.when(pl.program_id(2) == 0)
    def _():
        vt_ref[0:dv, :] = v_ref[0].astype(F32).T.astype(BF16)
        vt_ref[dv:, :] = jnp.ones((8, vt_ref.shape[1]), BF16)

    lv = lam_ref[...]
    l1 = jnp.sum(lv[0:1] * lv[1:2], axis=-1, keepdims=True)
    l2 = jnp.sum(lv[2:3] * lv[3:4], axis=-1, keepdims=True)
    lam = jnp.exp(l1) - jnp.exp(l2) + lam_init
    q = q_ref[0]
    k = k_ref[0]
    low = lax.broadcasted_iota(jnp.int32, q.shape, 1) % DIFF_D < DIFF_D // 2
    zq = jnp.zeros_like(q)
    t = k.shape[0]
    kc = min(DIFF_KEY_CHUNK, t)
    nc = t // kc
    qms = (jnp.where(low, q, zq), jnp.where(low, zq, q))
    st = [dict(m=None, acc=None) for _ in qms]
    sc = [[None] * nc for _ in qms]
    for p_, qm in enumerate(qms):
        sc[p_][0] = _dot_nt(k[:kc], qm)
    for c in range(nc):
        for p_, qm in enumerate(qms):
            if c + 1 < nc:
                sc[p_][c + 1] = _dot_nt(k[(c + 1) * kc:(c + 2) * kc], qm)
            s_ = sc[p_][c]
            d = st[p_]
            cm = jnp.max(s_, axis=0, keepdims=True)
            vc = vt_ref[:, c * kc:(c + 1) * kc]
            if c == 0:
                d["m"] = cm
                d["acc"] = _dot(vc, jnp.exp2(s_ - cm).astype(BF16))
            else:
                m_new = jnp.maximum(d["m"], cm)
                d["acc"] = jnp.exp2(d["m"] - m_new) * d["acc"] + _dot(vc, jnp.exp2(s_ - m_new).astype(BF16))
                d["m"] = m_new
    outs = [d["acc"][:dv] / d["acc"][dv:dv + 1] for d in st]
    o = outs[0] - lam * outs[1]
    y = o * lax.rsqrt(jnp.mean(o * o, axis=0, keepdims=True) + LN_EPS)
    o_ref[0] = (y.T * (ng_ref[...] * (1.0 - lam_init))).astype(o_ref.dtype)


def _diff(pb, lam_vec, norm_g, lam_init):
    bsz, t, _ = pb.shape
    tq = min(1024, t)
    base = (NA_HEADS * NA_DH * 3) // 128
    return pl.pallas_call(
        functools.partial(_diff_kernel, lam_init=lam_init),
        grid=(bsz, DIFF_HEADS, t // tq),
        in_specs=[pl.BlockSpec((1, tq, 128), lambda b, h, i: (b, i, base + h)),
                  pl.BlockSpec((1, t, 128), lambda b, h, i: (b, 0, base + DIFF_HEADS + h)),
                  pl.BlockSpec((1, t, 128), lambda b, h, i: (b, 0, base + 2 * DIFF_HEADS + h)),
                  pl.BlockSpec((4, DIFF_D), lambda b, h, i: (0, 0)),
                  pl.BlockSpec((1, 2 * DIFF_D), lambda b, h, i: (0, 0))],
        out_specs=pl.BlockSpec((1, tq, 128), lambda b, h, i: (b, i, h)),
        out_shape=jax.ShapeDtypeStruct((bsz, t, D_MODEL), BF16),
        scratch_shapes=[pltpu.VMEM((2 * DIFF_D + 8, t), BF16)],
        compiler_params=_cparams(("parallel", "parallel", "arbitrary")),
        name="diff_attn",
    )(pb, pb, pb, lam_vec, norm_g)


def _merge_kernel(x_ref, mod_ref, ya_ref, yb_ref, yc_ref, wgt_ref, wbr_ref, wo_ref, g_ref, b_ref, o_ref):
    x = x_ref[0]
    h = (x * (1.0 + mod_ref[0, 1:2, :]) + mod_ref[0, 0:1, :]).astype(BF16)
    merged = None
    for n, y_ref in enumerate((ya_ref, yb_ref, yc_ref)):
        gt = _dot(h, wgt_ref[:, n * D_MODEL:(n + 1) * D_MODEL])
        br = _dot(y_ref[0], wbr_ref[n])
        term = _sigmoid(gt) * br
        merged = term if merged is None else merged + term
    out = _dot(merged.astype(BF16), wo_ref[...])
    u = DN_ALPHA * x + mod_ref[0, 2:3, :] * out
    o_ref[0] = _layer_norm(u, g_ref[...], b_ref[...])


def _const_spec(shape, nidx):
    zeros = (0,) * len(shape)
    if nidx == 2:
        return pl.BlockSpec(shape, lambda b, i: zeros, pipeline_mode=pl.Buffered(1))
    return pl.BlockSpec(shape, lambda b, i, e: zeros, pipeline_mode=pl.Buffered(1))


def _merge(x, mod, ya, yb, yc, wgt, wbr, wo, g, b):
    bsz, t, _ = x.shape
    tm = min(512, t)
    tok = pl.BlockSpec((1, tm, D_MODEL), lambda bb, i: (bb, i, 0))
    return pl.pallas_call(
        _merge_kernel,
        grid=(bsz, t // tm),
        in_specs=[tok, pl.BlockSpec((1, 3, D_MODEL), lambda bb, i: (bb, 0, 0)), tok, tok, tok,
                  _const_spec((D_MODEL, 3 * D_MODEL), 2), _const_spec((3, D_MODEL, D_MODEL), 2),
                  _const_spec((D_MODEL, D_MODEL), 2), _const_spec((1, D_MODEL), 2), _const_spec((1, D_MODEL), 2)],
        out_specs=tok,
        out_shape=jax.ShapeDtypeStruct((bsz, t, D_MODEL), F32),
        compiler_params=_cparams(("parallel", "parallel")),
        name="merge_ln",
    )(x, mod, ya, yb, yc, wgt, wbr, wo, g, b)


def _sort_network(n):
    pairs = []
    p = 1
    while p < n:
        k = p
        while k >= 1:
            for jj in range(k % p, n - k, 2 * k):
                for ii in range(min(k, n - jj - k)):
                    if (ii + jj) // (2 * p) == (ii + jj + k) // (2 * p):
                        pairs.append((ii + jj, ii + jj + k))
            k //= 2
        p *= 2
    return pairs


_SORT16 = _sort_network(16)


def _top_rows(x, n):
    rows_in, lanes = x.shape
    groups = [x[8 * g:8 * g + 8] for g in range(rows_in // 8)]
    groups += [jnp.full((8, lanes), NEG, F32)] * (16 - len(groups))
    for i, j in _SORT16:
        if j >= rows_in // 8:
            continue
        hi = jnp.maximum(groups[i], groups[j])
        groups[j] = jnp.minimum(groups[i], groups[j])
        groups[i] = hi
    rows = []
    for r in range(n):
        m = jnp.max(groups[0], axis=0, keepdims=True)
        rows.append(m)
        hit = groups[0] >= m
        for v in range(min(16, n - r - 1)):
            nxt = groups[v + 1] if v + 1 < 16 else NEG
            groups[v] = jnp.where(hit, nxt, groups[v])
    return rows


def _peer_route_kernel(x_ref, mod_ref, wq_ref, sk_ref, h_ref, pj_ref, pb_ref):
    x = x_ref[0]
    h = (x * (1.0 + mod_ref[0, 1:2, :]) + mod_ref[0, 0:1, :]).astype(BF16)
    h_ref[0] = h
    q = _dot(h, wq_ref[...]).astype(BF16)
    kk = PEER_TOPK
    for hd in range(PEER_HEADS):
        qh = q[:, hd * PEER_DKEY:(hd + 1) * PEER_DKEY]
        s1 = _dot_nt(sk_ref[2 * hd], qh)
        s2 = _dot_nt(sk_ref[2 * hd + 1], qh)
        a = _top_rows(s1, kk + 1)
        b = _top_rows(s2, kk + 1)
        b_lo = jnp.concatenate(b[:8], axis=0)
        cand = [a[i] + b_lo for i in range(8)]
        cand.append(a[0] + jnp.concatenate(b[8:16], axis=0))
        cand.append(jnp.concatenate(a[8:16], axis=0) + b[0])
        cand.append(jnp.concatenate([a[0] + b[16], a[16] + b[0]] + [jnp.full_like(a[0], NEG)] * 6, axis=0))
        cand = jnp.concatenate(cand, axis=0)
        top = _top_rows(cand, kk + 1)
        thr = 0.5 * (top[kk - 1] + top[kk])
        mx = a[0] + b[0]
        zs = jnp.sum(jnp.where(cand >= thr, jnp.exp(cand - mx), 0.0), axis=0, keepdims=True)
        t1 = thr - s1
        cnt = jnp.zeros_like(s1)
        rank2 = jnp.zeros_like(s2)
        for c in range(kk):
            cnt = jnp.where(b[c] >= t1, float(c + 1), cnt)
        for c in range(kk + 1):
            rank2 = jnp.where(s2 < b[c], float(c + 1), rank2)
        e1 = jnp.exp(s1 - a[0]) / zs
        for c in range(cnt.shape[1] // 128):
            cs = slice(c * 128, (c + 1) * 128)
            pj_ref[0, hd, 0, :, c] = cnt[:, cs].reshape(PEER_NKEYS // 8, 8, 128)
            pj_ref[0, hd, 1, :, c] = e1[:, cs].reshape(PEER_NKEYS // 8, 8, 128)
        pb_ref[0, hd, 0] = rank2.astype(BF16)
        pb_ref[0, hd, 1] = jnp.exp(s2 - b[0]).astype(BF16)


def _peer_route(x1, mod, wq, sk):
    bsz, t, _ = x1.shape
    tp = min(256, t)
    pspec = pl.BlockSpec((1, PEER_HEADS, 2, PEER_NKEYS, tp), lambda b, i: (b, 0, 0, 0, i))
    jspec = pl.BlockSpec((1, PEER_HEADS, 2, PEER_NKEYS // 8, tp // 128, 8, 128), lambda b, i: (b, 0, 0, 0, i, 0, 0))
    return pl.pallas_call(
        _peer_route_kernel,
        grid=(bsz, t // tp),
        in_specs=[pl.BlockSpec((1, tp, D_MODEL), lambda b, i: (b, i, 0)),
                  pl.BlockSpec((1, 2, D_MODEL), lambda b, i: (b, 0, 0)),
                  _const_spec((D_MODEL, PEER_HEADS * PEER_DKEY), 2),
                  _const_spec((2 * PEER_HEADS, PEER_NKEYS, PEER_DKEY), 2)],
        out_specs=[pl.BlockSpec((1, tp, D_MODEL), lambda b, i: (b, i, 0)), jspec, pspec],
        out_shape=[jax.ShapeDtypeStruct((bsz, t, D_MODEL), BF16),
                   jax.ShapeDtypeStruct((bsz, PEER_HEADS, 2, PEER_NKEYS // 8, t // 128, 8, 128), F32),
                   jax.ShapeDtypeStruct((bsz, PEER_HEADS, 2, PEER_NKEYS, t), BF16)],
        compiler_params=_cparams(("parallel", "parallel")),
        name="peer_route",
    )(x1, mod, wq, sk)


PEER_LANE_CHUNK = 256
PEER_EB = 512


def _bcast_rows(pj_ref, hd, which, r0, cc, lc):
    parts = [pj_ref[0, hd, which, 0, cc * (lc // 128) + c, pl.ds(r0, 16, stride=0), :] for c in range(lc // 128)]
    return jnp.concatenate(parts, axis=1).astype(BF16)


def _peer_dense_kernel(h_ref, u_ref, vt_ref, pj_ref, pb_ref, x_ref, mod_ref, g_ref, b_ref, o_ref,
                       acc_ref, act0_ref, act1_ref, w0_ref, w1_ref, *, eb, nblk, nitems):
    e = pl.program_id(0)
    blk3 = jnp.clip(e - 2, 0, nitems - 1) % nblk
    nk = PEER_NKEYS
    tm = acc_ref.shape[1]
    lc = PEER_LANE_CHUNK
    assert tm % lc == 0 and eb == 4 * nk
    zero = jnp.zeros((), BF16)

    @pl.when(e == 0)
    def _():
        act1_ref[...] = jnp.zeros_like(act1_ref)
        w0_ref[...] = jnp.zeros_like(w0_ref)
        w1_ref[...] = jnp.zeros_like(w1_ref)

    @pl.when(blk3 == 0)
    def _():
        acc_ref[...] = jnp.zeros_like(acc_ref)

    def step(par, act_in, act_out, w_in, w_out):
        nj = eb // nk
        rc = nk // 2
        for cc in range(tm // lc):
            ls = slice(cc * lc, (cc + 1) * lc)
            for ri in range(2):
                rr = slice(ri * rc, (ri + 1) * rc)
                for jp in range(nj // 2):
                    if ri == 0:
                        us = slice(jp * (eb // 2), (jp + 1) * (eb // 2))
                        a = _dot_nt(u_ref[us, :], h_ref[0, ls, :])
                        act_out[us, ls] = (0.5 * a * (1.0 + lax.erf(a * (2.0 ** -0.5)))).astype(BF16)
                    else:
                        ds_ = slice(jp * (D_MODEL // 2), (jp + 1) * (D_MODEL // 2))
                        acc_ref[ds_, ls] += _dot(vt_ref[0, ds_, :], w_in[:, ls])
                    w = [None, None]
                    for hd in range(PEER_HEADS):
                        rank = pb_ref[0, hd, 0, rr, ls]
                        e2 = pb_ref[0, hd, 1, rr, ls]
                        for k in range(2):
                            r0 = (1 - par) * nj + jp * 2 + k
                            cnt = _bcast_rows(pj_ref, hd, 0, r0, cc, lc)
                            e1 = _bcast_rows(pj_ref, hd, 1, r0, cc, lc)
                            cnt = jnp.concatenate([cnt] * (rc // 16), axis=0)
                            e1 = jnp.concatenate([e1] * (rc // 16), axis=0)
                            term = jnp.where(rank < cnt, e2, zero) * e1
                            w[k] = term if w[k] is None else w[k] + term
                    for k in range(2):
                        jj = jp * 2 + k
                        rs = slice(jj * nk + ri * rc, jj * nk + (ri + 1) * rc)
                        w_out[rs, ls] = w[k] * act_in[rs, ls]

    @pl.when(e % 2 == 0)
    def _():
        step(0, act1_ref, act0_ref, w0_ref, w1_ref)

    @pl.when(e % 2 == 1)
    def _():
        step(1, act0_ref, act1_ref, w1_ref, w0_ref)

    @pl.when(blk3 == nblk - 1)
    def _():
        f = acc_ref[...].T
        x = x_ref[0]
        u = DN_ALPHA * x + mod_ref[0, 0:1, :] * f
        o_ref[0] = _layer_norm(u, g_ref[...], b_ref[...])


def _peer_dense(h2, u_tab, vt_tab, pj, pb, x1, mod, g, b):
    bsz, t, _ = x1.shape
    tm = min(1024, t)
    eb = PEER_EB
    nblk = PEER_N // eb
    ntile = t // tm
    nitems = bsz * ntile * nblk
    assert 2 * (eb // PEER_NKEYS) == 8 and nblk % 2 == 0

    def item(e, lag):
        it = jnp.clip(e - lag, 0, nitems - 1)
        tile = it // nblk
        return tile // ntile, tile % ntile, it % nblk

    def tok(lag):
        return pl.BlockSpec((1, tm, D_MODEL), lambda e: item(e, lag)[:2] + (0,))

    def const(shape):
        return pl.BlockSpec(shape, lambda e: (0,) * len(shape), pipeline_mode=pl.Buffered(1))

    return pl.pallas_call(
        functools.partial(_peer_dense_kernel, eb=eb, nblk=nblk, nitems=nitems),
        grid=(nitems + 2,),
        in_specs=[tok(0),
                  pl.BlockSpec((eb, D_MODEL), lambda e: (item(e, 0)[2], 0)),
                  pl.BlockSpec((1, D_MODEL, eb), lambda e: (item(e, 2)[2], 0, 0)),
                  pl.BlockSpec((1, PEER_HEADS, 2, 1, tm // 128, 8, 128),
                               lambda e: (item(e, 1)[0], 0, 0, item(e, 1)[2] // 2, item(e, 1)[1], 0, 0)),
                  pl.BlockSpec((1, PEER_HEADS, 2, PEER_NKEYS, tm), lambda e: (item(e, 1)[0], 0, 0, 0, item(e, 1)[1])),
                  tok(2),
                  pl.BlockSpec((1, 1, D_MODEL), lambda e: (item(e, 2)[0], 0, 0)),
                  const((1, D_MODEL)), const((1, D_MODEL))],
        out_specs=tok(2),
        out_shape=jax.ShapeDtypeStruct((bsz, t, D_MODEL), F32),
        scratch_shapes=[pltpu.VMEM((D_MODEL, tm), F32),
                        pltpu.VMEM((eb, tm), BF16), pltpu.VMEM((eb, tm), BF16),
                        pltpu.VMEM((eb, tm), BF16), pltpu.VMEM((eb, tm), BF16)],
        compiler_params=_cparams(("arbitrary",)),
        name="peer_dense",
    )(h2, u_tab, vt_tab, pj, pb, x1, mod, g, b)


def _rope_tables(t):
    half = DIFF_D // 2
    inv = jnp.exp(-math.log(ROPE_THETA) * jnp.arange(half, dtype=F32) / half)
    ang = jnp.arange(t, dtype=F32)[:, None] * inv[None, :]
    cos, sin = jnp.cos(ang), jnp.sin(ang)
    cos128 = jnp.tile(cos, (1, 4))
    sin128 = jnp.concatenate([-sin, -sin, sin, sin], axis=1)
    return cos128, sin128


def _layer_params(l, w_in, gla_w_gk, gla_b_gk, gla_norm_g, na_rpb, diff_lambda, diff_norm_g, w_br, w_o,
                  ln1_g, ln1_b, peer_wq, peer_subkeys, peer_u, peer_v, ln2_g, ln2_b):
    w = w_in[l]
    w_gla = jnp.pad(w[:, :3104], ((0, 0), (0, GLA_COLS - 3104))).astype(BF16)
    w_att = w[:, 3104:3104 + ATT_COLS]
    qscale = np.ones((ATT_COLS,), np.float32)
    qscale[0:1024] = NA_DH ** -0.5
    qscale[3072:4096] = DIFF_D ** -0.5 * math.log2(math.e)
    half = DIFF_D // 2
    lane_src = np.concatenate([np.arange(0, half), np.arange(DIFF_D, DIFF_D + half),
                               np.arange(half, DIFF_D), np.arange(DIFF_D + half, 2 * DIFF_D)])
    cols = np.arange(ATT_COLS)
    for base in (3072, 4096):
        for hd in range(DIFF_HEADS):
            cols[base + hd * 128:base + (hd + 1) * 128] = base + hd * 128 + lane_src
    w_att = (w_att * qscale)[:, cols].astype(BF16)
    w_gt = w[:, 3104 + ATT_COLS:].astype(BF16)
    wgk = jnp.zeros((2, 128, GLA_HEADS * GLA_DK), F32)
    wgk = wgk.at[0, :GLA_RANK].set(gla_w_gk[l, 0]).at[1, GLA_RANK:2 * GLA_RANK].set(gla_w_gk[l, 1]).astype(BF16)
    sk = peer_subkeys[l]
    half = PEER_DKEY // 2
    skp = jnp.zeros((PEER_HEADS, 2, PEER_NKEYS, PEER_DKEY), F32)
    skp = skp.at[:, 0, :, :half].set(sk[:, 0]).at[:, 1, :, half:].set(sk[:, 1])
    return dict(
        w_gla=w_gla, w_att=w_att, w_gt=w_gt, wgk=wgk, bgk=gla_b_gk[l],
        gla_g=gla_norm_g[l][None], na_bias=_na_bias_table(na_rpb[l]),
        lam=diff_lambda[l], diff_g=diff_norm_g[l][None],
        w_br=w_br[l].astype(BF16), w_o=w_o[l].astype(BF16),
        ln1_g=ln1_g[l][None], ln1_b=ln1_b[l][None],
        wq=peer_wq[l].astype(BF16), sk=skp.reshape(2 * PEER_HEADS, PEER_NKEYS, PEER_DKEY).astype(BF16),
        u=peer_u[l].astype(BF16),
        vt=peer_v[l].astype(BF16).reshape(PEER_N // PEER_EB, PEER_EB, D_MODEL).transpose(0, 2, 1),
        ln2_g=ln2_g[l][None], ln2_b=ln2_b[l][None],
        lam_init=0.8 - 0.6 * math.exp(-0.3 * l),
    )


def _layer(x, mod, p, rope, cms):
    pa = _proj(x, mod[:, 0:2], p["w_gla"], F32, tn=GLA_COLS, name="proj_gla")
    pb = _proj(x, mod[:, 0:2], p["w_att"], BF16, tn=1024, rope=rope + ((3, 4),), name="proj_att")
    ofwd = _gla_dir(pa, cms[0], p["wgk"][0], p["bgk"][0:1], reverse=False)
    ya = _gla_dir(pa, cms[1], p["wgk"][1], p["bgk"][1:2], reverse=True, ofwd=ofwd, norm_g=p["gla_g"])
    yb = _na(pb, p["na_bias"])
    yc = _diff(pb, p["lam"], p["diff_g"], p["lam_init"])
    x1 = _merge(x, mod[:, 0:3], ya, yb, yc, p["w_gt"], p["w_br"], p["w_o"], p["ln1_g"], p["ln1_b"])
    h2, pj, pb2 = _peer_route(x1, mod[:, 3:5], p["wq"], p["sk"])
    return _peer_dense(h2, p["u"], p["vt"], pj, pb2, x1, mod[:, 5:6], p["ln2_g"], p["ln2_b"])


def kernel(x_prompt, x_sample, c_prompt, c_sample, w_ada, b_ada, w_in, gla_w_gk, gla_b_gk, gla_norm_g, na_rpb, diff_lambda, diff_norm_g, w_br, w_o, ln1_g, ln1_b, peer_wq, peer_subkeys, peer_u, peer_v, ln2_g, ln2_b):
    nb = x_prompt.shape[0]
    c_all = jnp.concatenate([c_prompt, c_sample], axis=0)
    cms = (jnp.asarray(_gla_masks(False)), jnp.asarray(_gla_masks(True)))
    ropes = {x.shape[1]: _rope_tables(x.shape[1]) for x in (x_prompt, x_sample)}
    xs = [x_prompt, x_sample]
    for l in range(DEPTH):
        p = _layer_params(l, w_in, gla_w_gk, gla_b_gk, gla_norm_g, na_rpb, diff_lambda, diff_norm_g, w_br, w_o,
                          ln1_g, ln1_b, peer_wq, peer_subkeys, peer_u, peer_v, ln2_g, ln2_b)
        mod_all = _ada(c_all, w_ada[l].astype(BF16), b_ada[l][None]).reshape(c_all.shape[0], 6, D_MODEL)
        mods = [mod_all[:nb], mod_all[nb:]]
        xs = [_layer(x, m, p, ropes[x.shape[1]], cms) for x, m in zip(xs, mods)]
    return (xs[0], xs[1])
```

```python
import functools
import math

import numpy as np
import jax
import jax.numpy as jnp
from jax import lax
from jax.experimental import pallas as pl
from jax.experimental.pallas import tpu as pltpu

F32 = jnp.float32
BF16 = jnp.bfloat16

D_MODEL = 1024
DEPTH = 2
GRID_W = 64
GLA_HEADS, GLA_DK, GLA_DV, GLA_RANK, GLA_TAU, GLA_CHUNK = 4, 128, 256, 16, 16.0, 64
GLA_SUB = 16
NA_HEADS, NA_DH, NA_KR, NA_KC = 16, 64, 8, 16
NA_ROW_UNROLL = 8
DIFF_HEADS, DIFF_D = 8, 64
DIFF_KEY_CHUNK = 512
ROPE_THETA = 10000.0
PEER_HEADS, PEER_NKEYS, PEER_DKEY, PEER_TOPK = 8, 128, 128, 16
PEER_N = PEER_NKEYS * PEER_NKEYS
DN_ALPHA = (2 * DEPTH) ** 0.25
LN_EPS = 1e-5
NEG = -1e30

GLA_COLS = 3328
ATT_COLS = 6144
VMEM_LIMIT = 56 * 1024 * 1024


def _cparams(sem):
    return pltpu.CompilerParams(dimension_semantics=sem, vmem_limit_bytes=VMEM_LIMIT)


def _dot(a, b):
    return jnp.dot(a, b, preferred_element_type=F32)


def _dot_nt(a, b):
    return lax.dot_general(a, b, (((1,), (1,)), ((), ())), preferred_element_type=F32)


def _dot_tn(a, b):
    return lax.dot_general(a, b, (((0,), (0,)), ((), ())), preferred_element_type=F32)


def _sigmoid(x):
    return 1.0 / (1.0 + jnp.exp(-x))


def _layer_norm(u, g, b):
    mu = jnp.mean(u, axis=-1, keepdims=True)
    d = u - mu
    var = jnp.mean(d * d, axis=-1, keepdims=True)
    return d * lax.rsqrt(var + LN_EPS) * g + b


def _ada_kernel(c_ref, w_ref, b_ref, o_ref):
    c = c_ref[...]
    s = (c * _sigmoid(c)).astype(BF16)
    o_ref[...] = _dot(s, w_ref[...]) + b_ref[...]


def _ada(c, w, b):
    n = c.shape[0]
    cols = w.shape[1]
    return pl.pallas_call(
        _ada_kernel,
        grid=(cols // D_MODEL,),
        in_specs=[pl.BlockSpec((n, D_MODEL), lambda j: (0, 0)),
                  pl.BlockSpec((D_MODEL, D_MODEL), lambda j: (0, j)),
                  pl.BlockSpec((1, D_MODEL), lambda j: (0, j))],
        out_specs=pl.BlockSpec((n, D_MODEL), lambda j: (0, j)),
        out_shape=jax.ShapeDtypeStruct((n, cols), F32),
        compiler_params=_cparams(("arbitrary",)),
        name="ada_mod",
    )(c, w, b)


PROJ_COL_CHUNK = 256


def _proj_kernel(x_ref, mod_ref, w_ref, *rest, rope_tiles):
    if rope_tiles:
        cos_ref, sin_ref, o_ref, h_ref = rest
    else:
        o_ref, h_ref = rest
    j = pl.program_id(2)

    @pl.when(j == 0)
    def _():
        x = x_ref[0]
        h_ref[...] = (x * (1.0 + mod_ref[0, 1:2, :]) + mod_ref[0, 0:1, :]).astype(BF16)

    tn = w_ref.shape[1]
    bounds = list(range(0, tn, PROJ_COL_CHUNK)) + [tn]

    def body(with_rope):
        h = h_ref[...]
        if with_rope:
            cos = cos_ref[...]
            sin = sin_ref[...]
        accs = {0: _dot(h, w_ref[:, bounds[0]:bounds[1]])}
        for c in range(len(bounds) - 1):
            if c + 2 < len(bounds):
                accs[c + 1] = _dot(h, w_ref[:, bounds[c + 1]:bounds[c + 2]])
            acc = accs.pop(c)
            if not with_rope:
                o_ref[0, :, bounds[c]:bounds[c + 1]] = acc.astype(o_ref.dtype)
                continue
            for p in range(acc.shape[1] // 128):
                xc = acc[:, p * 128:(p + 1) * 128]
                rot = pltpu.roll(xc, DIFF_D, 1)
                lo = bounds[c] + p * 128
                o_ref[0, :, lo:lo + 128] = (xc * cos + rot * sin).astype(o_ref.dtype)

    if not rope_tiles:
        body(False)
        return
    is_rope = functools.reduce(jnp.logical_or, [j == t for t in rope_tiles])
    pl.when(is_rope)(lambda: body(True))
    pl.when(jnp.logical_not(is_rope))(lambda: body(False))


def _proj(x, mod, w, out_dtype, tn, rope=None, name="proj"):
    bsz, t, _ = x.shape
    cols = w.shape[1]
    tm = min(1024, t)
    in_specs = [pl.BlockSpec((1, tm, D_MODEL), lambda b, i, j: (b, i, 0)),
                pl.BlockSpec((1, 2, D_MODEL), lambda b, i, j: (b, 0, 0)),
                pl.BlockSpec((D_MODEL, tn), lambda b, i, j: (0, j),
                             **({"pipeline_mode": pl.Buffered(1)} if tn == cols else {}))]
    args = [x, mod, w]
    rope_tiles = ()
    if rope is not None:
        cos, sin, rope_tiles = rope
        in_specs += [pl.BlockSpec((tm, 128), lambda b, i, j: (i, 0)),
                     pl.BlockSpec((tm, 128), lambda b, i, j: (i, 0))]
        args += [cos, sin]
    return pl.pallas_call(
        functools.partial(_proj_kernel, rope_tiles=tuple(rope_tiles)),
        grid=(bsz, t // tm, cols // tn),
        in_specs=in_specs,
        out_specs=pl.BlockSpec((1, tm, tn), lambda b, i, j: (b, i, j)),
        out_shape=jax.ShapeDtypeStruct((bsz, t, cols), out_dtype),
        scratch_shapes=[pltpu.VMEM((tm, D_MODEL), BF16)],
        compiler_params=_cparams(("parallel", "parallel", "arbitrary")),
        name=name,
    )(*args)


def _gla_masks(reverse):
    c, s = GLA_CHUNK, GLA_SUB
    i = np.arange(c)[:, None]
    j = np.arange(c)[None, :]
    if not reverse:
        cum = (j <= i)
        ref = (j < (i // s) * s)
    else:
        cum = (j >= i)
        ref = (j >= (i // s + 1) * s)
    return np.concatenate([cum, ref], axis=0).astype(np.float32)


def _gla_kernel(q_ref, k_ref, v_ref, glr_ref, cm_ref, wgk_ref, bgk_ref, *rest, reverse, n_chunks):
    if reverse:
        ofwd_ref, r_ref, ng_ref, o_ref, st_ref, acc_ref = rest
    else:
        o_ref, st_ref = rest
        acc_ref = o_ref.at[0]
    c, s = GLA_CHUNK, GLA_SUB
    nsub = c // s
    dk = GLA_DK

    @pl.when(pl.program_id(2) == 0)
    def _():
        st_ref[...] = jnp.zeros_like(st_ref)

    def lanes(x):
        return jnp.concatenate([x[ci * c:(ci + 1) * c] for ci in range(n_chunks)], axis=1)

    z = _dot(glr_ref[0].astype(BF16), wgk_ref[...]) + bgk_ref[...]
    g = lanes((jnp.minimum(z, 0.0) - jnp.log(1.0 + jnp.exp(-jnp.abs(z)))) * (1.0 / GLA_TAU))
    g_hi = g.astype(BF16)
    g_lo = (g - g_hi.astype(F32)).astype(BF16)
    cm = cm_ref[...].astype(BF16)
    cums = _dot(cm, g_hi) + _dot(cm, g_lo)
    bc, bref = cums[:c], cums[c:]
    q = lanes(q_ref[0]) * (GLA_DK ** -0.5)
    k = lanes(k_ref[0])
    qe = q * jnp.exp(bc - bref)
    qi = (qe * jnp.exp(bref)).astype(BF16)
    qe = qe.astype(BF16)
    bl = bc[0:1] if reverse else bc[c - 1:c]
    dec = jnp.exp(bl)
    kl = (k * jnp.exp(bl - bc)).astype(BF16)
    row = lax.broadcasted_iota(jnp.int32, (c, 1), 0)
    ksub = []
    for si in range(nsub):
        ok = (row >= si * s) if reverse else (row < (si + 1) * s)
        ksub.append(jnp.where(ok, k * jnp.exp(bref[si * s:si * s + 1] - bc), 0.0).astype(BF16))
    rr = lax.broadcasted_iota(jnp.int32, (c, nsub * c), 0)
    cc = lax.broadcasted_iota(jnp.int32, (c, nsub * c), 1)
    causal = (cc % c > rr) if reverse else (cc % c <= rr)
    keep = jnp.logical_and(cc // c == rr // s, causal)
    vs = [v_ref[0, ci * c:(ci + 1) * c, :].astype(BF16) for ci in range(n_chunks)]
    atts = []
    for ci in range(n_chunks):
        ls = slice(ci * dk, (ci + 1) * dk)
        kcat = jnp.concatenate([ks[:, ls] for ks in ksub], axis=0)
        atts.append(_dot_nt(qe[:, ls], kcat))
    atts = [jnp.where(keep, a, 0.0).astype(BF16) for a in atts]
    for ci in range(n_chunks):
        acc_ref[ci * c:(ci + 1) * c, :] = _dot(atts[ci], jnp.concatenate([vs[ci]] * nsub, axis=0))
    uts = [_dot_tn(vs[ci], kl[:, ci * dk:(ci + 1) * dk]) for ci in range(n_chunks)]

    st = st_ref[...]
    order = range(n_chunks - 1, -1, -1) if reverse else range(n_chunks)
    for ci in order:
        ls = slice(ci * dk, (ci + 1) * dk)
        inter = _dot_nt(qi[:, ls], st.astype(BF16))
        acc_ref[ci * c:(ci + 1) * c, :] = acc_ref[ci * c:(ci + 1) * c, :] + inter
        st = st * dec[:, ls] + uts[ci]
    st_ref[...] = st

    if reverse:
        y = ofwd_ref[0] + acc_ref[...]
        y = y * lax.rsqrt(jnp.mean(y * y, axis=-1, keepdims=True) + LN_EPS) * ng_ref[...]
        r = r_ref[0]
        o_ref[0] = (y * (r * _sigmoid(r))).astype(o_ref.dtype)


def _gla_dir(pa, cm, wgk, bgk, reverse, ofwd=None, norm_g=None):
    bsz, t, _ = pa.shape
    tt = min(2048, t)
    nt = t // tt

    def ti(i):
        return (nt - 1 - i) if reverse else i

    in_specs = [pl.BlockSpec((1, tt, GLA_DK), lambda b, h, i: (b, ti(i), h)),
                pl.BlockSpec((1, tt, GLA_DK), lambda b, h, i: (b, ti(i), GLA_HEADS + h)),
                pl.BlockSpec((1, tt, GLA_DV), lambda b, h, i: (b, ti(i), GLA_HEADS + h)),
                pl.BlockSpec((1, tt, 128), lambda b, h, i: (b, ti(i), 3072 // 128)),
                pl.BlockSpec((2 * GLA_CHUNK, GLA_CHUNK), lambda b, h, i: (0, 0)),
                pl.BlockSpec((128, GLA_DK), lambda b, h, i: (0, h)),
                pl.BlockSpec((1, GLA_DK), lambda b, h, i: (0, h))]
    args = [pa, pa, pa, pa, cm, wgk, bgk]
    scratch = [pltpu.VMEM((GLA_DV, GLA_DK), F32)]
    if reverse:
        in_specs += [pl.BlockSpec((1, tt, GLA_DV), lambda b, h, i: (b, ti(i), h)),
                     pl.BlockSpec((1, tt, GLA_DV), lambda b, h, i: (b, ti(i), 2 * GLA_HEADS + h)),
                     pl.BlockSpec((1, GLA_DV), lambda b, h, i: (0, 0))]
        args += [ofwd, pa, norm_g]
        scratch += [pltpu.VMEM((tt, GLA_DV), F32)]
        out_dtype = BF16
    else:
        out_dtype = F32
    return pl.pallas_call(
        functools.partial(_gla_kernel, reverse=reverse, n_chunks=tt // GLA_CHUNK),
        grid=(bsz, GLA_HEADS, nt),
        in_specs=in_specs,
        out_specs=pl.BlockSpec((1, tt, GLA_DV), lambda b, h, i: (b, ti(i), h)),
        out_shape=jax.ShapeDtypeStruct((bsz, t, GLA_HEADS * GLA_DV), out_dtype),
        scratch_shapes=scratch,
        compiler_params=_cparams(("parallel", "parallel", "arbitrary")),
        name="gla_bwd" if reverse else "gla_fwd",
    )(*args)


def _na_bias_table(rpb):
    qc = np.arange(GRID_W)[:, None]
    kc = np.arange(GRID_W)[None, :]
    cs = np.clip(qc - NA_KC // 2, 0, GRID_W - NA_KC)
    allowed = (kc >= cs) & (kc < cs + NA_KC)
    dc = np.clip(kc - qc + NA_KC - 1, 0, 2 * NA_KC - 2)
    onehot = (dc[None] == np.arange(2 * NA_KC - 1)[:, None, None]).astype(np.float32)
    tab = jnp.einsum("hrd,dqk->hrqk", rpb.astype(F32), jnp.asarray(onehot), precision=lax.Precision.HIGHEST)
    tab = jnp.where(jnp.asarray(allowed)[None, None], tab, NEG)
    tab = jnp.stack([tab[:, NA_KR - 1 - d:2 * NA_KR - 1 - d] for d in range(NA_KR)], axis=1)
    tab = tab.transpose(0, 1, 3, 2, 4).reshape(NA_HEADS // 2, 2, NA_KR, GRID_W, NA_KR * GRID_W)
    return tab.transpose(0, 2, 1, 3, 4).reshape(NA_HEADS // 2, NA_KR, 2 * GRID_W, NA_KR * GRID_W)


def _na_kernel(q_ref, k_ref, v_ref, bias_ref, o_ref, *, rows):
    lane = lax.broadcasted_iota(jnp.int32, (GRID_W, 128), 1)
    low = lane < NA_DH
    win = NA_KR * GRID_W
    nr = NA_ROW_UNROLL

    def body(it, carry):
        q0s, scs, vws = [], [], []
        for u in range(nr):
            r = it * nr + u
            rs = jnp.clip(r - NA_KR // 2, 0, rows - NA_KR)
            q0 = pl.multiple_of(r * GRID_W, GRID_W)
            k0 = pl.multiple_of(rs * GRID_W, GRID_W)
            q = q_ref[0, pl.ds(q0, GRID_W), :]
            zq = jnp.zeros_like(q)
            q2 = jnp.concatenate([jnp.where(low, q, zq), jnp.where(low, zq, q)], axis=0)
            scs.append(_dot_nt(q2, k_ref[0, pl.ds(k0, win), :]) + bias_ref[0, r - rs])
            vws.append(v_ref[0, pl.ds(k0, win), :])
            q0s.append(q0)
        ms = [jnp.max(sc, axis=-1, keepdims=True) for sc in scs]
        es = [jnp.exp(sc - m) for sc, m in zip(scs, ms)]
        zs = [jnp.sum(e, axis=-1, keepdims=True) for e in es]
        os_ = [_dot(e.astype(BF16), vw) / z for e, vw, z in zip(es, vws, zs)]
        for q0, o in zip(q0s, os_):
            o_ref[0, pl.ds(q0, GRID_W), :] = jnp.where(low, o[:GRID_W], o[GRID_W:]).astype(o_ref.dtype)
        return carry

    lax.fori_loop(0, rows // nr, body, 0)


def _na(pb, bias):
    bsz, t, _ = pb.shape
    rows = t // GRID_W
    assert rows >= NA_KR and rows % NA_ROW_UNROLL == 0
    npair = NA_HEADS // 2
    return pl.pallas_call(
        functools.partial(_na_kernel, rows=rows),
        grid=(bsz, npair),
        in_specs=[pl.BlockSpec((1, t, 128), lambda b, j: (b, 0, j)),
                  pl.BlockSpec((1, t, 128), lambda b, j: (b, 0, npair + j)),
                  pl.BlockSpec((1, t, 128), lambda b, j: (b, 0, 2 * npair + j)),
                  pl.BlockSpec((1, NA_KR, 2 * GRID_W, NA_KR * GRID_W), lambda b, j: (j, 0, 0, 0))],
        out_specs=pl.BlockSpec((1, t, 128), lambda b, j: (b, 0, j)),
        out_shape=jax.ShapeDtypeStruct((bsz, t, D_MODEL), BF16),
        compiler_params=_cparams(("parallel", "parallel")),
        name="na_attn",
    )(pb, pb, pb, bias)


def _diff_kernel(q_ref, k_ref, v_ref, lam_ref, ng_ref, o_ref, vt_ref, *, lam_init):
    dv = 2 * DIFF_D

    @pl.when(pl.program_id(2) == 0)
    def _():
        vt_ref[0:dv, :] = v_ref[0].astype(F32).T.astype(BF16)
        vt_ref[dv:, :] = jnp.ones((8, vt_ref.shape[1]), BF16)

    lv = lam_ref[...]
    l1 = jnp.sum(lv[0:1] * lv[1:2], axis=-1, keepdims=True)
    l2 = jnp.sum(lv[2:3] * lv[3:4], axis=-1, keepdims=True)
    lam = jnp.exp(l1) - jnp.exp(l2) + lam_init
    q = q_ref[0]
    k = k_ref[0]
    low = lax.broadcasted_iota(jnp.int32, q.shape, 1) % DIFF_D < DIFF_D // 2
    zq = jnp.zeros_like(q)
    t = k.shape[0]
    kc = min(DIFF_KEY_CHUNK, t)
    nc = t // kc
    qms = (jnp.where(low, q, zq), jnp.where(low, zq, q))
    st = [dict(m=None, acc=None) for _ in qms]
    sc = [[None] * nc for _ in qms]
    for p_, qm in enumerate(qms):
        sc[p_][0] = _dot_nt(k[:kc], qm)
    for c in range(nc):
        for p_, qm in enumerate(qms):
            if c + 1 < nc:
                sc[p_][c + 1] = _dot_nt(k[(c + 1) * kc:(c + 2) * kc], qm)
            s_ = sc[p_][c]
            d = st[p_]
            cm = jnp.max(s_, axis=0, keepdims=True)
            vc = vt_ref[:, c * kc:(c + 1) * kc]
            if c == 0:
                d["m"] = cm
                d["acc"] = _dot(vc, jnp.exp2(s_ - cm).astype(BF16))
            else:
                m_new = jnp.maximum(d["m"], cm)
                d["acc"] = jnp.exp2(d["m"] - m_new) * d["acc"] + _dot(vc, jnp.exp2(s_ - m_new).astype(BF16))
                d["m"] = m_new
    outs = [d["acc"][:dv] / d["acc"][dv:dv + 1] for d in st]
    o = outs[0] - lam * outs[1]
    y = o * lax.rsqrt(jnp.mean(o * o, axis=0, keepdims=True) + LN_EPS)
    o_ref[0] = (y.T * (ng_ref[...] * (1.0 - lam_init))).astype(o_ref.dtype)


def _diff(pb, lam_vec, norm_g, lam_init):
    bsz, t, _ = pb.shape
    tq = min(1024, t)
    base = (NA_HEADS * NA_DH * 3) // 128
    return pl.pallas_call(
        functools.partial(_diff_kernel, lam_init=lam_init),
        grid=(bsz, DIFF_HEADS, t // tq),
        in_specs=[pl.BlockSpec((1, tq, 128), lambda b, h, i: (b, i, base + h)),
                  pl.BlockSpec((1, t, 128), lambda b, h, i: (b, 0, base + DIFF_HEADS + h)),
                  pl.BlockSpec((1, t, 128), lambda b, h, i: (b, 0, base + 2 * DIFF_HEADS + h)),
                  pl.BlockSpec((4, DIFF_D), lambda b, h, i: (0, 0)),
                  pl.BlockSpec((1, 2 * DIFF_D), lambda b, h, i: (0, 0))],
        out_specs=pl.BlockSpec((1, tq, 128), lambda b, h, i: (b, i, h)),
        out_shape=jax.ShapeDtypeStruct((bsz, t, D_MODEL), BF16),
        scratch_shapes=[pltpu.VMEM((2 * DIFF_D + 8, t), BF16)],
        compiler_params=_cparams(("parallel", "parallel", "arbitrary")),
        name="diff_attn",
    )(pb, pb, pb, lam_vec, norm_g)


def _merge_kernel(x_ref, mod_ref, ya_ref, yb_ref, yc_ref, wgt_ref, wbr_ref, wo_ref, g_ref, b_ref, o_ref):
    x = x_ref[0]
    h = (x * (1.0 + mod_ref[0, 1:2, :]) + mod_ref[0, 0:1, :]).astype(BF16)
    merged = None
    for n, y_ref in enumerate((ya_ref, yb_ref, yc_ref)):
        gt = _dot(h, wgt_ref[:, n * D_MODEL:(n + 1) * D_MODEL])
        br = _dot(y_ref[0], wbr_ref[n])
        term = _sigmoid(gt) * br
        merged = term if merged is None else merged + term
    out = _dot(merged.astype(BF16), wo_ref[...])
    u = DN_ALPHA * x + mod_ref[0, 2:3, :] * out
    o_ref[0] = _layer_norm(u, g_ref[...], b_ref[...])


def _const_spec(shape, nidx):
    zeros = (0,) * len(shape)
    if nidx == 2:
        return pl.BlockSpec(shape, lambda b, i: zeros, pipeline_mode=pl.Buffered(1))
    return pl.BlockSpec(shape, lambda b, i, e: zeros, pipeline_mode=pl.Buffered(1))


def _merge(x, mod, ya, yb, yc, wgt, wbr, wo, g, b):
    bsz, t, _ = x.shape
    tm = min(512, t)
    tok = pl.BlockSpec((1, tm, D_MODEL), lambda bb, i: (bb, i, 0))
    return pl.pallas_call(
        _merge_kernel,
        grid=(bsz, t // tm),
        in_specs=[tok, pl.BlockSpec((1, 3, D_MODEL), lambda bb, i: (bb, 0, 0)), tok, tok, tok,
                  _const_spec((D_MODEL, 3 * D_MODEL), 2), _const_spec((3, D_MODEL, D_MODEL), 2),
                  _const_spec((D_MODEL, D_MODEL), 2), _const_spec((1, D_MODEL), 2), _const_spec((1, D_MODEL), 2)],
        out_specs=tok,
        out_shape=jax.ShapeDtypeStruct((bsz, t, D_MODEL), F32),
        compiler_params=_cparams(("parallel", "parallel")),
        name="merge_ln",
    )(x, mod, ya, yb, yc, wgt, wbr, wo, g, b)


def _sort_network(n):
    pairs = []
    p = 1
    while p < n:
        k = p
        while k >= 1:
            for jj in range(k % p, n - k, 2 * k):
                for ii in range(min(k, n - jj - k)):
                    if (ii + jj) // (2 * p) == (ii + jj + k) // (2 * p):
                        pairs.append((ii + jj, ii + jj + k))
            k //= 2
        p *= 2
    return pairs


_SORT16 = _sort_network(16)


def _top_rows(x, n):
    rows_in, lanes = x.shape
    groups = [x[8 * g:8 * g + 8] for g in range(rows_in // 8)]
    groups += [jnp.full((8, lanes), NEG, F32)] * (16 - len(groups))
    for i, j in _SORT16:
        if j >= rows_in // 8:
            continue
        hi = jnp.maximum(groups[i], groups[j])
        groups[j] = jnp.minimum(groups[i], groups[j])
        groups[i] = hi
    rows = []
    for r in range(n):
        m = jnp.max(groups[0], axis=0, keepdims=True)
        rows.append(m)
        hit = groups[0] >= m
        for v in range(min(16, n - r - 1)):
            nxt = groups[v + 1] if v + 1 < 16 else NEG
            groups[v] = jnp.where(hit, nxt, groups[v])
    return rows


def _peer_route_kernel(x_ref, mod_ref, wq_ref, sk_ref, h_ref, pj_ref, pb_ref):
    x = x_ref[0]
    h = (x * (1.0 + mod_ref[0, 1:2, :]) + mod_ref[0, 0:1, :]).astype(BF16)
    h_ref[0] = h
    q = _dot(h, wq_ref[...]).astype(BF16)
    kk = PEER_TOPK
    for hd in range(PEER_HEADS):
        qh = q[:, hd * PEER_DKEY:(hd + 1) * PEER_DKEY]
        s1 = _dot_nt(sk_ref[2 * hd], qh)
        s2 = _dot_nt(sk_ref[2 * hd + 1], qh)
        a = _top_rows(s1, kk + 1)
        b = _top_rows(s2, kk + 1)
        b_lo = jnp.concatenate(b[:8], axis=0)
        cand = [a[i] + b_lo for i in range(8)]
        cand.append(a[0] + jnp.concatenate(b[8:16], axis=0))
        cand.append(jnp.concatenate(a[8:16], axis=0) + b[0])
        cand.append(jnp.concatenate([a[0] + b[16], a[16] + b[0]] + [jnp.full_like(a[0], NEG)] * 6, axis=0))
        cand = jnp.concatenate(cand, axis=0)
        top = _top_rows(cand, kk + 1)
        thr = 0.5 * (top[kk - 1] + top[kk])
        mx = a[0] + b[0]
        zs = jnp.sum(jnp.where(cand >= thr, jnp.exp(cand - mx), 0.0), axis=0, keepdims=True)
        t1 = thr - s1
        cnt = jnp.zeros_like(s1)
        rank2 = jnp.zeros_like(s2)
        for c in range(kk):
            cnt = jnp.where(b[c] >= t1, float(c + 1), cnt)
        for c in range(kk + 1):
            rank2 = jnp.where(s2 < b[c], float(c + 1), rank2)
        e1 = jnp.exp(s1 - a[0]) / zs
        for c in range(cnt.shape[1] // 128):
            cs = slice(c * 128, (c + 1) * 128)
            pj_ref[0, hd, 0, :, c] = cnt[:, cs].reshape(PEER_NKEYS // 8, 8, 128)
            pj_ref[0, hd, 1, :, c] = e1[:, cs].reshape(PEER_NKEYS // 8, 8, 128)
        pb_ref[0, hd, 0] = rank2.astype(BF16)
        pb_ref[0, hd, 1] = jnp.exp(s2 - b[0]).astype(BF16)


def _peer_route(x1, mod, wq, sk):
    bsz, t, _ = x1.shape
    tp = min(256, t)
    pspec = pl.BlockSpec((1, PEER_HEADS, 2, PEER_NKEYS, tp), lambda b, i: (b, 0, 0, 0, i))
    jspec = pl.BlockSpec((1, PEER_HEADS, 2, PEER_NKEYS // 8, tp // 128, 8, 128), lambda b, i: (b, 0, 0, 0, i, 0, 0))
    return pl.pallas_call(
        _peer_route_kernel,
        grid=(bsz, t // tp),
        in_specs=[pl.BlockSpec((1, tp, D_MODEL), lambda b, i: (b, i, 0)),
                  pl.BlockSpec((1, 2, D_MODEL), lambda b, i: (b, 0, 0)),
                  _const_spec((D_MODEL, PEER_HEADS * PEER_DKEY), 2),
                  _const_spec((2 * PEER_HEADS, PEER_NKEYS, PEER_DKEY), 2)],
        out_specs=[pl.BlockSpec((1, tp, D_MODEL), lambda b, i: (b, i, 0)), jspec, pspec],
        out_shape=[jax.ShapeDtypeStruct((bsz, t, D_MODEL), BF16),
                   jax.ShapeDtypeStruct((bsz, PEER_HEADS, 2, PEER_NKEYS // 8, t // 128, 8, 128), F32),
                   jax.ShapeDtypeStruct((bsz, PEER_HEADS, 2, PEER_NKEYS, t), BF16)],
        compiler_params=_cparams(("parallel", "parallel")),
        name="peer_route",
    )(x1, mod, wq, sk)


PEER_LANE_CHUNK = 256
PEER_EB = 512


def _bcast_rows(pj_ref, hd, which, r0, cc, lc):
    parts = [pj_ref[0, hd, which, 0, cc * (lc // 128) + c, pl.ds(r0, 16, stride=0), :] for c in range(lc // 128)]
    return jnp.concatenate(parts, axis=1).astype(BF16)


def _peer_dense_kernel(h_ref, u_ref, vt_ref, pj_ref, pb_ref, x_ref, mod_ref, g_ref, b_ref, o_ref,
                       acc_ref, act0_ref, act1_ref, w0_ref, w1_ref, *, eb, nblk, nitems):
    e = pl.program_id(0)
    blk3 = jnp.clip(e - 2, 0, nitems - 1) % nblk
    nk = PEER_NKEYS
    tm = acc_ref.shape[1]
    lc = PEER_LANE_CHUNK
    assert tm % lc == 0 and eb == 4 * nk
    zero = jnp.zeros((), BF16)

    @pl.when(e == 0)
    def _():
        act1_ref[...] = jnp.zeros_like(act1_ref)
        w0_ref[...] = jnp.zeros_like(w0_ref)
        w1_ref[...] = jnp.zeros_like(w1_ref)

    @pl.when(blk3 == 0)
    def _():
        acc_ref[...] = jnp.zeros_like(acc_ref)

    def step(par, act_in, act_out, w_in, w_out):
        nj = eb // nk
        rc = nk // 2
        for cc in range(tm // lc):
            ls = slice(cc * lc, (cc + 1) * lc)
            for ri in range(2):
                rr = slice(ri * rc, (ri + 1) * rc)
                for jp in range(nj // 2):
                    if ri == 0:
                        us = slice(jp * (eb // 2), (jp + 1) * (eb // 2))
                        act_out[us, ls] = _dot_nt(u_ref[us, :], h_ref[0, ls, :])
                    else:
                        ds_ = slice(jp * (D_MODEL // 2), (jp + 1) * (D_MODEL // 2))
                        acc_ref[ds_, ls] += _dot(vt_ref[0, ds_, :], w_in[:, ls])
                    w = [None, None]
                    for hd in range(PEER_HEADS):
                        rank = pb_ref[0, hd, 0, rr, ls]
                        e2 = pb_ref[0, hd, 1, rr, ls]
                        for k in range(2):
                            r0 = (1 - par) * nj + jp * 2 + k
                            cnt = _bcast_rows(pj_ref, hd, 0, r0, cc, lc)
                            e1 = _bcast_rows(pj_ref, hd, 1, r0, cc, lc)
                            cnt = jnp.concatenate([cnt] * (rc // 16), axis=0)
                            e1 = jnp.concatenate([e1] * (rc // 16), axis=0)
                            term = jnp.where(rank < cnt, e2, zero) * e1
                            w[k] = term if w[k] is None else w[k] + term
                    for k in range(2):
                        jj = jp * 2 + k
                        rs = slice(jj * nk + ri * rc, jj * nk + (ri + 1) * rc)
                        a = act_in[rs, ls]
                        gelu = 0.5 * a * (1.0 + lax.erf(a * (2.0 ** -0.5)))
                        w_out[rs, ls] = w[k] * gelu.astype(BF16)

    @pl.when(e % 2 == 0)
    def _():
        step(0, act1_ref, act0_ref, w0_ref, w1_ref)

    @pl.when(e % 2 == 1)
    def _():
        step(1, act0_ref, act1_ref, w1_ref, w0_ref)

    @pl.when(blk3 == nblk - 1)
    def _():
        f = acc_ref[...].T
        x = x_ref[0]
        u = DN_ALPHA * x + mod_ref[0, 0:1, :] * f
        o_ref[0] = _layer_norm(u, g_ref[...], b_ref[...])


def _peer_dense(h2, u_tab, vt_tab, pj, pb, x1, mod, g, b):
    bsz, t, _ = x1.shape
    tm = min(1024, t)
    eb = PEER_EB
    nblk = PEER_N // eb
    ntile = t // tm
    nitems = bsz * ntile * nblk
    assert 2 * (eb // PEER_NKEYS) == 8 and nblk % 2 == 0

    def item(e, lag):
        it = jnp.clip(e - lag, 0, nitems - 1)
        tile = it // nblk
        return tile // ntile, tile % ntile, it % nblk

    def tok(lag):
        return pl.BlockSpec((1, tm, D_MODEL), lambda e: item(e, lag)[:2] + (0,))

    def const(shape):
        return pl.BlockSpec(shape, lambda e: (0,) * len(shape), pipeline_mode=pl.Buffered(1))

    return pl.pallas_call(
        functools.partial(_peer_dense_kernel, eb=eb, nblk=nblk, nitems=nitems),
        grid=(nitems + 2,),
        in_specs=[tok(0),
                  pl.BlockSpec((eb, D_MODEL), lambda e: (item(e, 0)[2], 0)),
                  pl.BlockSpec((1, D_MODEL, eb), lambda e: (item(e, 2)[2], 0, 0)),
                  pl.BlockSpec((1, PEER_HEADS, 2, 1, tm // 128, 8, 128),
                               lambda e: (item(e, 1)[0], 0, 0, item(e, 1)[2] // 2, item(e, 1)[1], 0, 0)),
                  pl.BlockSpec((1, PEER_HEADS, 2, PEER_NKEYS, tm), lambda e: (item(e, 1)[0], 0, 0, 0, item(e, 1)[1])),
                  tok(2),
                  pl.BlockSpec((1, 1, D_MODEL), lambda e: (item(e, 2)[0], 0, 0)),
                  const((1, D_MODEL)), const((1, D_MODEL))],
        out_specs=tok(2),
        out_shape=jax.ShapeDtypeStruct((bsz, t, D_MODEL), F32),
        scratch_shapes=[pltpu.VMEM((D_MODEL, tm), F32),
                        pltpu.VMEM((eb, tm), F32), pltpu.VMEM((eb, tm), F32),
                        pltpu.VMEM((eb, tm), BF16), pltpu.VMEM((eb, tm), BF16)],
        compiler_params=_cparams(("arbitrary",)),
        name="peer_dense",
    )(h2, u_tab, vt_tab, pj, pb, x1, mod, g, b)


def _rope_tables(t):
    half = DIFF_D // 2
    inv = jnp.exp(-math.log(ROPE_THETA) * jnp.arange(half, dtype=F32) / half)
    ang = jnp.arange(t, dtype=F32)[:, None] * inv[None, :]
    cos, sin = jnp.cos(ang), jnp.sin(ang)
    cos128 = jnp.tile(cos, (1, 4))
    sin128 = jnp.concatenate([-sin, -sin, sin, sin], axis=1)
    return cos128, sin128


def _layer_params(l, w_in, gla_w_gk, gla_b_gk, gla_norm_g, na_rpb, diff_lambda, diff_norm_g, w_br, w_o,
                  ln1_g, ln1_b, peer_wq, peer_subkeys, peer_u, peer_v, ln2_g, ln2_b):
    w = w_in[l]
    w_gla = jnp.pad(w[:, :3104], ((0, 0), (0, GLA_COLS - 3104))).astype(BF16)
    w_att = w[:, 3104:3104 + ATT_COLS]
    qscale = np.ones((ATT_COLS,), np.float32)
    qscale[0:1024] = NA_DH ** -0.5
    qscale[3072:4096] = DIFF_D ** -0.5 * math.log2(math.e)
    half = DIFF_D // 2
    lane_src = np.concatenate([np.arange(0, half), np.arange(DIFF_D, DIFF_D + half),
                               np.arange(half, DIFF_D), np.arange(DIFF_D + half, 2 * DIFF_D)])
    cols = np.arange(ATT_COLS)
    for base in (3072, 4096):
        for hd in range(DIFF_HEADS):
            cols[base + hd * 128:base + (hd + 1) * 128] = base + hd * 128 + lane_src
    w_att = (w_att * qscale)[:, cols].astype(BF16)
    w_gt = w[:, 3104 + ATT_COLS:].astype(BF16)
    wgk = jnp.zeros((2, 128, GLA_HEADS * GLA_DK), F32)
    wgk = wgk.at[0, :GLA_RANK].set(gla_w_gk[l, 0]).at[1, GLA_RANK:2 * GLA_RANK].set(gla_w_gk[l, 1]).astype(BF16)
    sk = peer_subkeys[l]
    half = PEER_DKEY // 2
    skp = jnp.zeros((PEER_HEADS, 2, PEER_NKEYS, PEER_DKEY), F32)
    skp = skp.at[:, 0, :, :half].set(sk[:, 0]).at[:, 1, :, half:].set(sk[:, 1])
    return dict(
        w_gla=w_gla, w_att=w_att, w_gt=w_gt, wgk=wgk, bgk=gla_b_gk[l],
        gla_g=gla_norm_g[l][None], na_bias=_na_bias_table(na_rpb[l]),
        lam=diff_lambda[l], diff_g=diff_norm_g[l][None],
        w_br=w_br[l].astype(BF16), w_o=w_o[l].astype(BF16),
        ln1_g=ln1_g[l][None], ln1_b=ln1_b[l][None],
        wq=peer_wq[l].astype(BF16), sk=skp.reshape(2 * PEER_HEADS, PEER_NKEYS, PEER_DKEY).astype(BF16),
        u=peer_u[l].astype(BF16),
        vt=peer_v[l].astype(BF16).reshape(PEER_N // PEER_EB, PEER_EB, D_MODEL).transpose(0, 2, 1),
        ln2_g=ln2_g[l][None], ln2_b=ln2_b[l][None],
        lam_init=0.8 - 0.6 * math.exp(-0.3 * l),
    )


def _layer(x, mod, p, rope, cms):
    pa = _proj(x, mod[:, 0:2], p["w_gla"], F32, tn=GLA_COLS, name="proj_gla")
    pb = _proj(x, mod[:, 0:2], p["w_att"], BF16, tn=1024, rope=rope + ((3, 4),), name="proj_att")
    ofwd = _gla_dir(pa, cms[0], p["wgk"][0], p["bgk"][0:1], reverse=False)
    ya = _gla_dir(pa, cms[1], p["wgk"][1], p["bgk"][1:2], reverse=True, ofwd=ofwd, norm_g=p["gla_g"])
    yb = _na(pb, p["na_bias"])
    yc = _diff(pb, p["lam"], p["diff_g"], p["lam_init"])
    x1 = _merge(x, mod[:, 0:3], ya, yb, yc, p["w_gt"], p["w_br"], p["w_o"], p["ln1_g"], p["ln1_b"])
    h2, pj, pb2 = _peer_route(x1, mod[:, 3:5], p["wq"], p["sk"])
    return _peer_dense(h2, p["u"], p["vt"], pj, pb2, x1, mod[:, 5:6], p["ln2_g"], p["ln2_b"])


def kernel(x_prompt, x_sample, c_prompt, c_sample, w_ada, b_ada, w_in, gla_w_gk, gla_b_gk, gla_norm_g, na_rpb, diff_lambda, diff_norm_g, w_br, w_o, ln1_g, ln1_b, peer_wq, peer_subkeys, peer_u, peer_v, ln2_g, ln2_b):
    nb = x_prompt.shape[0]
    c_all = jnp.concatenate([c_prompt, c_sample], axis=0)
    cms = (jnp.asarray(_gla_masks(False)), jnp.asarray(_gla_masks(True)))
    ropes = {x.shape[1]: _rope_tables(x.shape[1]) for x in (x_prompt, x_sample)}
    xs = [x_prompt, x_sample]
    for l in range(DEPTH):
        p = _layer_params(l, w_in, gla_w_gk, gla_b_gk, gla_norm_g, na_rpb, diff_lambda, diff_norm_g, w_br, w_o,
                          ln1_g, ln1_b, peer_wq, peer_subkeys, peer_u, peer_v, ln2_g, ln2_b)
        mod_all = _ada(c_all, w_ada[l].astype(BF16), b_ada[l][None]).reshape(c_all.shape[0], 6, D_MODEL)
        mods = [mod_all[:nb], mod_all[nb:]]
        xs = [_layer(x, m, p, ropes[x.shape[1]], cms) for x, m in zip(xs, mods)]
    return (xs[0], xs[1])
```

```python
import functools
import math

import numpy as np
import jax
import jax.numpy as jnp
from jax import lax
from jax.experimental import pallas as pl
from jax.experimental.pallas import tpu as pltpu

F32 = jnp.float32
BF16 = jnp.bfloat16

D_MODEL = 1024
DEPTH = 2
GRID_W = 64
GLA_HEADS, GLA_DK, GLA_DV, GLA_RANK, GLA_TAU, GLA_CHUNK = 4, 128, 256, 16, 16.0, 64
GLA_SUB = 16
NA_HEADS, NA_DH, NA_KR, NA_KC = 16, 64, 8, 16
NA_ROW_UNROLL = 16
DIFF_HEADS, DIFF_D = 8, 64
DIFF_KEY_CHUNK = 512
ROPE_THETA = 10000.0
PEER_HEADS, PEER_NKEYS, PEER_DKEY, PEER_TOPK = 8, 128, 128, 16
PEER_N = PEER_NKEYS * PEER_NKEYS
DN_ALPHA = (2 * DEPTH) ** 0.25
LN_EPS = 1e-5
NEG = -1e30

GLA_COLS = 3328
ATT_COLS = 6144
VMEM_LIMIT = 56 * 1024 * 1024


def _cparams(sem):
    return pltpu.CompilerParams(dimension_semantics=sem, vmem_limit_bytes=VMEM_LIMIT)


def _dot(a, b):
    return jnp.dot(a, b, preferred_element_type=F32)


def _dot_nt(a, b):
    return lax.dot_general(a, b, (((1,), (1,)), ((), ())), preferred_element_type=F32)


def _dot_tn(a, b):
    return lax.dot_general(a, b, (((0,), (0,)), ((), ())), preferred_element_type=F32)


def _sigmoid(x):
    return 1.0 / (1.0 + jnp.exp(-x))


def _layer_norm(u, g, b):
    mu = jnp.mean(u, axis=-1, keepdims=True)
    d = u - mu
    var = jnp.mean(d * d, axis=-1, keepdims=True)
    return d * lax.rsqrt(var + LN_EPS) * g + b


def _ada_kernel(c_ref, w_ref, b_ref, o_ref):
    c = c_ref[...]
    s = (c * _sigmoid(c)).astype(BF16)
    o_ref[...] = _dot(s, w_ref[...]) + b_ref[...]


def _ada(c, w, b):
    n = c.shape[0]
    cols = w.shape[1]
    return pl.pallas_call(
        _ada_kernel,
        grid=(cols // D_MODEL,),
        in_specs=[pl.BlockSpec((n, D_MODEL), lambda j: (0, 0)),
                  pl.BlockSpec((D_MODEL, D_MODEL), lambda j: (0, j)),
                  pl.BlockSpec((1, D_MODEL), lambda j: (0, j))],
        out_specs=pl.BlockSpec((n, D_MODEL), lambda j: (0, j)),
        out_shape=jax.ShapeDtypeStruct((n, cols), F32),
        compiler_params=_cparams(("arbitrary",)),
        name="ada_mod",
    )(c, w, b)


PROJ_COL_CHUNK = 256


def _proj_kernel(x_ref, mod_ref, w_ref, *rest, rope_tiles):
    if rope_tiles:
        cos_ref, sin_ref, o_ref, h_ref = rest
    else:
        o_ref, h_ref = rest
    j = pl.program_id(2)

    @pl.when(j == 0)
    def _():
        x = x_ref[0]
        h_ref[...] = (x * (1.0 + mod_ref[0, 1:2, :]) + mod_ref[0, 0:1, :]).astype(BF16)

    tn = w_ref.shape[1]
    bounds = list(range(0, tn, PROJ_COL_CHUNK)) + [tn]

    def body(with_rope):
        h = h_ref[...]
        if with_rope:
            cos = cos_ref[...]
            sin = sin_ref[...]
        accs = {0: _dot(h, w_ref[:, bounds[0]:bounds[1]])}
        for c in range(len(bounds) - 1):
            if c + 2 < len(bounds):
                accs[c + 1] = _dot(h, w_ref[:, bounds[c + 1]:bounds[c + 2]])
            acc = accs.pop(c)
            if not with_rope:
                o_ref[0, :, bounds[c]:bounds[c + 1]] = acc.astype(o_ref.dtype)
                continue
            for p in range(acc.shape[1] // 128):
                xc = acc[:, p * 128:(p + 1) * 128]
                rot = pltpu.roll(xc, DIFF_D, 1)
                lo = bounds[c] + p * 128
                o_ref[0, :, lo:lo + 128] = (xc * cos + rot * sin).astype(o_ref.dtype)

    if not rope_tiles:
        body(False)
        return
    is_rope = functools.reduce(jnp.logical_or, [j == t for t in rope_tiles])
    pl.when(is_rope)(lambda: body(True))
    pl.when(jnp.logical_not(is_rope))(lambda: body(False))


def _proj(x, mod, w, out_dtype, tn, rope=None, name="proj"):
    bsz, t, _ = x.shape
    cols = w.shape[1]
    tm = min(1024, t)
    in_specs = [pl.BlockSpec((1, tm, D_MODEL), lambda b, i, j: (b, i, 0)),
                pl.BlockSpec((1, 2, D_MODEL), lambda b, i, j: (b, 0, 0)),
                pl.BlockSpec((D_MODEL, tn), lambda b, i, j: (0, j),
                             **({"pipeline_mode": pl.Buffered(1)} if tn == cols else {}))]
    args = [x, mod, w]
    rope_tiles = ()
    if rope is not None:
        cos, sin, rope_tiles = rope
        in_specs += [pl.BlockSpec((tm, 128), lambda b, i, j: (i, 0)),
                     pl.BlockSpec((tm, 128), lambda b, i, j: (i, 0))]
        args += [cos, sin]
    return pl.pallas_call(
        functools.partial(_proj_kernel, rope_tiles=tuple(rope_tiles)),
        grid=(bsz, t // tm, cols // tn),
        in_specs=in_specs,
        out_specs=pl.BlockSpec((1, tm, tn), lambda b, i, j: (b, i, j)),
        out_shape=jax.ShapeDtypeStruct((bsz, t, cols), out_dtype),
        scratch_shapes=[pltpu.VMEM((tm, D_MODEL), BF16)],
        compiler_params=_cparams(("parallel", "parallel", "arbitrary")),
        name=name,
    )(*args)


def _gla_masks(reverse):
    c, s = GLA_CHUNK, GLA_SUB
    i = np.arange(c)[:, None]
    j = np.arange(c)[None, :]
    if not reverse:
        cum = (j <= i)
        ref = (j < (i // s) * s)
    else:
        cum = (j >= i)
        ref = (j >= (i // s + 1) * s)
    return np.concatenate([cum, ref], axis=0).astype(np.float32)


def _gla_kernel(q_ref, k_ref, v_ref, glr_ref, cm_ref, wgk_ref, bgk_ref, *rest, reverse, n_chunks):
    if reverse:
        ofwd_ref, r_ref, ng_ref, o_ref, st_ref, acc_ref = rest
    else:
        o_ref, st_ref = rest
        acc_ref = o_ref.at[0]
    c, s = GLA_CHUNK, GLA_SUB
    nsub = c // s
    dk = GLA_DK

    @pl.when(pl.program_id(2) == 0)
    def _():
        st_ref[...] = jnp.zeros_like(st_ref)

    def lanes(x):
        return jnp.concatenate([x[ci * c:(ci + 1) * c] for ci in range(n_chunks)], axis=1)

    z = _dot(glr_ref[0].astype(BF16), wgk_ref[...]) + bgk_ref[...]
    g = lanes((jnp.minimum(z, 0.0) - jnp.log(1.0 + jnp.exp(-jnp.abs(z)))) * (1.0 / GLA_TAU))
    g_hi = g.astype(BF16)
    g_lo = (g - g_hi.astype(F32)).astype(BF16)
    cm = cm_ref[...].astype(BF16)
    cums = _dot(cm, g_hi) + _dot(cm, g_lo)
    bc, bref = cums[:c], cums[c:]
    q = lanes(q_ref[0]) * (GLA_DK ** -0.5)
    k = lanes(k_ref[0])
    qe = q * jnp.exp(bc - bref)
    qi = (qe * jnp.exp(bref)).astype(BF16)
    qe = qe.astype(BF16)
    bl = bc[0:1] if reverse else bc[c - 1:c]
    dec = jnp.exp(bl)
    kl = (k * jnp.exp(bl - bc)).astype(BF16)
    row = lax.broadcasted_iota(jnp.int32, (c, 1), 0)
    ksub = []
    for si in range(nsub):
        ok = (row >= si * s) if reverse else (row < (si + 1) * s)
        ksub.append(jnp.where(ok, k * jnp.exp(bref[si * s:si * s + 1] - bc), 0.0).astype(BF16))
    rr = lax.broadcasted_iota(jnp.int32, (c, nsub * c), 0)
    cc = lax.broadcasted_iota(jnp.int32, (c, nsub * c), 1)
    causal = (cc % c > rr) if reverse else (cc % c <= rr)
    keep = jnp.logical_and(cc // c == rr // s, causal)
    vs = [v_ref[0, ci * c:(ci + 1) * c, :].astype(BF16) for ci in range(n_chunks)]
    atts = []
    for ci in range(n_chunks):
        ls = slice(ci * dk, (ci + 1) * dk)
        kcat = jnp.concatenate([ks[:, ls] for ks in ksub], axis=0)
        atts.append(_dot_nt(qe[:, ls], kcat))
    atts = [jnp.where(keep, a, 0.0).astype(BF16) for a in atts]
    for ci in range(n_chunks):
        acc_ref[ci * c:(ci + 1) * c, :] = _dot(atts[ci], jnp.concatenate([vs[ci]] * nsub, axis=0))
    uts = [_dot_tn(vs[ci], kl[:, ci * dk:(ci + 1) * dk]) for ci in range(n_chunks)]

    st = st_ref[...]
    order = range(n_chunks - 1, -1, -1) if reverse else range(n_chunks)
    for ci in order:
        ls = slice(ci * dk, (ci + 1) * dk)
        inter = _dot_nt(qi[:, ls], st.astype(BF16))
        acc_ref[ci * c:(ci + 1) * c, :] = acc_ref[ci * c:(ci + 1) * c, :] + inter
        st = st * dec[:, ls] + uts[ci]
    st_ref[...] = st

    if reverse:
        y = ofwd_ref[0] + acc_ref[...]
        y = y * lax.rsqrt(jnp.mean(y * y, axis=-1, keepdims=True) + LN_EPS) * ng_ref[...]
        r = r_ref[0]
        o_ref[0] = (y * (r * _sigmoid(r))).astype(o_ref.dtype)


def _gla_dir(pa, cm, wgk, bgk, reverse, ofwd=None, norm_g=None):
    bsz, t, _ = pa.shape
    tt = min(2048, t)
    nt = t // tt

    def ti(i):
        return (nt - 1 - i) if reverse else i

    in_specs = [pl.BlockSpec((1, tt, GLA_DK), lambda b, h, i: (b, ti(i), h)),
                pl.BlockSpec((1, tt, GLA_DK), lambda b, h, i: (b, ti(i), GLA_HEADS + h)),
                pl.BlockSpec((1, tt, GLA_DV), lambda b, h, i: (b, ti(i), GLA_HEADS + h)),
                pl.BlockSpec((1, tt, 128), lambda b, h, i: (b, ti(i), 3072 // 128)),
                pl.BlockSpec((2 * GLA_CHUNK, GLA_CHUNK), lambda b, h, i: (0, 0)),
                pl.BlockSpec((128, GLA_DK), lambda b, h, i: (0, h)),
                pl.BlockSpec((1, GLA_DK), lambda b, h, i: (0, h))]
    args = [pa, pa, pa, pa, cm, wgk, bgk]
    scratch = [pltpu.VMEM((GLA_DV, GLA_DK), F32)]
    if reverse:
        in_specs += [pl.BlockSpec((1, tt, GLA_DV), lambda b, h, i: (b, ti(i), h)),
                     pl.BlockSpec((1, tt, GLA_DV), lambda b, h, i: (b, ti(i), 2 * GLA_HEADS + h)),
                     pl.BlockSpec((1, GLA_DV), lambda b, h, i: (0, 0))]
        args += [ofwd, pa, norm_g]
        scratch += [pltpu.VMEM((tt, GLA_DV), F32)]
        out_dtype = BF16
    else:
        out_dtype = F32
    return pl.pallas_call(
        functools.partial(_gla_kernel, reverse=reverse, n_chunks=tt // GLA_CHUNK),
        grid=(bsz, GLA_HEADS, nt),
        in_specs=in_specs,
        out_specs=pl.BlockSpec((1, tt, GLA_DV), lambda b, h, i: (b, ti(i), h)),
        out_shape=jax.ShapeDtypeStruct((bsz, t, GLA_HEADS * GLA_DV), out_dtype),
        scratch_shapes=scratch,
        compiler_params=_cparams(("parallel", "parallel", "arbitrary")),
        name="gla_bwd" if reverse else "gla_fwd",
    )(*args)


def _na_bias_table(rpb):
    qc = np.arange(GRID_W)[:, None]
    kc = np.arange(GRID_W)[None, :]
    cs = np.clip(qc - NA_KC // 2, 0, GRID_W - NA_KC)
    allowed = (kc >= cs) & (kc < cs + NA_KC)
    dc = np.clip(kc - qc + NA_KC - 1, 0, 2 * NA_KC - 2)
    onehot = (dc[None] == np.arange(2 * NA_KC - 1)[:, None, None]).astype(np.float32)
    tab = jnp.einsum("hrd,dqk->hrqk", rpb.astype(F32), jnp.asarray(onehot), precision=lax.Precision.HIGHEST)
    tab = jnp.where(jnp.asarray(allowed)[None, None], tab, NEG)
    tab = jnp.stack([tab[:, NA_KR - 1 - d:2 * NA_KR - 1 - d] for d in range(NA_KR)], axis=1)
    tab = tab.transpose(0, 1, 3, 2, 4).reshape(NA_HEADS // 2, 2, NA_KR, GRID_W, NA_KR * GRID_W)
    return tab.transpose(0, 2, 1, 3, 4).reshape(NA_HEADS // 2, NA_KR, 2 * GRID_W, NA_KR * GRID_W)


def _na_kernel(q_ref, k_ref, v_ref, bias_ref, o_ref, *, rows):
    lane = lax.broadcasted_iota(jnp.int32, (GRID_W, 128), 1)
    low = lane < NA_DH
    win = NA_KR * GRID_W
    nr = NA_ROW_UNROLL

    def body(it, carry):
        q0s, scs, vws = [], [], []
        for u in range(nr):
            r = it * nr + u
            rs = jnp.clip(r - NA_KR // 2, 0, rows - NA_KR)
            q0 = pl.multiple_of(r * GRID_W, GRID_W)
            k0 = pl.multiple_of(rs * GRID_W, GRID_W)
            q = q_ref[0, pl.ds(q0, GRID_W), :]
            zq = jnp.zeros_like(q)
            q2 = jnp.concatenate([jnp.where(low, q, zq), jnp.where(low, zq, q)], axis=0)
            scs.append(_dot_nt(q2, k_ref[0, pl.ds(k0, win), :]) + bias_ref[0, r - rs])
            vws.append(v_ref[0, pl.ds(k0, win), :])
            q0s.append(q0)
        ms = [jnp.max(sc, axis=-1, keepdims=True) for sc in scs]
        es = [jnp.exp(sc - m) for sc, m in zip(scs, ms)]
        zs = [jnp.sum(e, axis=-1, keepdims=True) for e in es]
        os_ = [_dot(e.astype(BF16), vw) / z for e, vw, z in zip(es, vws, zs)]
        for q0, o in zip(q0s, os_):
            o_ref[0, pl.ds(q0, GRID_W), :] = jnp.where(low, o[:GRID_W], o[GRID_W:]).astype(o_ref.dtype)
        return carry

    lax.fori_loop(0, rows // nr, body, 0)


def _na(pb, bias):
    bsz, t, _ = pb.shape
    rows = t // GRID_W
    assert rows >= NA_KR and rows % NA_ROW_UNROLL == 0
    npair = NA_HEADS // 2
    return pl.pallas_call(
        functools.partial(_na_kernel, rows=rows),
        grid=(bsz, npair),
        in_specs=[pl.BlockSpec((1, t, 128), lambda b, j: (b, 0, j)),
                  pl.BlockSpec((1, t, 128), lambda b, j: (b, 0, npair + j)),
                  pl.BlockSpec((1, t, 128), lambda b, j: (b, 0, 2 * npair + j)),
                  pl.BlockSpec((1, NA_KR, 2 * GRID_W, NA_KR * GRID_W), lambda b, j: (j, 0, 0, 0))],
        out_specs=pl.BlockSpec((1, t, 128), lambda b, j: (b, 0, j)),
        out_shape=jax.ShapeDtypeStruct((bsz, t, D_MODEL), BF16),
        compiler_params=_cparams(("parallel", "parallel")),
        name="na_attn",
    )(pb, pb, pb, bias)


def _diff_kernel(q_ref, k_ref, v_ref, lam_ref, ng_ref, o_ref, vt_ref, *, lam_init):
    dv = 2 * DIFF_D

    @pl.when(pl.program_id(2) == 0)
    def _():
        vt_ref[0:dv, :] = v_ref[0].astype(F32).T.astype(BF16)
        vt_ref[dv:, :] = jnp.ones((8, vt_ref.shape[1]), BF16)

    lv = lam_ref[...]
    l1 = jnp.sum(lv[0:1] * lv[1:2], axis=-1, keepdims=True)
    l2 = jnp.sum(lv[2:3] * lv[3:4], axis=-1, keepdims=True)
    lam = jnp.exp(l1) - jnp.exp(l2) + lam_init
    q = q_ref[0]
    k = k_ref[0]
    low = lax.broadcasted_iota(jnp.int32, q.shape, 1) % DIFF_D < DIFF_D // 2
    zq = jnp.zeros_like(q)
    t = k.shape[0]
    kc = min(DIFF_KEY_CHUNK, t)
    nc = t // kc
    qms = (jnp.where(low, q, zq), jnp.where(low, zq, q))
    st = [dict(m=None, acc=None) for _ in qms]
    sc = [[None] * nc for _ in qms]
    for p_, qm in enumerate(qms):
        sc[p_][0] = _dot_nt(k[:kc], qm)
    for c in range(nc):
        for p_, qm in enumerate(qms):
            if c + 1 < nc:
                sc[p_][c + 1] = _dot_nt(k[(c + 1) * kc:(c + 2) * kc], qm)
            s_ = sc[p_][c]
            d = st[p_]
            cm = jnp.max(s_, axis=0, keepdims=True)
            vc = vt_ref[:, c * kc:(c + 1) * kc]
            if c == 0:
                d["m"] = cm
                d["acc"] = _dot(vc, jnp.exp2(s_ - cm).astype(BF16))
            else:
                m_new = jnp.maximum(d["m"], cm)
                d["acc"] = jnp.exp2(d["m"] - m_new) * d["acc"] + _dot(vc, jnp.exp2(s_ - m_new).astype(BF16))
                d["m"] = m_new
    outs = [d["acc"][:dv] / d["acc"][dv:dv + 1] for d in st]
    o = outs[0] - lam * outs[1]
    y = o * lax.rsqrt(jnp.mean(o * o, axis=0, keepdims=True) + LN_EPS)
    o_ref[0] = (y.T * (ng_ref[...] * (1.0 - lam_init))).astype(o_ref.dtype)


def _diff(pb, lam_vec, norm_g, lam_init):
    bsz, t, _ = pb.shape
    tq = min(1024, t)
    base = (NA_HEADS * NA_DH * 3) // 128
    return pl.pallas_call(
        functools.partial(_diff_kernel, lam_init=lam_init),
        grid=(bsz, DIFF_HEADS, t // tq),
        in_specs=[pl.BlockSpec((1, tq, 128), lambda b, h, i: (b, i, base + h)),
                  pl.BlockSpec((1, t, 128), lambda b, h, i: (b, 0, base + DIFF_HEADS + h)),
                  pl.BlockSpec((1, t, 128), lambda b, h, i: (b, 0, base + 2 * DIFF_HEADS + h)),
                  pl.BlockSpec((4, DIFF_D), lambda b, h, i: (0, 0)),
                  pl.BlockSpec((1, 2 * DIFF_D), lambda b, h, i: (0, 0))],
        out_specs=pl.BlockSpec((1, tq, 128), lambda b, h, i: (b, i, h)),
        out_shape=jax.ShapeDtypeStruct((bsz, t, D_MODEL), BF16),
        scratch_shapes=[pltpu.VMEM((2 * DIFF_D + 8, t), BF16)],
        compiler_params=_cparams(("parallel", "parallel", "arbitrary")),
        name="diff_attn",
    )(pb, pb, pb, lam_vec, norm_g)


def _merge_kernel(x_ref, mod_ref, ya_ref, yb_ref, yc_ref, wgt_ref, wbr_ref, wo_ref, g_ref, b_ref, o_ref):
    x = x_ref[0]
    h = (x * (1.0 + mod_ref[0, 1:2, :]) + mod_ref[0, 0:1, :]).astype(BF16)
    merged = None
    for n, y_ref in enumerate((ya_ref, yb_ref, yc_ref)):
        gt = _dot(h, wgt_ref[:, n * D_MODEL:(n + 1) * D_MODEL])
        br = _dot(y_ref[0], wbr_ref[n])
        term = _sigmoid(gt) * br
        merged = term if merged is None else merged + term
    out = _dot(merged.astype(BF16), wo_ref[...])
    u = DN_ALPHA * x + mod_ref[0, 2:3, :] * out
    o_ref[0] = _layer_norm(u, g_ref[...], b_ref[...])


def _const_spec(shape, nidx):
    zeros = (0,) * len(shape)
    if nidx == 2:
        return pl.BlockSpec(shape, lambda b, i: zeros, pipeline_mode=pl.Buffered(1))
    return pl.BlockSpec(shape, lambda b, i, e: zeros, pipeline_mode=pl.Buffered(1))


def _merge(x, mod, ya, yb, yc, wgt, wbr, wo, g, b):
    bsz, t, _ = x.shape
    tm = min(512, t)
    tok = pl.BlockSpec((1, tm, D_MODEL), lambda bb, i: (bb, i, 0))
    return pl.pallas_call(
        _merge_kernel,
        grid=(bsz, t // tm),
        in_specs=[tok, pl.BlockSpec((1, 3, D_MODEL), lambda bb, i: (bb, 0, 0)), tok, tok, tok,
                  _const_spec((D_MODEL, 3 * D_MODEL), 2), _const_spec((3, D_MODEL, D_MODEL), 2),
                  _const_spec((D_MODEL, D_MODEL), 2), _const_spec((1, D_MODEL), 2), _const_spec((1, D_MODEL), 2)],
        out_specs=tok,
        out_shape=jax.ShapeDtypeStruct((bsz, t, D_MODEL), F32),
        compiler_params=_cparams(("parallel", "parallel")),
        name="merge_ln",
    )(x, mod, ya, yb, yc, wgt, wbr, wo, g, b)


def _sort_network(n):
    pairs = []
    p = 1
    while p < n:
        k = p
        while k >= 1:
            for jj in range(k % p, n - k, 2 * k):
                for ii in range(min(k, n - jj - k)):
                    if (ii + jj) // (2 * p) == (ii + jj + k) // (2 * p):
                        pairs.append((ii + jj, ii + jj + k))
            k //= 2
        p *= 2
    return pairs


_SORT16 = _sort_network(16)


def _top_rows(x, n):
    rows_in, lanes = x.shape
    groups = [x[8 * g:8 * g + 8] for g in range(rows_in // 8)]
    groups += [jnp.full((8, lanes), NEG, F32)] * (16 - len(groups))
    for i, j in _SORT16:
        if j >= rows_in // 8:
            continue
        hi = jnp.maximum(groups[i], groups[j])
        groups[j] = jnp.minimum(groups[i], groups[j])
        groups[i] = hi
    rows = []
    for r in range(n):
        m = jnp.max(groups[0], axis=0, keepdims=True)
        rows.append(m)
        hit = groups[0] >= m
        for v in range(min(16, n - r - 1)):
            nxt = groups[v + 1] if v + 1 < 16 else NEG
            groups[v] = jnp.where(hit, nxt, groups[v])
    return rows


def _peer_route_kernel(x_ref, mod_ref, wq_ref, sk_ref, h_ref, pj_ref, pb_ref):
    x = x_ref[0]
    h = (x * (1.0 + mod_ref[0, 1:2, :]) + mod_ref[0, 0:1, :]).astype(BF16)
    h_ref[0] = h
    q = _dot(h, wq_ref[...]).astype(BF16)
    kk = PEER_TOPK
    for hd in range(PEER_HEADS):
        qh = q[:, hd * PEER_DKEY:(hd + 1) * PEER_DKEY]
        s1 = _dot_nt(sk_ref[2 * hd], qh)
        s2 = _dot_nt(sk_ref[2 * hd + 1], qh)
        a = _top_rows(s1, kk + 1)
        b = _top_rows(s2, kk + 1)
        b_lo = jnp.concatenate(b[:8], axis=0)
        cand = [a[i] + b_lo for i in range(8)]
        cand.append(a[0] + jnp.concatenate(b[8:16], axis=0))
        cand.append(jnp.concatenate(a[8:16], axis=0) + b[0])
        cand.append(jnp.concatenate([a[0] + b[16], a[16] + b[0]] + [jnp.full_like(a[0], NEG)] * 6, axis=0))
        cand = jnp.concatenate(cand, axis=0)
        top = _top_rows(cand, kk + 1)
        thr = 0.5 * (top[kk - 1] + top[kk])
        mx = a[0] + b[0]
        zs = jnp.sum(jnp.where(cand >= thr, jnp.exp(cand - mx), 0.0), axis=0, keepdims=True)
        t1 = thr - s1
        cnt = jnp.zeros_like(s1)
        rank2 = jnp.zeros_like(s2)
        for c in range(kk):
            cnt = jnp.where(b[c] >= t1, float(c + 1), cnt)
        for c in range(kk + 1):
            rank2 = jnp.where(s2 < b[c], float(c + 1), rank2)
        e1 = jnp.exp(s1 - a[0]) / zs
        for c in range(cnt.shape[1] // 128):
            cs = slice(c * 128, (c + 1) * 128)
            pj_ref[0, hd, 0, :, c] = cnt[:, cs].reshape(PEER_NKEYS // 8, 8, 128)
            pj_ref[0, hd, 1, :, c] = e1[:, cs].reshape(PEER_NKEYS // 8, 8, 128)
        pb_ref[0, hd, 0] = rank2.astype(BF16)
        pb_ref[0, hd, 1] = jnp.exp(s2 - b[0]).astype(BF16)


def _peer_route(x1, mod, wq, sk):
    bsz, t, _ = x1.shape
    tp = min(256, t)
    pspec = pl.BlockSpec((1, PEER_HEADS, 2, PEER_NKEYS, tp), lambda b, i: (b, 0, 0, 0, i))
    jspec = pl.BlockSpec((1, PEER_HEADS, 2, PEER_NKEYS // 8, tp // 128, 8, 128), lambda b, i: (b, 0, 0, 0, i, 0, 0))
    return pl.pallas_call(
        _peer_route_kernel,
        grid=(bsz, t // tp),
        in_specs=[pl.BlockSpec((1, tp, D_MODEL), lambda b, i: (b, i, 0)),
                  pl.BlockSpec((1, 2, D_MODEL), lambda b, i: (b, 0, 0)),
                  _const_spec((D_MODEL, PEER_HEADS * PEER_DKEY), 2),
                  _const_spec((2 * PEER_HEADS, PEER_NKEYS, PEER_DKEY), 2)],
        out_specs=[pl.BlockSpec((1, tp, D_MODEL), lambda b, i: (b, i, 0)), jspec, pspec],
        out_shape=[jax.ShapeDtypeStruct((bsz, t, D_MODEL), BF16),
                   jax.ShapeDtypeStruct((bsz, PEER_HEADS, 2, PEER_NKEYS // 8, t // 128, 8, 128), F32),
                   jax.ShapeDtypeStruct((bsz, PEER_HEADS, 2, PEER_NKEYS, t), BF16)],
        compiler_params=_cparams(("parallel", "parallel")),
        name="peer_route",
    )(x1, mod, wq, sk)


PEER_LANE_CHUNK = 256
PEER_EB = 512


def _bcast_rows(pj_ref, hd, which, r0, cc, lc):
    parts = [pj_ref[0, hd, which, 0, cc * (lc // 128) + c, pl.ds(r0, 16, stride=0), :] for c in range(lc // 128)]
    return jnp.concatenate(parts, axis=1).astype(BF16)


def _peer_dense_kernel(h_ref, u_ref, vt_ref, pj_ref, pb_ref, x_ref, mod_ref, g_ref, b_ref, o_ref,
                       acc_ref, act0_ref, act1_ref, w0_ref, w1_ref, *, eb, nblk, nitems):
    e = pl.program_id(0)
    blk3 = jnp.clip(e - 2, 0, nitems - 1) % nblk
    nk = PEER_NKEYS
    tm = acc_ref.shape[1]
    lc = PEER_LANE_CHUNK
    assert tm % lc == 0 and eb == 4 * nk
    zero = jnp.zeros((), BF16)

    @pl.when(e == 0)
    def _():
        act1_ref[...] = jnp.zeros_like(act1_ref)
        w0_ref[...] = jnp.zeros_like(w0_ref)
        w1_ref[...] = jnp.zeros_like(w1_ref)

    @pl.when(blk3 == 0)
    def _():
        acc_ref[...] = jnp.zeros_like(acc_ref)

    def step(par, act_in, act_out, w_in, w_out):
        nj = eb // nk
        rc = nk // 2
        for cc in range(tm // lc):
            ls = slice(cc * lc, (cc + 1) * lc)
            for ri in range(2):
                rr = slice(ri * rc, (ri + 1) * rc)
                for jp in range(nj // 2):
                    if ri == 0:
                        us = slice(jp * (eb // 2), (jp + 1) * (eb // 2))
                        act_out[us, ls] = _dot_nt(u_ref[us, :], h_ref[0, ls, :])
                    else:
                        ds_ = slice(jp * (D_MODEL // 2), (jp + 1) * (D_MODEL // 2))
                        acc_ref[ds_, ls] += _dot(vt_ref[0, ds_, :], w_in[:, ls])
                    w = [None, None]
                    for hd in range(PEER_HEADS):
                        rank = pb_ref[0, hd, 0, rr, ls]
                        e2 = pb_ref[0, hd, 1, rr, ls]
                        for k in range(2):
                            r0 = (1 - par) * nj + jp * 2 + k
                            cnt = _bcast_rows(pj_ref, hd, 0, r0, cc, lc)
                            e1 = _bcast_rows(pj_ref, hd, 1, r0, cc, lc)
                            cnt = jnp.concatenate([cnt] * (rc // 16), axis=0)
                            e1 = jnp.concatenate([e1] * (rc // 16), axis=0)
                            term = jnp.where(rank < cnt, e2, zero) * e1
                            w[k] = term if w[k] is None else w[k] + term
                    for k in range(2):
                        jj = jp * 2 + k
                        rs = slice(jj * nk + ri * rc, jj * nk + (ri + 1) * rc)
                        a = act_in[rs, ls]
                        gelu = 0.5 * a * (1.0 + lax.erf(a * (2.0 ** -0.5)))
                        w_out[rs, ls] = w[k] * gelu.astype(BF16)

    @pl.when(e % 2 == 0)
    def _():
        step(0, act1_ref, act0_ref, w0_ref, w1_ref)

    @pl.when(e % 2 == 1)
    def _():
        step(1, act0_ref, act1_ref, w1_ref, w0_ref)

    @pl.when(blk3 == nblk - 1)
    def _():
        f = acc_ref[...].T
        x = x_ref[0]
        u = DN_ALPHA * x + mod_ref[0, 0:1, :] * f
        o_ref[0] = _layer_norm(u, g_ref[...], b_ref[...])


def _peer_dense(h2, u_tab, vt_tab, pj, pb, x1, mod, g, b):
    bsz, t, _ = x1.shape
    tm = min(1024, t)
    eb = PEER_EB
    nblk = PEER_N // eb
    ntile = t // tm
    nitems = bsz * ntile * nblk
    assert 2 * (eb // PEER_NKEYS) == 8 and nblk % 2 == 0

    def item(e, lag):
        it = jnp.clip(e - lag, 0, nitems - 1)
        tile = it // nblk
        return tile // ntile, tile % ntile, it % nblk

    def tok(lag):
        return pl.BlockSpec((1, tm, D_MODEL), lambda e: item(e, lag)[:2] + (0,))

    def const(shape):
        return pl.BlockSpec(shape, lambda e: (0,) * len(shape), pipeline_mode=pl.Buffered(1))

    return pl.pallas_call(
        functools.partial(_peer_dense_kernel, eb=eb, nblk=nblk, nitems=nitems),
        grid=(nitems + 2,),
        in_specs=[tok(0),
                  pl.BlockSpec((eb, D_MODEL), lambda e: (item(e, 0)[2], 0)),
                  pl.BlockSpec((1, D_MODEL, eb), lambda e: (item(e, 2)[2], 0, 0)),
                  pl.BlockSpec((1, PEER_HEADS, 2, 1, tm // 128, 8, 128),
                               lambda e: (item(e, 1)[0], 0, 0, item(e, 1)[2] // 2, item(e, 1)[1], 0, 0)),
                  pl.BlockSpec((1, PEER_HEADS, 2, PEER_NKEYS, tm), lambda e: (item(e, 1)[0], 0, 0, 0, item(e, 1)[1])),
                  tok(2),
                  pl.BlockSpec((1, 1, D_MODEL), lambda e: (item(e, 2)[0], 0, 0)),
                  const((1, D_MODEL)), const((1, D_MODEL))],
        out_specs=tok(2),
        out_shape=jax.ShapeDtypeStruct((bsz, t, D_MODEL), F32),
        scratch_shapes=[pltpu.VMEM((D_MODEL, tm), F32),
                        pltpu.VMEM((eb, tm), F32), pltpu.VMEM((eb, tm), F32),
                        pltpu.VMEM((eb, tm), BF16), pltpu.VMEM((eb, tm), BF16)],
        compiler_params=_cparams(("arbitrary",)),
        name="peer_dense",
    )(h2, u_tab, vt_tab, pj, pb, x1, mod, g, b)


def _rope_tables(t):
    half = DIFF_D // 2
    inv = jnp.exp(-math.log(ROPE_THETA) * jnp.arange(half, dtype=F32) / half)
    ang = jnp.arange(t, dtype=F32)[:, None] * inv[None, :]
    cos, sin = jnp.cos(ang), jnp.sin(ang)
    cos128 = jnp.tile(cos, (1, 4))
    sin128 = jnp.concatenate([-sin, -sin, sin, sin], axis=1)
    return cos128, sin128


def _layer_params(l, w_in, gla_w_gk, gla_b_gk, gla_norm_g, na_rpb, diff_lambda, diff_norm_g, w_br, w_o,
                  ln1_g, ln1_b, peer_wq, peer_subkeys, peer_u, peer_v, ln2_g, ln2_b):
    w = w_in[l]
    w_gla = jnp.pad(w[:, :3104], ((0, 0), (0, GLA_COLS - 3104))).astype(BF16)
    w_att = w[:, 3104:3104 + ATT_COLS]
    qscale = np.ones((ATT_COLS,), np.float32)
    qscale[0:1024] = NA_DH ** -0.5
    qscale[3072:4096] = DIFF_D ** -0.5 * math.log2(math.e)
    half = DIFF_D // 2
    lane_src = np.concatenate([np.arange(0, half), np.arange(DIFF_D, DIFF_D + half),
                               np.arange(half, DIFF_D), np.arange(DIFF_D + half, 2 * DIFF_D)])
    cols = np.arange(ATT_COLS)
    for base in (3072, 4096):
        for hd in range(DIFF_HEADS):
            cols[base + hd * 128:base + (hd + 1) * 128] = base + hd * 128 + lane_src
    w_att = (w_att * qscale)[:, cols].astype(BF16)
    w_gt = w[:, 3104 + ATT_COLS:].astype(BF16)
    wgk = jnp.zeros((2, 128, GLA_HEADS * GLA_DK), F32)
    wgk = wgk.at[0, :GLA_RANK].set(gla_w_gk[l, 0]).at[1, GLA_RANK:2 * GLA_RANK].set(gla_w_gk[l, 1]).astype(BF16)
    sk = peer_subkeys[l]
    half = PEER_DKEY // 2
    skp = jnp.zeros((PEER_HEADS, 2, PEER_NKEYS, PEER_DKEY), F32)
    skp = skp.at[:, 0, :, :half].set(sk[:, 0]).at[:, 1, :, half:].set(sk[:, 1])
    return dict(
        w_gla=w_gla, w_att=w_att, w_gt=w_gt, wgk=wgk, bgk=gla_b_gk[l],
        gla_g=gla_norm_g[l][None], na_bias=_na_bias_table(na_rpb[l]),
        lam=diff_lambda[l], diff_g=diff_norm_g[l][None],
        w_br=w_br[l].astype(BF16), w_o=w_o[l].astype(BF16),
        ln1_g=ln1_g[l][None], ln1_b=ln1_b[l][None],
        wq=peer_wq[l].astype(BF16), sk=skp.reshape(2 * PEER_HEADS, PEER_NKEYS, PEER_DKEY).astype(BF16),
        u=peer_u[l].astype(BF16),
        vt=peer_v[l].astype(BF16).reshape(PEER_N // PEER_EB, PEER_EB, D_MODEL).transpose(0, 2, 1),
        ln2_g=ln2_g[l][None], ln2_b=ln2_b[l][None],
        lam_init=0.8 - 0.6 * math.exp(-0.3 * l),
    )


def _layer(x, mod, p, rope, cms):
    pa = _proj(x, mod[:, 0:2], p["w_gla"], F32, tn=GLA_COLS, name="proj_gla")
    pb = _proj(x, mod[:, 0:2], p["w_att"], BF16, tn=1024, rope=rope + ((3, 4),), name="proj_att")
    ofwd = _gla_dir(pa, cms[0], p["wgk"][0], p["bgk"][0:1], reverse=False)
    ya = _gla_dir(pa, cms[1], p["wgk"][1], p["bgk"][1:2], reverse=True, ofwd=ofwd, norm_g=p["gla_g"])
    yb = _na(pb, p["na_bias"])
    yc = _diff(pb, p["lam"], p["diff_g"], p["lam_init"])
    x1 = _merge(x, mod[:, 0:3], ya, yb, yc, p["w_gt"], p["w_br"], p["w_o"], p["ln1_g"], p["ln1_b"])
    h2, pj, pb2 = _peer_route(x1, mod[:, 3:5], p["wq"], p["sk"])
    return _peer_dense(h2, p["u"], p["vt"], pj, pb2, x1, mod[:, 5:6], p["ln2_g"], p["ln2_b"])


def kernel(x_prompt, x_sample, c_prompt, c_sample, w_ada, b_ada, w_in, gla_w_gk, gla_b_gk, gla_norm_g, na_rpb, diff_lambda, diff_norm_g, w_br, w_o, ln1_g, ln1_b, peer_wq, peer_subkeys, peer_u, peer_v, ln2_g, ln2_b):
    nb = x_prompt.shape[0]
    c_all = jnp.concatenate([c_prompt, c_sample], axis=0)
    cms = (jnp.asarray(_gla_masks(False)), jnp.asarray(_gla_masks(True)))
    ropes = {x.shape[1]: _rope_tables(x.shape[1]) for x in (x_prompt, x_sample)}
    xs = [x_prompt, x_sample]
    for l in range(DEPTH):
        p = _layer_params(l, w_in, gla_w_gk, gla_b_gk, gla_norm_g, na_rpb, diff_lambda, diff_norm_g, w_br, w_o,
                          ln1_g, ln1_b, peer_wq, peer_subkeys, peer_u, peer_v, ln2_g, ln2_b)
        mod_all = _ada(c_all, w_ada[l].astype(BF16), b_ada[l][None]).reshape(c_all.shape[0], 6, D_MODEL)
        mods = [mod_all[:nb], mod_all[nb:]]
        xs = [_layer(x, m, p, ropes[x.shape[1]], cms) for x, m in zip(xs, mods)]
    return (xs[0], xs[1])
```

```python
import functools
import math

import numpy as np
import jax
import jax.numpy as jnp
from jax import lax
from jax.experimental import pallas as pl
from jax.experimental.pallas import tpu as pltpu

F32 = jnp.float32
BF16 = jnp.bfloat16

D_MODEL = 1024
DEPTH = 2
GRID_W = 64
GLA_HEADS, GLA_DK, GLA_DV, GLA_RANK, GLA_TAU, GLA_CHUNK = 4, 128, 256, 16, 16.0, 64
GLA_SUB = 16
NA_HEADS, NA_DH, NA_KR, NA_KC = 16, 64, 8, 16
NA_ROW_UNROLL = 16
DIFF_HEADS, DIFF_D = 8, 64
DIFF_KEY_CHUNK = 512
ROPE_THETA = 10000.0
PEER_HEADS, PEER_NKEYS, PEER_DKEY, PEER_TOPK = 8, 128, 128, 16
PEER_N = PEER_NKEYS * PEER_NKEYS
DN_ALPHA = (2 * DEPTH) ** 0.25
LN_EPS = 1e-5
NEG = -1e30

GLA_COLS = 3328
ATT_COLS = 6144
VMEM_LIMIT = 56 * 1024 * 1024


def _cparams(sem):
    return pltpu.CompilerParams(dimension_semantics=sem, vmem_limit_bytes=VMEM_LIMIT)


def _dot(a, b):
    return jnp.dot(a, b, preferred_element_type=F32)


def _dot_nt(a, b):
    return lax.dot_general(a, b, (((1,), (1,)), ((), ())), preferred_element_type=F32)


def _dot_tn(a, b):
    return lax.dot_general(a, b, (((0,), (0,)), ((), ())), preferred_element_type=F32)


def _sigmoid(x):
    return 1.0 / (1.0 + jnp.exp(-x))


def _layer_norm(u, g, b):
    mu = jnp.mean(u, axis=-1, keepdims=True)
    d = u - mu
    var = jnp.mean(d * d, axis=-1, keepdims=True)
    return d * lax.rsqrt(var + LN_EPS) * g + b


def _ada_kernel(c_ref, w_ref, b_ref, o_ref):
    c = c_ref[...]
    s = (c * _sigmoid(c)).astype(BF16)
    o_ref[...] = _dot(s, w_ref[...]) + b_ref[...]


def _ada(c, w, b):
    n = c.shape[0]
    cols = w.shape[1]
    return pl.pallas_call(
        _ada_kernel,
        grid=(cols // D_MODEL,),
        in_specs=[pl.BlockSpec((n, D_MODEL), lambda j: (0, 0)),
                  pl.BlockSpec((D_MODEL, D_MODEL), lambda j: (0, j)),
                  pl.BlockSpec((1, D_MODEL), lambda j: (0, j))],
        out_specs=pl.BlockSpec((n, D_MODEL), lambda j: (0, j)),
        out_shape=jax.ShapeDtypeStruct((n, cols), F32),
        compiler_params=_cparams(("arbitrary",)),
        name="ada_mod",
    )(c, w, b)


PROJ_COL_CHUNK = 256


def _proj_kernel(x_ref, mod_ref, w_ref, *rest, rope_tiles):
    if rope_tiles:
        cos_ref, sin_ref, o_ref, h_ref = rest
    else:
        o_ref, h_ref = rest
    j = pl.program_id(2)

    @pl.when(j == 0)
    def _():
        x = x_ref[0]
        h_ref[...] = (x * (1.0 + mod_ref[0, 1:2, :]) + mod_ref[0, 0:1, :]).astype(BF16)

    tn = w_ref.shape[1]
    bounds = list(range(0, tn, PROJ_COL_CHUNK)) + [tn]

    def body(with_rope):
        h = h_ref[...]
        if with_rope:
            cos = cos_ref[...]
            sin = sin_ref[...]
        accs = {0: _dot(h, w_ref[:, bounds[0]:bounds[1]])}
        for c in range(len(bounds) - 1):
            if c + 2 < len(bounds):
                accs[c + 1] = _dot(h, w_ref[:, bounds[c + 1]:bounds[c + 2]])
            acc = accs.pop(c)
            if not with_rope:
                o_ref[0, :, bounds[c]:bounds[c + 1]] = acc.astype(o_ref.dtype)
                continue
            for p in range(acc.shape[1] // 128):
                xc = acc[:, p * 128:(p + 1) * 128]
                rot = pltpu.roll(xc, DIFF_D, 1)
                lo = bounds[c] + p * 128
                o_ref[0, :, lo:lo + 128] = (xc * cos + rot * sin).astype(o_ref.dtype)

    if not rope_tiles:
        body(False)
        return
    is_rope = functools.reduce(jnp.logical_or, [j == t for t in rope_tiles])
    pl.when(is_rope)(lambda: body(True))
    pl.when(jnp.logical_not(is_rope))(lambda: body(False))


def _proj(x, mod, w, out_dtype, tn, rope=None, name="proj"):
    bsz, t, _ = x.shape
    cols = w.shape[1]
    tm = min(1024, t)
    in_specs = [pl.BlockSpec((1, tm, D_MODEL), lambda b, i, j: (b, i, 0)),
                pl.BlockSpec((1, 2, D_MODEL), lambda b, i, j: (b, 0, 0)),
                pl.BlockSpec((D_MODEL, tn), lambda b, i, j: (0, j),
                             **({"pipeline_mode": pl.Buffered(1)} if tn == cols else {}))]
    args = [x, mod, w]
    rope_tiles = ()
    if rope is not None:
        cos, sin, rope_tiles = rope
        in_specs += [pl.BlockSpec((tm, 128), lambda b, i, j: (i, 0)),
                     pl.BlockSpec((tm, 128), lambda b, i, j: (i, 0))]
        args += [cos, sin]
    return pl.pallas_call(
        functools.partial(_proj_kernel, rope_tiles=tuple(rope_tiles)),
        grid=(bsz, t // tm, cols // tn),
        in_specs=in_specs,
        out_specs=pl.BlockSpec((1, tm, tn), lambda b, i, j: (b, i, j)),
        out_shape=jax.ShapeDtypeStruct((bsz, t, cols), out_dtype),
        scratch_shapes=[pltpu.VMEM((tm, D_MODEL), BF16)],
        compiler_params=_cparams(("parallel", "parallel", "arbitrary")),
        name=name,
    )(*args)


def _gla_masks(reverse):
    c, s = GLA_CHUNK, GLA_SUB
    i = np.arange(c)[:, None]
    j = np.arange(c)[None, :]
    if not reverse:
        cum = (j <= i)
        ref = (j < (i // s) * s)
    else:
        cum = (j >= i)
        ref = (j >= (i // s + 1) * s)
    return np.concatenate([cum, ref], axis=0).astype(np.float32)


def _gla_kernel(q_ref, k_ref, v_ref, glr_ref, cm_ref, wgk_ref, bgk_ref, *rest, reverse, n_chunks):
    if reverse:
        ofwd_ref, r_ref, ng_ref, o_ref, st_ref, acc_ref = rest
    else:
        o_ref, st_ref = rest
        acc_ref = o_ref.at[0]
    c, s = GLA_CHUNK, GLA_SUB
    nsub = c // s
    dk = GLA_DK

    @pl.when(pl.program_id(2) == 0)
    def _():
        st_ref[...] = jnp.zeros_like(st_ref)

    def lanes(x):
        return jnp.concatenate([x[ci * c:(ci + 1) * c] for ci in range(n_chunks)], axis=1)

    z = _dot(glr_ref[0].astype(BF16), wgk_ref[...]) + bgk_ref[...]
    g = lanes((jnp.minimum(z, 0.0) - jnp.log(1.0 + jnp.exp(-jnp.abs(z)))) * (1.0 / GLA_TAU))
    g_hi = g.astype(BF16)
    g_lo = (g - g_hi.astype(F32)).astype(BF16)
    cm = cm_ref[...].astype(BF16)
    cums = _dot(cm, g_hi) + _dot(cm, g_lo)
    bc, bref = cums[:c], cums[c:]
    q = lanes(q_ref[0]) * (GLA_DK ** -0.5)
    k = lanes(k_ref[0])
    qe = q * jnp.exp(bc - bref)
    qi = (qe * jnp.exp(bref)).astype(BF16)
    qe = qe.astype(BF16)
    bl = bc[0:1] if reverse else bc[c - 1:c]
    dec = jnp.exp(bl)
    kl = (k * jnp.exp(bl - bc)).astype(BF16)
    row = lax.broadcasted_iota(jnp.int32, (c, 1), 0)
    ksub = []
    for si in range(nsub):
        ok = (row >= si * s) if reverse else (row < (si + 1) * s)
        ksub.append(jnp.where(ok, k * jnp.exp(bref[si * s:si * s + 1] - bc), 0.0).astype(BF16))
    rr = lax.broadcasted_iota(jnp.int32, (c, nsub * c), 0)
    cc = lax.broadcasted_iota(jnp.int32, (c, nsub * c), 1)
    causal = (cc % c > rr) if reverse else (cc % c <= rr)
    keep = jnp.logical_and(cc // c == rr // s, causal)
    vs = [v_ref[0, ci * c:(ci + 1) * c, :].astype(BF16) for ci in range(n_chunks)]
    atts = []
    for ci in range(n_chunks):
        ls = slice(ci * dk, (ci + 1) * dk)
        kcat = jnp.concatenate([ks[:, ls] for ks in ksub], axis=0)
        atts.append(_dot_nt(qe[:, ls], kcat))
    atts = [jnp.where(keep, a, 0.0).astype(BF16) for a in atts]
    for ci in range(n_chunks):
        acc_ref[ci * c:(ci + 1) * c, :] = _dot(atts[ci], jnp.concatenate([vs[ci]] * nsub, axis=0))
    uts = [_dot_tn(vs[ci], kl[:, ci * dk:(ci + 1) * dk]) for ci in range(n_chunks)]

    st = st_ref[...]
    order = range(n_chunks - 1, -1, -1) if reverse else range(n_chunks)
    for ci in order:
        ls = slice(ci * dk, (ci + 1) * dk)
        inter = _dot_nt(qi[:, ls], st.astype(BF16))
        acc_ref[ci * c:(ci + 1) * c, :] = acc_ref[ci * c:(ci + 1) * c, :] + inter
        st = st * dec[:, ls] + uts[ci]
    st_ref[...] = st

    if reverse:
        y = ofwd_ref[0] + acc_ref[...]
        y = y * lax.rsqrt(jnp.mean(y * y, axis=-1, keepdims=True) + LN_EPS) * ng_ref[...]
        r = r_ref[0]
        o_ref[0] = (y * (r * _sigmoid(r))).astype(o_ref.dtype)


def _gla_dir(pa, cm, wgk, bgk, reverse, ofwd=None, norm_g=None):
    bsz, t, _ = pa.shape
    tt = min(4096, t)
    nt = t // tt

    def ti(i):
        return (nt - 1 - i) if reverse else i

    in_specs = [pl.BlockSpec((1, tt, GLA_DK), lambda b, h, i: (b, ti(i), h)),
                pl.BlockSpec((1, tt, GLA_DK), lambda b, h, i: (b, ti(i), GLA_HEADS + h)),
                pl.BlockSpec((1, tt, GLA_DV), lambda b, h, i: (b, ti(i), GLA_HEADS + h)),
                pl.BlockSpec((1, tt, 128), lambda b, h, i: (b, ti(i), 3072 // 128)),
                pl.BlockSpec((2 * GLA_CHUNK, GLA_CHUNK), lambda b, h, i: (0, 0)),
                pl.BlockSpec((128, GLA_DK), lambda b, h, i: (0, h)),
                pl.BlockSpec((1, GLA_DK), lambda b, h, i: (0, h))]
    args = [pa, pa, pa, pa, cm, wgk, bgk]
    scratch = [pltpu.VMEM((GLA_DV, GLA_DK), F32)]
    if reverse:
        in_specs += [pl.BlockSpec((1, tt, GLA_DV), lambda b, h, i: (b, ti(i), h)),
                     pl.BlockSpec((1, tt, GLA_DV), lambda b, h, i: (b, ti(i), 2 * GLA_HEADS + h)),
                     pl.BlockSpec((1, GLA_DV), lambda b, h, i: (0, 0))]
        args += [ofwd, pa, norm_g]
        scratch += [pltpu.VMEM((tt, GLA_DV), F32)]
        out_dtype = BF16
    else:
        out_dtype = F32
    return pl.pallas_call(
        functools.partial(_gla_kernel, reverse=reverse, n_chunks=tt // GLA_CHUNK),
        grid=(bsz, GLA_HEADS, nt),
        in_specs=in_specs,
        out_specs=pl.BlockSpec((1, tt, GLA_DV), lambda b, h, i: (b, ti(i), h)),
        out_shape=jax.ShapeDtypeStruct((bsz, t, GLA_HEADS * GLA_DV), out_dtype),
        scratch_shapes=scratch,
        compiler_params=_cparams(("parallel", "parallel", "arbitrary")),
        name="gla_bwd" if reverse else "gla_fwd",
    )(*args)


def _na_bias_table(rpb):
    qc = np.arange(GRID_W)[:, None]
    kc = np.arange(GRID_W)[None, :]
    cs = np.clip(qc - NA_KC // 2, 0, GRID_W - NA_KC)
    allowed = (kc >= cs) & (kc < cs + NA_KC)
    dc = np.clip(kc - qc + NA_KC - 1, 0, 2 * NA_KC - 2)
    onehot = (dc[None] == np.arange(2 * NA_KC - 1)[:, None, None]).astype(np.float32)
    tab = jnp.einsum("hrd,dqk->hrqk", rpb.astype(F32), jnp.asarray(onehot), precision=lax.Precision.HIGHEST)
    tab = jnp.where(jnp.asarray(allowed)[None, None], tab, NEG)
    tab = jnp.stack([tab[:, NA_KR - 1 - d:2 * NA_KR - 1 - d] for d in range(NA_KR)], axis=1)
    tab = tab.transpose(0, 1, 3, 2, 4).reshape(NA_HEADS // 2, 2, NA_KR, GRID_W, NA_KR * GRID_W)
    return tab.transpose(0, 2, 1, 3, 4).reshape(NA_HEADS // 2, NA_KR, 2 * GRID_W, NA_KR * GRID_W)


def _na_kernel(q_ref, k_ref, v_ref, bias_ref, o_ref, *, rows):
    lane = lax.broadcasted_iota(jnp.int32, (GRID_W, 128), 1)
    low = lane < NA_DH
    win = NA_KR * GRID_W
    nr = NA_ROW_UNROLL

    def body(it, carry):
        q0s, scs, vws = [], [], []
        for u in range(nr):
            r = it * nr + u
            rs = jnp.clip(r - NA_KR // 2, 0, rows - NA_KR)
            q0 = pl.multiple_of(r * GRID_W, GRID_W)
            k0 = pl.multiple_of(rs * GRID_W, GRID_W)
            q = q_ref[0, pl.ds(q0, GRID_W), :]
            zq = jnp.zeros_like(q)
            q2 = jnp.concatenate([jnp.where(low, q, zq), jnp.where(low, zq, q)], axis=0)
            scs.append(_dot_nt(q2, k_ref[0, pl.ds(k0, win), :]) + bias_ref[0, r - rs])
            vws.append(v_ref[0, pl.ds(k0, win), :])
            q0s.append(q0)
        ms = [jnp.max(sc, axis=-1, keepdims=True) for sc in scs]
        es = [jnp.exp(sc - m) for sc, m in zip(scs, ms)]
        zs = [jnp.sum(e, axis=-1, keepdims=True) for e in es]
        os_ = [_dot(e.astype(BF16), vw) / z for e, vw, z in zip(es, vws, zs)]
        for q0, o in zip(q0s, os_):
            o_ref[0, pl.ds(q0, GRID_W), :] = jnp.where(low, o[:GRID_W], o[GRID_W:]).astype(o_ref.dtype)
        return carry

    lax.fori_loop(0, rows // nr, body, 0)


def _na(pb, bias):
    bsz, t, _ = pb.shape
    rows = t // GRID_W
    assert rows >= NA_KR and rows % NA_ROW_UNROLL == 0
    npair = NA_HEADS // 2
    return pl.pallas_call(
        functools.partial(_na_kernel, rows=rows),
        grid=(bsz, npair),
        in_specs=[pl.BlockSpec((1, t, 128), lambda b, j: (b, 0, j)),
                  pl.BlockSpec((1, t, 128), lambda b, j: (b, 0, npair + j)),
                  pl.BlockSpec((1, t, 128), lambda b, j: (b, 0, 2 * npair + j)),
                  pl.BlockSpec((1, NA_KR, 2 * GRID_W, NA_KR * GRID_W), lambda b, j: (j, 0, 0, 0))],
        out_specs=pl.BlockSpec((1, t, 128), lambda b, j: (b, 0, j)),
        out_shape=jax.ShapeDtypeStruct((bsz, t, D_MODEL), BF16),
        compiler_params=_cparams(("parallel", "parallel")),
        name="na_attn",
    )(pb, pb, pb, bias)


def _diff_kernel(q_ref, k_ref, v_ref, lam_ref, ng_ref, o_ref, vt_ref, *, lam_init):
    dv = 2 * DIFF_D

    @pl.when(pl.program_id(2) == 0)
    def _():
        vt_ref[0:dv, :] = v_ref[0].astype(F32).T.astype(BF16)
        vt_ref[dv:, :] = jnp.ones((8, vt_ref.shape[1]), BF16)

    lv = lam_ref[...]
    l1 = jnp.sum(lv[0:1] * lv[1:2], axis=-1, keepdims=True)
    l2 = jnp.sum(lv[2:3] * lv[3:4], axis=-1, keepdims=True)
    lam = jnp.exp(l1) - jnp.exp(l2) + lam_init
    q = q_ref[0]
    k = k_ref[0]
    low = lax.broadcasted_iota(jnp.int32, q.shape, 1) % DIFF_D < DIFF_D // 2
    zq = jnp.zeros_like(q)
    t = k.shape[0]
    kc = min(DIFF_KEY_CHUNK, t)
    nc = t // kc
    qms = (jnp.where(low, q, zq), jnp.where(low, zq, q))
    st = [dict(m=None, acc=None) for _ in qms]
    sc = [[None] * nc for _ in qms]
    for p_, qm in enumerate(qms):
        sc[p_][0] = _dot_nt(k[:kc], qm)
    for c in range(nc):
        for p_, qm in enumerate(qms):
            if c + 1 < nc:
                sc[p_][c + 1] = _dot_nt(k[(c + 1) * kc:(c + 2) * kc], qm)
            s_ = sc[p_][c]
            d = st[p_]
            cm = jnp.max(s_, axis=0, keepdims=True)
            vc = vt_ref[:, c * kc:(c + 1) * kc]
            if c == 0:
                d["m"] = cm
                d["acc"] = _dot(vc, jnp.exp2(s_ - cm).astype(BF16))
            else:
                m_new = jnp.maximum(d["m"], cm)
                d["acc"] = jnp.exp2(d["m"] - m_new) * d["acc"] + _dot(vc, jnp.exp2(s_ - m_new).astype(BF16))
                d["m"] = m_new
    outs = [d["acc"][:dv] / d["acc"][dv:dv + 1] for d in st]
    o = outs[0] - lam * outs[1]
    y = o * lax.rsqrt(jnp.mean(o * o, axis=0, keepdims=True) + LN_EPS)
    o_ref[0] = (y.T * (ng_ref[...] * (1.0 - lam_init))).astype(o_ref.dtype)


def _diff(pb, lam_vec, norm_g, lam_init):
    bsz, t, _ = pb.shape
    tq = min(1024, t)
    base = (NA_HEADS * NA_DH * 3) // 128
    return pl.pallas_call(
        functools.partial(_diff_kernel, lam_init=lam_init),
        grid=(bsz, DIFF_HEADS, t // tq),
        in_specs=[pl.BlockSpec((1, tq, 128), lambda b, h, i: (b, i, base + h)),
                  pl.BlockSpec((1, t, 128), lambda b, h, i: (b, 0, base + DIFF_HEADS + h)),
                  pl.BlockSpec((1, t, 128), lambda b, h, i: (b, 0, base + 2 * DIFF_HEADS + h)),
                  pl.BlockSpec((4, DIFF_D), lambda b, h, i: (0, 0)),
                  pl.BlockSpec((1, 2 * DIFF_D), lambda b, h, i: (0, 0))],
        out_specs=pl.BlockSpec((1, tq, 128), lambda b, h, i: (b, i, h)),
        out_shape=jax.ShapeDtypeStruct((bsz, t, D_MODEL), BF16),
        scratch_shapes=[pltpu.VMEM((2 * DIFF_D + 8, t), BF16)],
        compiler_params=_cparams(("parallel", "parallel", "arbitrary")),
        name="diff_attn",
    )(pb, pb, pb, lam_vec, norm_g)


def _merge_kernel(x_ref, mod_ref, ya_ref, yb_ref, yc_ref, wgt_ref, wbr_ref, wo_ref, g_ref, b_ref, o_ref):
    x = x_ref[0]
    h = (x * (1.0 + mod_ref[0, 1:2, :]) + mod_ref[0, 0:1, :]).astype(BF16)
    merged = None
    for n, y_ref in enumerate((ya_ref, yb_ref, yc_ref)):
        gt = _dot(h, wgt_ref[:, n * D_MODEL:(n + 1) * D_MODEL])
        br = _dot(y_ref[0], wbr_ref[n])
        term = _sigmoid(gt) * br
        merged = term if merged is None else merged + term
    out = _dot(merged.astype(BF16), wo_ref[...])
    u = DN_ALPHA * x + mod_ref[0, 2:3, :] * out
    o_ref[0] = _layer_norm(u, g_ref[...], b_ref[...])


def _const_spec(shape, nidx):
    zeros = (0,) * len(shape)
    if nidx == 2:
        return pl.BlockSpec(shape, lambda b, i: zeros, pipeline_mode=pl.Buffered(1))
    return pl.BlockSpec(shape, lambda b, i, e: zeros, pipeline_mode=pl.Buffered(1))


def _merge(x, mod, ya, yb, yc, wgt, wbr, wo, g, b):
    bsz, t, _ = x.shape
    tm = min(512, t)
    tok = pl.BlockSpec((1, tm, D_MODEL), lambda bb, i: (bb, i, 0))
    return pl.pallas_call(
        _merge_kernel,
        grid=(bsz, t // tm),
        in_specs=[tok, pl.BlockSpec((1, 3, D_MODEL), lambda bb, i: (bb, 0, 0)), tok, tok, tok,
                  _const_spec((D_MODEL, 3 * D_MODEL), 2), _const_spec((3, D_MODEL, D_MODEL), 2),
                  _const_spec((D_MODEL, D_MODEL), 2), _const_spec((1, D_MODEL), 2), _const_spec((1, D_MODEL), 2)],
        out_specs=tok,
        out_shape=jax.ShapeDtypeStruct((bsz, t, D_MODEL), F32),
        compiler_params=_cparams(("parallel", "parallel")),
        name="merge_ln",
    )(x, mod, ya, yb, yc, wgt, wbr, wo, g, b)


def _sort_network(n):
    pairs = []
    p = 1
    while p < n:
        k = p
        while k >= 1:
            for jj in range(k % p, n - k, 2 * k):
                for ii in range(min(k, n - jj - k)):
                    if (ii + jj) // (2 * p) == (ii + jj + k) // (2 * p):
                        pairs.append((ii + jj, ii + jj + k))
            k //= 2
        p *= 2
    return pairs


_SORT16 = _sort_network(16)


def _top_rows(x, n):
    rows_in, lanes = x.shape
    groups = [x[8 * g:8 * g + 8] for g in range(rows_in // 8)]
    groups += [jnp.full((8, lanes), NEG, F32)] * (16 - len(groups))
    for i, j in _SORT16:
        if j >= rows_in // 8:
            continue
        hi = jnp.maximum(groups[i], groups[j])
        groups[j] = jnp.minimum(groups[i], groups[j])
        groups[i] = hi
    rows = []
    for r in range(n):
        m = jnp.max(groups[0], axis=0, keepdims=True)
        rows.append(m)
        hit = groups[0] >= m
        for v in range(min(16, n - r - 1)):
            nxt = groups[v + 1] if v + 1 < 16 else NEG
            groups[v] = jnp.where(hit, nxt, groups[v])
    return rows


def _peer_route_kernel(x_ref, mod_ref, wq_ref, sk_ref, h_ref, pj_ref, pb_ref):
    x = x_ref[0]
    h = (x * (1.0 + mod_ref[0, 1:2, :]) + mod_ref[0, 0:1, :]).astype(BF16)
    h_ref[0] = h
    q = _dot(h, wq_ref[...]).astype(BF16)
    kk = PEER_TOPK
    for hd in range(PEER_HEADS):
        qh = q[:, hd * PEER_DKEY:(hd + 1) * PEER_DKEY]
        s1 = _dot_nt(sk_ref[2 * hd], qh)
        s2 = _dot_nt(sk_ref[2 * hd + 1], qh)
        a = _top_rows(s1, kk + 1)
        b = _top_rows(s2, kk + 1)
        b_lo = jnp.concatenate(b[:8], axis=0)
        cand = [a[i] + b_lo for i in range(8)]
        cand.append(a[0] + jnp.concatenate(b[8:16], axis=0))
        cand.append(jnp.concatenate(a[8:16], axis=0) + b[0])
        cand.append(jnp.concatenate([a[0] + b[16], a[16] + b[0]] + [jnp.full_like(a[0], NEG)] * 6, axis=0))
        cand = jnp.concatenate(cand, axis=0)
        top = _top_rows(cand, kk + 1)
        thr = 0.5 * (top[kk - 1] + top[kk])
        mx = a[0] + b[0]
        zs = jnp.sum(jnp.where(cand >= thr, jnp.exp(cand - mx), 0.0), axis=0, keepdims=True)
        t1 = thr - s1
        cnt = jnp.zeros_like(s1)
        rank2 = jnp.zeros_like(s2)
        for c in range(kk):
            cnt = jnp.where(b[c] >= t1, float(c + 1), cnt)
        for c in range(kk + 1):
            rank2 = jnp.where(s2 < b[c], float(c + 1), rank2)
        e1 = jnp.exp(s1 - a[0]) / zs
        for c in range(cnt.shape[1] // 128):
            cs = slice(c * 128, (c + 1) * 128)
            pj_ref[0, hd, 0, :, c] = cnt[:, cs].reshape(PEER_NKEYS // 8, 8, 128)
            pj_ref[0, hd, 1, :, c] = e1[:, cs].reshape(PEER_NKEYS // 8, 8, 128)
        pb_ref[0, hd, 0] = rank2.astype(BF16)
        pb_ref[0, hd, 1] = jnp.exp(s2 - b[0]).astype(BF16)


def _peer_route(x1, mod, wq, sk):
    bsz, t, _ = x1.shape
    tp = min(256, t)
    pspec = pl.BlockSpec((1, PEER_HEADS, 2, PEER_NKEYS, tp), lambda b, i: (b, 0, 0, 0, i))
    jspec = pl.BlockSpec((1, PEER_HEADS, 2, PEER_NKEYS // 8, tp // 128, 8, 128), lambda b, i: (b, 0, 0, 0, i, 0, 0))
    return pl.pallas_call(
        _peer_route_kernel,
        grid=(bsz, t // tp),
        in_specs=[pl.BlockSpec((1, tp, D_MODEL), lambda b, i: (b, i, 0)),
                  pl.BlockSpec((1, 2, D_MODEL), lambda b, i: (b, 0, 0)),
                  _const_spec((D_MODEL, PEER_HEADS * PEER_DKEY), 2),
                  _const_spec((2 * PEER_HEADS, PEER_NKEYS, PEER_DKEY), 2)],
        out_specs=[pl.BlockSpec((1, tp, D_MODEL), lambda b, i: (b, i, 0)), jspec, pspec],
        out_shape=[jax.ShapeDtypeStruct((bsz, t, D_MODEL), BF16),
                   jax.ShapeDtypeStruct((bsz, PEER_HEADS, 2, PEER_NKEYS // 8, t // 128, 8, 128), F32),
                   jax.ShapeDtypeStruct((bsz, PEER_HEADS, 2, PEER_NKEYS, t), BF16)],
        compiler_params=_cparams(("parallel", "parallel")),
        name="peer_route",
    )(x1, mod, wq, sk)


PEER_LANE_CHUNK = 256
PEER_EB = 512


def _bcast_rows(pj_ref, hd, which, r0, cc, lc):
    parts = [pj_ref[0, hd, which, 0, cc * (lc // 128) + c, pl.ds(r0, 16, stride=0), :] for c in range(lc // 128)]
    return jnp.concatenate(parts, axis=1).astype(BF16)


def _peer_dense_kernel(h_ref, u_ref, vt_ref, pj_ref, pb_ref, x_ref, mod_ref, g_ref, b_ref, o_ref,
                       acc_ref, act0_ref, act1_ref, w0_ref, w1_ref, *, eb, nblk, nitems):
    e = pl.program_id(0)
    blk3 = jnp.clip(e - 2, 0, nitems - 1) % nblk
    nk = PEER_NKEYS
    tm = acc_ref.shape[1]
    lc = PEER_LANE_CHUNK
    assert tm % lc == 0 and eb == 4 * nk
    zero = jnp.zeros((), BF16)

    @pl.when(e == 0)
    def _():
        act1_ref[...] = jnp.zeros_like(act1_ref)
        w0_ref[...] = jnp.zeros_like(w0_ref)
        w1_ref[...] = jnp.zeros_like(w1_ref)

    @pl.when(blk3 == 0)
    def _():
        acc_ref[...] = jnp.zeros_like(acc_ref)

    def step(par, act_in, act_out, w_in, w_out):
        nj = eb // nk
        rc = nk // 2
        for cc in range(tm // lc):
            ls = slice(cc * lc, (cc + 1) * lc)
            for ri in range(2):
                rr = slice(ri * rc, (ri + 1) * rc)
                for jp in range(nj // 2):
                    if ri == 0:
                        us = slice(jp * (eb // 2), (jp + 1) * (eb // 2))
                        act_out[us, ls] = _dot_nt(u_ref[us, :], h_ref[0, ls, :])
                    else:
                        ds_ = slice(jp * (D_MODEL // 2), (jp + 1) * (D_MODEL // 2))
                        acc_ref[ds_, ls] += _dot(vt_ref[0, ds_, :], w_in[:, ls])
                    w = [None, None]
                    for hd in range(PEER_HEADS):
                        rank = pb_ref[0, hd, 0, rr, ls]
                        e2 = pb_ref[0, hd, 1, rr, ls]
                        for k in range(2):
                            r0 = (1 - par) * nj + jp * 2 + k
                            cnt = _bcast_rows(pj_ref, hd, 0, r0, cc, lc)
                            e1 = _bcast_rows(pj_ref, hd, 1, r0, cc, lc)
                            cnt = jnp.concatenate([cnt] * (rc // 16), axis=0)
                            e1 = jnp.concatenate([e1] * (rc // 16), axis=0)
                            term = jnp.where(rank < cnt, e2, zero) * e1
                            w[k] = term if w[k] is None else w[k] + term
                    for k in range(2):
                        jj = jp * 2 + k
                        rs = slice(jj * nk + ri * rc, jj * nk + (ri + 1) * rc)
                        a = act_in[rs, ls]
                        gelu = 0.5 * a * (1.0 + lax.erf(a * (2.0 ** -0.5)))
                        w_out[rs, ls] = w[k] * gelu.astype(BF16)

    @pl.when(e % 2 == 0)
    def _():
        step(0, act1_ref, act0_ref, w0_ref, w1_ref)

    @pl.when(e % 2 == 1)
    def _():
        step(1, act0_ref, act1_ref, w1_ref, w0_ref)

    @pl.when(blk3 == nblk - 1)
    def _():
        f = acc_ref[...].T
        x = x_ref[0]
        u = DN_ALPHA * x + mod_ref[0, 0:1, :] * f
        o_ref[0] = _layer_norm(u, g_ref[...], b_ref[...])


def _peer_dense(h2, u_tab, vt_tab, pj, pb, x1, mod, g, b):
    bsz, t, _ = x1.shape
    tm = min(1024, t)
    eb = PEER_EB
    nblk = PEER_N // eb
    ntile = t // tm
    nitems = bsz * ntile * nblk
    assert 2 * (eb // PEER_NKEYS) == 8 and nblk % 2 == 0

    def item(e, lag):
        it = jnp.clip(e - lag, 0, nitems - 1)
        tile = it // nblk
        return tile // ntile, tile % ntile, it % nblk

    def tok(lag):
        return pl.BlockSpec((1, tm, D_MODEL), lambda e: item(e, lag)[:2] + (0,))

    def const(shape):
        return pl.BlockSpec(shape, lambda e: (0,) * len(shape), pipeline_mode=pl.Buffered(1))

    return pl.pallas_call(
        functools.partial(_peer_dense_kernel, eb=eb, nblk=nblk, nitems=nitems),
        grid=(nitems + 2,),
        in_specs=[tok(0),
                  pl.BlockSpec((eb, D_MODEL), lambda e: (item(e, 0)[2], 0)),
                  pl.BlockSpec((1, D_MODEL, eb), lambda e: (item(e, 2)[2], 0, 0)),
                  pl.BlockSpec((1, PEER_HEADS, 2, 1, tm // 128, 8, 128),
                               lambda e: (item(e, 1)[0], 0, 0, item(e, 1)[2] // 2, item(e, 1)[1], 0, 0)),
                  pl.BlockSpec((1, PEER_HEADS, 2, PEER_NKEYS, tm), lambda e: (item(e, 1)[0], 0, 0, 0, item(e, 1)[1])),
                  tok(2),
                  pl.BlockSpec((1, 1, D_MODEL), lambda e: (item(e, 2)[0], 0, 0)),
                  const((1, D_MODEL)), const((1, D_MODEL))],
        out_specs=tok(2),
        out_shape=jax.ShapeDtypeStruct((bsz, t, D_MODEL), F32),
        scratch_shapes=[pltpu.VMEM((D_MODEL, tm), F32),
                        pltpu.VMEM((eb, tm), F32), pltpu.VMEM((eb, tm), F32),
                        pltpu.VMEM((eb, tm), BF16), pltpu.VMEM((eb, tm), BF16)],
        compiler_params=_cparams(("arbitrary",)),
        name="peer_dense",
    )(h2, u_tab, vt_tab, pj, pb, x1, mod, g, b)


def _rope_tables(t):
    half = DIFF_D // 2
    inv = jnp.exp(-math.log(ROPE_THETA) * jnp.arange(half, dtype=F32) / half)
    ang = jnp.arange(t, dtype=F32)[:, None] * inv[None, :]
    cos, sin = jnp.cos(ang), jnp.sin(ang)
    cos128 = jnp.tile(cos, (1, 4))
    sin128 = jnp.concatenate([-sin, -sin, sin, sin], axis=1)
    return cos128, sin128


def _layer_params(l, w_in, gla_w_gk, gla_b_gk, gla_norm_g, na_rpb, diff_lambda, diff_norm_g, w_br, w_o,
                  ln1_g, ln1_b, peer_wq, peer_subkeys, peer_u, peer_v, ln2_g, ln2_b):
    w = w_in[l]
    w_gla = jnp.pad(w[:, :3104], ((0, 0), (0, GLA_COLS - 3104))).astype(BF16)
    w_att = w[:, 3104:3104 + ATT_COLS]
    qscale = np.ones((ATT_COLS,), np.float32)
    qscale[0:1024] = NA_DH ** -0.5
    qscale[3072:4096] = DIFF_D ** -0.5 * math.log2(math.e)
    half = DIFF_D // 2
    lane_src = np.concatenate([np.arange(0, half), np.arange(DIFF_D, DIFF_D + half),
                               np.arange(half, DIFF_D), np.arange(DIFF_D + half, 2 * DIFF_D)])
    cols = np.arange(ATT_COLS)
    for base in (3072, 4096):
        for hd in range(DIFF_HEADS):
            cols[base + hd * 128:base + (hd + 1) * 128] = base + hd * 128 + lane_src
    w_att = (w_att * qscale)[:, cols].astype(BF16)
    w_gt = w[:, 3104 + ATT_COLS:].astype(BF16)
    wgk = jnp.zeros((2, 128, GLA_HEADS * GLA_DK), F32)
    wgk = wgk.at[0, :GLA_RANK].set(gla_w_gk[l, 0]).at[1, GLA_RANK:2 * GLA_RANK].set(gla_w_gk[l, 1]).astype(BF16)
    sk = peer_subkeys[l]
    half = PEER_DKEY // 2
    skp = jnp.zeros((PEER_HEADS, 2, PEER_NKEYS, PEER_DKEY), F32)
    skp = skp.at[:, 0, :, :half].set(sk[:, 0]).at[:, 1, :, half:].set(sk[:, 1])
    return dict(
        w_gla=w_gla, w_att=w_att, w_gt=w_gt, wgk=wgk, bgk=gla_b_gk[l],
        gla_g=gla_norm_g[l][None], na_bias=_na_bias_table(na_rpb[l]),
        lam=diff_lambda[l], diff_g=diff_norm_g[l][None],
        w_br=w_br[l].astype(BF16), w_o=w_o[l].astype(BF16),
        ln1_g=ln1_g[l][None], ln1_b=ln1_b[l][None],
        wq=peer_wq[l].astype(BF16), sk=skp.reshape(2 * PEER_HEADS, PEER_NKEYS, PEER_DKEY).astype(BF16),
        u=peer_u[l].astype(BF16),
        vt=peer_v[l].astype(BF16).reshape(PEER_N // PEER_EB, PEER_EB, D_MODEL).transpose(0, 2, 1),
        ln2_g=ln2_g[l][None], ln2_b=ln2_b[l][None],
        lam_init=0.8 - 0.6 * math.exp(-0.3 * l),
    )


def _layer(x, mod, p, rope, cms):
    pa = _proj(x, mod[:, 0:2], p["w_gla"], F32, tn=GLA_COLS, name="proj_gla")
    pb = _proj(x, mod[:, 0:2], p["w_att"], BF16, tn=1024, rope=rope + ((3, 4),), name="proj_att")
    ofwd = _gla_dir(pa, cms[0], p["wgk"][0], p["bgk"][0:1], reverse=False)
    ya = _gla_dir(pa, cms[1], p["wgk"][1], p["bgk"][1:2], reverse=True, ofwd=ofwd, norm_g=p["gla_g"])
    yb = _na(pb, p["na_bias"])
    yc = _diff(pb, p["lam"], p["diff_g"], p["lam_init"])
    x1 = _merge(x, mod[:, 0:3], ya, yb, yc, p["w_gt"], p["w_br"], p["w_o"], p["ln1_g"], p["ln1_b"])
    h2, pj, pb2 = _peer_route(x1, mod[:, 3:5], p["wq"], p["sk"])
    return _peer_dense(h2, p["u"], p["vt"], pj, pb2, x1, mod[:, 5:6], p["ln2_g"], p["ln2_b"])


def kernel(x_prompt, x_sample, c_prompt, c_sample, w_ada, b_ada, w_in, gla_w_gk, gla_b_gk, gla_norm_g, na_rpb, diff_lambda, diff_norm_g, w_br, w_o, ln1_g, ln1_b, peer_wq, peer_subkeys, peer_u, peer_v, ln2_g, ln2_b):
    nb = x_prompt.shape[0]
    c_all = jnp.concatenate([c_prompt, c_sample], axis=0)
    cms = (jnp.asarray(_gla_masks(False)), jnp.asarray(_gla_masks(True)))
    ropes = {x.shape[1]: _rope_tables(x.shape[1]) for x in (x_prompt, x_sample)}
    xs = [x_prompt, x_sample]
    for l in range(DEPTH):
        p = _layer_params(l, w_in, gla_w_gk, gla_b_gk, gla_norm_g, na_rpb, diff_lambda, diff_norm_g, w_br, w_o,
                          ln1_g, ln1_b, peer_wq, peer_subkeys, peer_u, peer_v, ln2_g, ln2_b)
        mod_all = _ada(c_all, w_ada[l].astype(BF16), b_ada[l][None]).reshape(c_all.shape[0], 6, D_MODEL)
        mods = [mod_all[:nb], mod_all[nb:]]
        xs = [_layer(x, m, p, ropes[x.shape[1]], cms) for x, m in zip(xs, mods)]
    return (xs[0], xs[1])
```
